```python
import math
import jax, jax.numpy as jnp
from jax import lax
import numpy as np

D_MODEL = 2048
BATCH = 1
SEQ = 16384
DEPTH = 1

DEEPNORM_ALPHA = (2.0 * DEPTH) ** 0.25
DEEPNORM_BETA = (8.0 * DEPTH) ** -0.25
LN_EPS = 1e-5
RMS_EPS = 1e-6
ROPE_THETA = 10000.0
NEG_INF = -1e30
Q_BLOCK = 128

D_FF = 5632
FFN_RES_WEIGHT = 0.5

MLA_HEADS = 8
MLA_Q_RANK = 512
MLA_KV_RANK = 256
MLA_NOPE_DIM = 128
MLA_ROPE_DIM = 64
MLA_V_DIM = 128

NSA_HEADS = 8
NSA_KV_HEADS = 2
NSA_HEAD_DIM = 128
NSA_GROUP = NSA_HEADS // NSA_KV_HEADS
CMP_BLOCK = 32
CMP_STRIDE = 16
CMP_HIDDEN = 256
SEL_BLOCK = 64
SEL_TOPK = 16
WINDOW = 512
N_BRANCH = 3
FORCE_BONUS = 1e4

MIX_WIDTH = MLA_HEADS * MLA_V_DIM + NSA_HEADS * NSA_HEAD_DIM
NSA_KV_WIDTH = NSA_KV_HEADS * NSA_HEAD_DIM
IN_SIZES = (MLA_Q_RANK, MLA_KV_RANK, MLA_ROPE_DIM, NSA_HEADS * NSA_HEAD_DIM,
            NSA_KV_WIDTH, NSA_KV_WIDTH, NSA_KV_WIDTH, NSA_KV_WIDTH, NSA_KV_WIDTH, NSA_KV_WIDTH,
            N_BRANCH * NSA_HEADS)
IN_COLS = sum(IN_SIZES)

kernel_name = "hybrid_mla_nsa_macaron_deepnorm"


def layer_norm(x, g, b):
    xf = x.astype(jnp.float32)
    mu = jnp.mean(xf, axis=-1, keepdims=True)
    var = jnp.mean(jnp.square(xf - mu), axis=-1, keepdims=True)
    return ((xf - mu) * lax.rsqrt(var + LN_EPS) * g + b).astype(x.dtype)


def rms_norm(x, g):
    xf = x.astype(jnp.float32)
    ms = jnp.mean(jnp.square(xf), axis=-1, keepdims=True)
    return (xf * lax.rsqrt(ms + RMS_EPS) * g).astype(x.dtype)


def rope(x, pos):
    d = x.shape[-1]
    half = d // 2
    inv = ROPE_THETA ** (-jnp.arange(half, dtype=jnp.float32) * (2.0 / d))
    ang = pos.astype(jnp.float32)[:, None] * inv[None, :]
    cos, sin = jnp.cos(ang), jnp.sin(ang)
    x1 = x[..., :half].astype(jnp.float32)
    x2 = x[..., half:].astype(jnp.float32)
    return jnp.concatenate([x1 * cos - x2 * sin, x2 * cos + x1 * sin], axis=-1).astype(x.dtype)


def swiglu(x, w_gate, w_up, w_down):
    return (jax.nn.silu(x @ w_gate) * (x @ w_up)) @ w_down


def split_cols(h, sizes):
    offsets = [int(v) for v in np.cumsum(sizes)[:-1]]
    return jnp.split(h, offsets, axis=-1)


def to_heads(t, n):
    b, s, _ = t.shape
    return t.reshape(b, s, n, -1).transpose(0, 2, 1, 3)


def masked_probs(s, mask):
    s = jnp.where(mask, s, NEG_INF)
    m = jnp.max(s, axis=-1, keepdims=True)
    p = jnp.where(mask, jnp.exp(s - m), 0.0)
    return p / jnp.maximum(jnp.sum(p, axis=-1, keepdims=True), 1e-30)


def mla_attention(c_q, c_kv, k_rope, q_norm_g, w_uq, kv_norm_g, w_ukv, pos):
    b, s, _ = c_q.shape
    q = (rms_norm(c_q, q_norm_g) @ w_uq).reshape(b, s, MLA_HEADS, MLA_NOPE_DIM + MLA_ROPE_DIM)
    q = q.transpose(0, 2, 1, 3)
    q_nope = q[..., :MLA_NOPE_DIM]
    q_rope = rope(q[..., MLA_NOPE_DIM:], pos)
    kv = (rms_norm(c_kv, kv_norm_g) @ w_ukv).reshape(b, s, MLA_HEADS, MLA_NOPE_DIM + MLA_V_DIM)
    kv = kv.transpose(0, 2, 1, 3)
    k_nope, v = kv[..., :MLA_NOPE_DIM], kv[..., MLA_NOPE_DIM:]
    k_r = rope(k_rope, pos)
    scale = (MLA_NOPE_DIM + MLA_ROPE_DIM) ** -0.5
    kpos = jnp.arange(s)

    def block(i):
        q0 = i * Q_BLOCK
        qn = lax.dynamic_slice_in_dim(q_nope, q0, Q_BLOCK, axis=2)
        qr = lax.dynamic_slice_in_dim(q_rope, q0, Q_BLOCK, axis=2)
        sc = (jnp.einsum("bhqd,bhkd->bhqk", qn, k_nope, preferred_element_type=jnp.float32)
              + jnp.einsum("bhqd,bkd->bhqk", qr, k_r, preferred_element_type=jnp.float32)) * scale
        tq = q0 + jnp.arange(Q_BLOCK)
        mask = tq[:, None] >= kpos[None, :]
        p = jax.nn.softmax(jnp.where(mask, sc, NEG_INF), axis=-1)
        return jnp.einsum("bhqk,bhkd->bhqd", p.astype(v.dtype), v)

    o = lax.map(block, jnp.arange(s // Q_BLOCK))
    return o.transpose(1, 0, 3, 2, 4).reshape(b, s, MLA_HEADS * MLA_V_DIM)


def nsa_attention(q, k_cmp, v_cmp, k_sel, v_sel, k_win, v_win, gate_logits, gate_b,
                  pe_k, w1_k, w2_k, pe_v, w1_v, w2_v, pos):
    b, s, _ = q.shape
    hk, g, d = NSA_KV_HEADS, NSA_GROUP, NSA_HEAD_DIM
    qg = rope(to_heads(q, NSA_HEADS), pos).reshape(b, hk, g, s, d)
    k_cmp = rope(to_heads(k_cmp, hk), pos)
    v_cmp = to_heads(v_cmp, hk)
    k_sel = rope(to_heads(k_sel, hk), pos)
    v_sel = to_heads(v_sel, hk)
    k_win = rope(to_heads(k_win, hk), pos)
    v_win = to_heads(v_win, hk)

    n_cmp = (s - CMP_BLOCK) // CMP_STRIDE + 1
    idx = jnp.arange(n_cmp)[:, None] * CMP_STRIDE + jnp.arange(CMP_BLOCK)[None, :]

    def compress(t, pe, w1, w2):
        blk = t[:, :, idx, :] + pe
        flat = blk.reshape(b, hk, n_cmp, CMP_BLOCK * d)
        return jax.nn.gelu(flat @ w1) @ w2

    kc = compress(k_cmp, pe_k, w1_k, w2_k)
    vc = compress(v_cmp, pe_v, w1_v, w2_v)
    cmp_end = jnp.arange(n_cmp) * CMP_STRIDE + CMP_BLOCK - 1

    n_sel = s // SEL_BLOCK
    top_n = min(SEL_TOPK, n_sel)
    ci = jnp.arange(n_cmp)[:, None] * CMP_STRIDE
    sj = jnp.arange(n_sel)[None, :] * SEL_BLOCK
    ov = jnp.clip(jnp.minimum(ci + CMP_BLOCK, sj + SEL_BLOCK) - jnp.maximum(ci, sj), 0, None)
    ov = ov.astype(jnp.float32) / CMP_STRIDE
    ks_blocks = k_sel.reshape(b, hk, n_sel, SEL_BLOCK, d)
    vs_blocks = v_sel.reshape(b, hk, n_sel, SEL_BLOCK, d)
    gather = jax.vmap(jax.vmap(lambda blocks, ix: blocks[ix]))
    bj = jnp.arange(n_sel)

    kw_pad = jnp.pad(k_win, ((0, 0), (0, 0), (WINDOW, 0), (0, 0)))
    vw_pad = jnp.pad(v_win, ((0, 0), (0, 0), (WINDOW, 0), (0, 0)))

    gates = jax.nn.sigmoid((gate_logits + gate_b).astype(jnp.float32))
    gates = gates.reshape(b, s, N_BRANCH, hk, g).transpose(0, 2, 3, 4, 1)
    scale = d ** -0.5

    def block(i):
        q0 = i * Q_BLOCK
        qb = lax.dynamic_slice_in_dim(qg, q0, Q_BLOCK, axis=3)
        tq = q0 + jnp.arange(Q_BLOCK)
        s_c = jnp.einsum("bhgqd,bhkd->bhgqk", qb, kc, preferred_element_type=jnp.float32) * scale
        p_c = masked_probs(s_c, cmp_end[None, :] <= tq[:, None])
        o_c = jnp.einsum("bhgqk,bhkd->bhgqd", p_c.astype(vc.dtype), vc)
        imp = jnp.einsum("bhgqc,cj->bhqj", p_c, ov)
        cur = tq // SEL_BLOCK
        valid = bj[None, :] * SEL_BLOCK <= tq[:, None]
        forced = (bj[None, :] == 0) | (bj[None, :] == cur[:, None]) | (bj[None, :] == cur[:, None] - 1)
        score = jnp.where(valid, imp + FORCE_BONUS * forced.astype(jnp.float32), NEG_INF)
        top_s, top_j = lax.top_k(score, top_n)
        sel_ok = top_s > 0.5 * NEG_INF
        kg = gather(ks_blocks, top_j).reshape(b, hk, Q_BLOCK, top_n * SEL_BLOCK, d)
        vg = gather(vs_blocks, top_j).reshape(b, hk, Q_BLOCK, top_n * SEL_BLOCK, d)
        key_pos = (top_j[..., None] * SEL_BLOCK + jnp.arange(SEL_BLOCK)).reshape(b, hk, Q_BLOCK, top_n * SEL_BLOCK)
        m_s = (key_pos <= tq[:, None]) & jnp.repeat(sel_ok, SEL_BLOCK, axis=-1)
        s_s = jnp.einsum("bhgqd,bhqkd->bhgqk", qb, kg, preferred_element_type=jnp.float32) * scale
        p_s = masked_probs(s_s, m_s[:, :, None])
        o_s = jnp.einsum("bhgqk,bhqkd->bhgqd", p_s.astype(vg.dtype), vg)
        kw = lax.dynamic_slice_in_dim(kw_pad, q0, WINDOW + Q_BLOCK, axis=2)
        vw = lax.dynamic_slice_in_dim(vw_pad, q0, WINDOW + Q_BLOCK, axis=2)
        wpos = q0 - WINDOW + jnp.arange(WINDOW + Q_BLOCK)
        m_w = (wpos[None, :] <= tq[:, None]) & (wpos[None, :] > tq[:, None] - WINDOW) & (wpos[None, :] >= 0)
        s_w = jnp.einsum("bhgqd,bhkd->bhgqk", qb, kw, preferred_element_type=jnp.float32) * scale
        p_w = masked_probs(s_w, m_w)
        o_w = jnp.einsum("bhgqk,bhkd->bhgqd", p_w.astype(vw.dtype), vw)
        gb = lax.dynamic_slice_in_dim(gates, q0, Q_BLOCK, axis=4).astype(o_c.dtype)
        return gb[:, 0, ..., None] * o_c + gb[:, 1, ..., None] * o_s + gb[:, 2, ..., None] * o_w

    o = lax.map(block, jnp.arange(s // Q_BLOCK))
    return o.transpose(1, 0, 4, 2, 3, 5).reshape(b, s, NSA_HEADS * d)


def setup_inputs(seed: int = 0) -> dict:
    key = jax.random.key(seed)
    ks = iter(jax.random.split(key, 40))

    def nrm(shape, scale):
        return jax.random.normal(next(ks), shape, jnp.float32) * scale

    def gain(shape):
        return 1.0 + nrm(shape, 0.02)

    L = DEPTH
    hd = NSA_HEAD_DIM
    return {
        "x": nrm((BATCH, SEQ, D_MODEL), 1.0),
        "ffn1_w_gate": nrm((L, D_MODEL, D_FF), D_MODEL ** -0.5),
        "ffn1_w_up": nrm((L, D_MODEL, D_FF), D_MODEL ** -0.5),
        "ffn1_w_down": nrm((L, D_FF, D_MODEL), D_FF ** -0.5 * DEEPNORM_BETA),
        "ln1_g": gain((L, D_MODEL)),
        "ln1_b": nrm((L, D_MODEL), 0.02),
        "w_in": nrm((L, D_MODEL, IN_COLS), D_MODEL ** -0.5),
        "mla_q_norm_g": gain((L, MLA_Q_RANK)),
        "mla_w_uq": nrm((L, MLA_Q_RANK, MLA_HEADS * (MLA_NOPE_DIM + MLA_ROPE_DIM)), MLA_Q_RANK ** -0.5),
        "mla_kv_norm_g": gain((L, MLA_KV_RANK)),
        "mla_w_ukv": nrm((L, MLA_KV_RANK, MLA_HEADS * (MLA_NOPE_DIM + MLA_V_DIM)), MLA_KV_RANK ** -0.5),
        "nsa_gate_b": nrm((L, N_BRANCH * NSA_HEADS), 0.02),
        "nsa_cmp_pe_k": nrm((L, CMP_BLOCK, hd), 0.1),
        "nsa_cmp_w1_k": nrm((L, CMP_BLOCK * hd, CMP_HIDDEN), (CMP_BLOCK * hd) ** -0.5),
        "nsa_cmp_w2_k": nrm((L, CMP_HIDDEN, hd), CMP_HIDDEN ** -0.5),
        "nsa_cmp_pe_v": nrm((L, CMP_BLOCK, hd), 0.1),
        "nsa_cmp_w1_v": nrm((L, CMP_BLOCK * hd, CMP_HIDDEN), (CMP_BLOCK * hd) ** -0.5),
        "nsa_cmp_w2_v": nrm((L, CMP_HIDDEN, hd), CMP_HIDDEN ** -0.5),
        "w_out": nrm((L, MIX_WIDTH, D_MODEL), MIX_WIDTH ** -0.5 * DEEPNORM_BETA),
        "ln2_g": gain((L, D_MODEL)),
        "ln2_b": nrm((L, D_MODEL), 0.02),
        "ffn2_w_gate": nrm((L, D_MODEL, D_FF), D_MODEL ** -0.5),
        "ffn2_w_up": nrm((L, D_MODEL, D_FF), D_MODEL ** -0.5),
        "ffn2_w_down": nrm((L, D_FF, D_MODEL), D_FF ** -0.5 * DEEPNORM_BETA),
        "ln3_g": gain((L, D_MODEL)),
        "ln3_b": nrm((L, D_MODEL), 0.02),
    }


def reference(x, ffn1_w_gate, ffn1_w_up, ffn1_w_down, ln1_g, ln1_b, w_in,
              mla_q_norm_g, mla_w_uq, mla_kv_norm_g, mla_w_ukv, nsa_gate_b,
              nsa_cmp_pe_k, nsa_cmp_w1_k, nsa_cmp_w2_k, nsa_cmp_pe_v, nsa_cmp_w1_v, nsa_cmp_w2_v,
              w_out, ln2_g, ln2_b, ffn2_w_gate, ffn2_w_up, ffn2_w_down, ln3_g, ln3_b):
    pos = jnp.arange(x.shape[1])
    for l in range(DEPTH):
        x = layer_norm(DEEPNORM_ALPHA * x + FFN_RES_WEIGHT * swiglu(x, ffn1_w_gate[l], ffn1_w_up[l], ffn1_w_down[l]),
                       ln1_g[l], ln1_b[l])
        (c_q, c_kv, k_rope, q_nsa, k_c, v_c, k_s, v_s, k_w, v_w, g_nsa) = split_cols(x @ w_in[l], IN_SIZES)
        o_mla = mla_attention(c_q, c_kv, k_rope, mla_q_norm_g[l], mla_w_uq[l], mla_kv_norm_g[l], mla_w_ukv[l], pos)
        o_nsa = nsa_attention(q_nsa, k_c, v_c, k_s, v_s, k_w, v_w, g_nsa, nsa_gate_b[l],
                              nsa_cmp_pe_k[l], nsa_cmp_w1_k[l], nsa_cmp_w2_k[l],
                              nsa_cmp_pe_v[l], nsa_cmp_w1_v[l], nsa_cmp_w2_v[l], pos)
        mix = jnp.concatenate([o_mla, o_nsa], axis=-1) @ w_out[l]
        x = layer_norm(DEEPNORM_ALPHA * x + mix, ln2_g[l], ln2_b[l])
        x = layer_norm(DEEPNORM_ALPHA * x + FFN_RES_WEIGHT * swiglu(x, ffn2_w_gate[l], ffn2_w_up[l], ffn2_w_down[l]),
                       ln3_g[l], ln3_b[l])
    return x
```

```python
import functools

import jax
import jax.numpy as jnp
import numpy as np
from jax import lax
from jax.experimental import pallas as pl
from jax.experimental.pallas import tpu as pltpu

F32 = jnp.float32
BF16 = jnp.bfloat16

D_MODEL = 2048
SEQ = 16384
DEPTH = 1
DEEPNORM_ALPHA = (2.0 * DEPTH) ** 0.25
LN_EPS = 1e-5
RMS_EPS = 1e-6
ROPE_THETA = 10000.0
NEG_INF = -1e30
D_FF = 5632
FFN_RES_WEIGHT = 0.5

MLA_HEADS = 8
MLA_Q_RANK = 512
MLA_KV_RANK = 256
MLA_NOPE_DIM = 128
MLA_ROPE_DIM = 64
MLA_V_DIM = 128
MLA_QK_PAD = 256

NSA_HEADS = 8
NSA_KV_HEADS = 2
NSA_HEAD_DIM = 128
NSA_GROUP = NSA_HEADS // NSA_KV_HEADS
CMP_BLOCK = 32
CMP_STRIDE = 16
CMP_HIDDEN = 256
SEL_BLOCK = 64
SEL_TOPK = 16
WINDOW = 512
N_BRANCH = 3
FORCE_BONUS = 1e4
N_CMP = (SEQ - CMP_BLOCK) // CMP_STRIDE + 1
N_CMP_PAD = SEQ // CMP_STRIDE
N_SEL = SEQ // SEL_BLOCK

LANE = 128
MASK_BIAS = -1e9
M_INIT = -1e30

VMEM_LIMIT = 56 * 1024 * 1024

NT_DIMS = (((1,), (1,)), ((), ()))


def _dot(a, b, **kw):
    return jnp.dot(a, b, preferred_element_type=F32, **kw)


def _dot_nt(a, b):
    return lax.dot_general(a, b, NT_DIMS, preferred_element_type=F32)


def _layer_norm(y, g, b):
    mu = jnp.mean(y, axis=-1, keepdims=True)
    yc = y - mu
    var = jnp.mean(yc * yc, axis=-1, keepdims=True)
    return yc * lax.rsqrt(var + LN_EPS) * g + b


def _rms_norm(x, g):
    ms = jnp.mean(x * x, axis=-1, keepdims=True)
    return x * lax.rsqrt(ms + RMS_EPS) * g


FFN_TM = 512
FFN_TF = 512


def _ffn_ln_kernel(x_ref, wg_ref, wu_ref, wd_ref, g_ref, b_ref, o_ref, xb_ref, acc_ref):
    k = pl.program_id(1)

    @pl.when(k == 0)
    def _():
        xb_ref[...] = x_ref[...].astype(BF16)
        acc_ref[...] = jnp.zeros_like(acc_ref)

    xb = xb_ref[...]
    gate = _dot(xb, wg_ref[...])
    up = _dot(xb, wu_ref[...])
    h = (jax.nn.silu(gate) * up).astype(BF16)
    acc_ref[...] += _dot(h, wd_ref[...])

    @pl.when(k == pl.num_programs(1) - 1)
    def _():
        y = DEEPNORM_ALPHA * x_ref[...] + FFN_RES_WEIGHT * acc_ref[...]
        o_ref[...] = _layer_norm(y, g_ref[...], b_ref[...])


def _ffn_ln(x, wg, wu, wd, g, b):
    s, d = x.shape
    f = wg.shape[1]
    grid = (s // FFN_TM, f // FFN_TF)
    return pl.pallas_call(
        _ffn_ln_kernel,
        grid=grid,
        in_specs=[
            pl.BlockSpec((FFN_TM, d), lambda i, k: (i, 0)),
            pl.BlockSpec((d, FFN_TF), lambda i, k: (0, k)),
            pl.BlockSpec((d, FFN_TF), lambda i, k: (0, k)),
            pl.BlockSpec((FFN_TF, d), lambda i, k: (k, 0)),
            pl.BlockSpec((1, d), lambda i, k: (0, 0)),
            pl.BlockSpec((1, d), lambda i, k: (0, 0)),
        ],
        out_specs=pl.BlockSpec((FFN_TM, d), lambda i, k: (i, 0)),
        out_shape=jax.ShapeDtypeStruct((s, d), F32),
        scratch_shapes=[pltpu.VMEM((FFN_TM, d), BF16), pltpu.VMEM((FFN_TM, d), F32)],
        compiler_params=pltpu.CompilerParams(
            dimension_semantics=("parallel", "arbitrary"), vmem_limit_bytes=VMEM_LIMIT),
        name="ffn_ln",
    )(x, wg, wu, wd, g, b)


PROJ_TM = 256
_O_CQ = 0
_O_CKV = _O_CQ + MLA_Q_RANK
_O_NQ = _O_CKV + MLA_KV_RANK
_O_KC = _O_NQ + NSA_HEADS * NSA_HEAD_DIM
_O_VC = _O_KC + NSA_KV_HEADS * NSA_HEAD_DIM
_O_KS = _O_VC + NSA_KV_HEADS * NSA_HEAD_DIM
_O_VS = _O_KS + NSA_KV_HEADS * NSA_HEAD_DIM
_O_KW = _O_VS + NSA_KV_HEADS * NSA_HEAD_DIM
_O_VW = _O_KW + NSA_KV_HEADS * NSA_HEAD_DIM
_O_KR = _O_VW + NSA_KV_HEADS * NSA_HEAD_DIM
_O_KRS = _O_KR + LANE
_O_G = _O_KRS + LANE
_PROJ_COLS = _O_G + NSA_KV_HEADS * LANE

MLA_SCALE = (MLA_NOPE_DIM + MLA_ROPE_DIM) ** -0.5
NSA_SCALE = NSA_HEAD_DIM ** -0.5


def _proj_kernel(x_ref, win_ref, gq_ref, gkv_ref, wuq_ref, wukv_ref, gb_ref,
                 cosm_ref, sinm_ref, cosn_ref, sinn_ref,
                 qm_ref, km_ref, vm_ref, qn_ref, kc_ref, vc_ref, ks_ref, vs_ref, kw_ref, vw_ref, gt_ref):
    tm = x_ref.shape[0]
    xb = x_ref[...].astype(BF16)
    h = _dot(xb, win_ref[...])
    cosm, sinm = cosm_ref[...], sinm_ref[...]
    cosn, sinn = cosn_ref[...], sinn_ref[...]

    cqn = _rms_norm(h[:, _O_CQ:_O_CQ + MLA_Q_RANK], gq_ref[...]).astype(BF16)
    qall = _dot(cqn, wuq_ref[...])
    hw = MLA_HEADS * LANE
    for hd in range(MLA_HEADS):
        sl = slice(hd * LANE, (hd + 1) * LANE)
        qm_ref[hd, :, 0:LANE] = (qall[:, sl] * MLA_SCALE).astype(BF16)
        qr = qall[:, hw + hd * LANE: hw + (hd + 1) * LANE]
        qrs = qall[:, 2 * hw + hd * LANE: 2 * hw + (hd + 1) * LANE]
        qm_ref[hd, :, LANE:2 * LANE] = ((qr * cosm + qrs * sinm) * MLA_SCALE).astype(BF16)

    ckvn = _rms_norm(h[:, _O_CKV:_O_CKV + MLA_KV_RANK], gkv_ref[...]).astype(BF16)
    kv = _dot(ckvn, wukv_ref[...])
    kr = (h[:, _O_KR:_O_KR + LANE] * cosm + h[:, _O_KRS:_O_KRS + LANE] * sinm).astype(BF16)
    for hd in range(MLA_HEADS):
        km_ref[hd, :, 0:LANE] = kv[:, hd * LANE:(hd + 1) * LANE].astype(BF16)
        km_ref[hd, :, LANE:2 * LANE] = kr
        vm_ref[hd] = kv[:, hw + hd * LANE: hw + (hd + 1) * LANE].astype(BF16)

    def rope128(t):
        return t * cosn + pltpu.roll(t, NSA_HEAD_DIM // 2, 1) * sinn

    for hd in range(NSA_HEADS):
        t = h[:, _O_NQ + hd * LANE:_O_NQ + (hd + 1) * LANE]
        qn_ref[hd] = (rope128(t) * NSA_SCALE).astype(BF16)

    row0 = pl.program_id(0) * tm
    kpos = row0 + lax.broadcasted_iota(jnp.int32, (tm, LANE), 0)
    lane = lax.broadcasted_iota(jnp.int32, (tm, LANE), 1)
    onehot = jnp.where(((kpos // SEL_BLOCK) % LANE) == lane, 1.0, 0.0).astype(BF16)
    for hk in range(NSA_KV_HEADS):
        sl = lambda off: slice(off + hk * LANE, off + (hk + 1) * LANE)
        kc_ref[hk] = rope128(h[:, sl(_O_KC)])
        vc_ref[hk] = h[:, sl(_O_VC)]
        ks_ref[hk, :, 0:LANE] = rope128(h[:, sl(_O_KS)]).astype(BF16)
        ks_ref[hk, :, LANE:2 * LANE] = onehot
        vs_ref[hk] = h[:, sl(_O_VS)].astype(BF16)
        kw_ref[hk] = rope128(h[:, sl(_O_KW)]).astype(BF16)
        vw_ref[hk] = h[:, sl(_O_VW)].astype(BF16)
        gt_ref[hk] = jax.nn.sigmoid(h[:, sl(_O_G)] + gb_ref[hk])


def _const_spec(shape):
    nd = len(shape)
    return pl.BlockSpec(shape, lambda i: (0,) * nd)


def _proj(x1, win_p, gq, gkv, wuq_all, wukv_p, gb_p, cosm, sinm, cosn, sinn):
    s, d = x1.shape
    tm = PROJ_TM
    grid = (s // tm,)
    row_spec = lambda w: pl.BlockSpec((tm, w), lambda i: (i, 0))
    head_spec = lambda nh, w: pl.BlockSpec((nh, tm, w), lambda i: (0, i, 0))
    out_shape = (
        jax.ShapeDtypeStruct((MLA_HEADS, s, MLA_QK_PAD), BF16),
        jax.ShapeDtypeStruct((MLA_HEADS, s, MLA_QK_PAD), BF16),
        jax.ShapeDtypeStruct((MLA_HEADS, s, MLA_V_DIM), BF16),
        jax.ShapeDtypeStruct((NSA_HEADS, s, NSA_HEAD_DIM), BF16),
        jax.ShapeDtypeStruct((NSA_KV_HEADS, s, NSA_HEAD_DIM), F32),
        jax.ShapeDtypeStruct((NSA_KV_HEADS, s, NSA_HEAD_DIM), F32),
        jax.ShapeDtypeStruct((NSA_KV_HEADS, s, 2 * LANE), BF16),
        jax.ShapeDtypeStruct((NSA_KV_HEADS, s, NSA_HEAD_DIM), BF16),
        jax.ShapeDtypeStruct((NSA_KV_HEADS, s, NSA_HEAD_DIM), BF16),
        jax.ShapeDtypeStruct((NSA_KV_HEADS, s, NSA_HEAD_DIM), BF16),
        jax.ShapeDtypeStruct((NSA_KV_HEADS, s, LANE), F32),
    )
    out_specs = (
        head_spec(MLA_HEADS, MLA_QK_PAD), head_spec(MLA_HEADS, MLA_QK_PAD), head_spec(MLA_HEADS, MLA_V_DIM),
        head_spec(NSA_HEADS, NSA_HEAD_DIM),
        head_spec(NSA_KV_HEADS, NSA_HEAD_DIM), head_spec(NSA_KV_HEADS, NSA_HEAD_DIM),
        head_spec(NSA_KV_HEADS, 2 * LANE), head_spec(NSA_KV_HEADS, NSA_HEAD_DIM),
        head_spec(NSA_KV_HEADS, NSA_HEAD_DIM), head_spec(NSA_KV_HEADS, NSA_HEAD_DIM),
        head_spec(NSA_KV_HEADS, LANE),
    )
    return pl.pallas_call(
        _proj_kernel,
        grid=grid,
        in_specs=[
            row_spec(d),
            _const_spec(win_p.shape), _const_spec(gq.shape), _const_spec(gkv.shape),
            _const_spec(wuq_all.shape), _const_spec(wukv_p.shape), _const_spec(gb_p.shape),
            row_spec(LANE), row_spec(LANE), row_spec(LANE), row_spec(LANE),
        ],
        out_specs=out_specs,
        out_shape=out_shape,
        compiler_params=pltpu.CompilerParams(
            dimension_semantics=("parallel",), vmem_limit_bytes=VMEM_LIMIT),
        name="proj",
    )(x1, win_p, gq, gkv, wuq_all, wukv_p, gb_p, cosm, sinm, cosn, sinn)


_CMP_HALF = CMP_STRIDE * NSA_HEAD_DIM


def _compress_kernel(t_ref, pe_ref, w1_ref, w2_ref, o_ref):
    hi = lax.Precision.HIGHEST
    t = t_ref[0]
    w1a = w1_ref[0:_CMP_HALF, :]
    w1b = w1_ref[_CMP_HALF:2 * _CMP_HALF, :]
    first = _dot(t, w1a, precision=hi)
    second = _dot(t, w1b, precision=hi)
    pe_row = _dot(pe_ref[...], w1_ref[...], precision=hi)[0:1]
    hid = first + pltpu.roll(second, N_CMP_PAD - 1, 0) + pe_row
    o_ref[0] = _dot(jax.nn.gelu(hid), w2_ref[...], precision=hi)


def _compress(t16, pe_flat8, w1, w2):
    return pl.pallas_call(
        _compress_kernel,
        grid=(NSA_KV_HEADS,),
        in_specs=[
            pl.BlockSpec((1, N_CMP_PAD, _CMP_HALF), lambda i: (i, 0, 0)),
            _const_spec(pe_flat8.shape), _const_spec(w1.shape), _const_spec(w2.shape),
        ],
        out_specs=pl.BlockSpec((1, N_CMP_PAD, NSA_HEAD_DIM), lambda i: (i, 0, 0)),
        out_shape=jax.ShapeDtypeStruct((NSA_KV_HEADS, N_CMP_PAD, NSA_HEAD_DIM), F32),
        compiler_params=pltpu.CompilerParams(
            dimension_semantics=("parallel",), vmem_limit_bytes=VMEM_LIMIT),
        name="compress",
    )(t16, pe_flat8, w1, w2)


def _flash_init(m_ref, l_ref, acc_ref):
    m_ref[...] = jnp.full_like(m_ref, M_INIT)
    l_ref[...] = jnp.zeros_like(l_ref)
    acc_ref[...] = jnp.zeros_like(acc_ref)


def _flash_step(q, k, v, m_ref, l_ref, acc_ref, mask=None):
    s = _dot_nt(q, k)
    if mask is not None:
        s = jnp.where(mask, s, MASK_BIAS)
    m_prev = m_ref[...]
    m_new = jnp.maximum(m_prev, jnp.max(s, axis=1, keepdims=True))
    alpha = jnp.exp(m_prev - m_new)
    p = jnp.exp(s - m_new)
    l_ref[...] = alpha * l_ref[...] + jnp.sum(p, axis=1, keepdims=True)
    acc_ref[...] = alpha * acc_ref[...] + _dot(p.astype(BF16), v)
    m_ref[...] = m_new


def _masked_probs(s, mask):
    s = jnp.where(mask, s, NEG_INF)
    m = jnp.max(s, axis=-1, keepdims=True)
    p = jnp.where(mask, jnp.exp(s - m), 0.0)
    return p * (1.0 / jnp.maximum(jnp.sum(p, axis=-1, keepdims=True), 1e-30))


MLA_TQ = 512
MLA_TK = 512


def _mla_kernel(q_ref, k_ref, v_ref, o_ref, m_ref, l_ref, acc_ref):
    i = pl.program_id(1)
    q = q_ref[0]
    _flash_init(m_ref, l_ref, acc_ref)

    def body(kt, carry):
        k0 = pl.multiple_of(kt * MLA_TK, MLA_TK)
        _flash_step(q, k_ref[0, pl.ds(k0, MLA_TK), :], v_ref[0, pl.ds(k0, MLA_TK), :], m_ref, l_ref, acc_ref)
        return carry

    lax.fori_loop(0, i, body, 0)
    k0 = pl.multiple_of(i * MLA_TK, MLA_TK)
    row = lax.broadcasted_iota(jnp.int32, (MLA_TQ, MLA_TK), 0)
    col = lax.broadcasted_iota(jnp.int32, (MLA_TQ, MLA_TK), 1)
    _flash_step(q, k_ref[0, pl.ds(k0, MLA_TK), :], v_ref[0, pl.ds(k0, MLA_TK), :], m_ref, l_ref, acc_ref,
                mask=row >= col)
    o_ref[...] = (acc_ref[...] * (1.0 / l_ref[...])).astype(o_ref.dtype)


def _mla(q, k, v):
    nh, s, dq = q.shape
    dv = v.shape[2]
    assert MLA_TQ == MLA_TK
    return pl.pallas_call(
        _mla_kernel,
        grid=(nh, s // MLA_TQ),
        in_specs=[
            pl.BlockSpec((1, MLA_TQ, dq), lambda h, i: (h, i, 0)),
            pl.BlockSpec((1, s, dq), lambda h, i: (h, 0, 0)),
            pl.BlockSpec((1, s, dv), lambda h, i: (h, 0, 0)),
        ],
        out_specs=pl.BlockSpec((MLA_TQ, dv), lambda h, i: (i, h)),
        out_shape=jax.ShapeDtypeStruct((s, nh * dv), BF16),
        scratch_shapes=[pltpu.VMEM((MLA_TQ, 1), F32), pltpu.VMEM((MLA_TQ, 1), F32),
                        pltpu.VMEM((MLA_TQ, dv), F32)],
        compiler_params=pltpu.CompilerParams(
            dimension_semantics=("parallel", "arbitrary"), vmem_limit_bytes=VMEM_LIMIT),
        name="mla",
    )(q, k, v)


NSA_TQ = 128
NSA_ROWS = NSA_GROUP * NSA_TQ
NSA_TK = 512
_SEL_HALF_KEYS = LANE * SEL_BLOCK
_WIN_TILES = WINDOW // NSA_TQ + 1


def _nsa_kernel(q_ref, kc_ref, vc_ref, ov_ref, ks_ref, vs_ref, *rest):
    kw_refs = rest[:_WIN_TILES]
    vw_refs = rest[_WIN_TILES:2 * _WIN_TILES]
    gt_ref, o_ref, qaug_ref, m_ref, l_ref, acc_ref = rest[2 * _WIN_TILES:]

    i = pl.program_id(1)
    q0 = i * NSA_TQ
    q = q_ref[...].reshape(NSA_ROWS, NSA_HEAD_DIM)
    row = lax.broadcasted_iota(jnp.int32, (NSA_ROWS, 1), 0)
    tq = q0 + (row & (NSA_TQ - 1))

    kc = kc_ref[0].astype(BF16)
    vc = vc_ref[0].astype(BF16)
    s_c = _dot_nt(q, kc)
    cidx = lax.broadcasted_iota(jnp.int32, (1, N_CMP_PAD), 1)
    p_c = _masked_probs(s_c, (cidx * CMP_STRIDE + (CMP_BLOCK - 1)) <= tq)
    o_c = _dot(p_c.astype(BF16), vc)

    psum = p_c[0:NSA_TQ]
    for g in range(1, NSA_GROUP):
        psum = psum + p_c[g * NSA_TQ:(g + 1) * NSA_TQ]
    imp = _dot(psum, ov_ref[...], precision=lax.Precision.HIGHEST)
    tq1 = q0 + lax.broadcasted_iota(jnp.int32, (NSA_TQ, 1), 0)
    bj = lax.broadcasted_iota(jnp.int32, (1, N_SEL), 1)
    cur = tq1 // SEL_BLOCK
    valid = bj * SEL_BLOCK <= tq1
    forced = (bj == 0) | (bj == cur) | (bj == cur - 1)
    score = jnp.where(valid, imp + FORCE_BONUS * jnp.where(forced, 1.0, 0.0), NEG_INF)
    bjf = bj.astype(F32)
    work = score
    picked = jnp.zeros(score.shape, F32)
    for _ in range(SEL_TOPK):
        mx = jnp.max(work, axis=1, keepdims=True)
        first = jnp.min(jnp.where(work == mx, bjf, float(N_SEL)), axis=1, keepdims=True)
        hit = bjf == first
        picked = jnp.where(hit, 1.0, picked)
        work = jnp.where(hit, -3e38, work)
    selected = (picked > 0.5) & (score > 0.5 * NEG_INF)
    bias = jnp.where(selected, 0.0, MASK_BIAS).astype(BF16)
    bias = jnp.concatenate([bias] * NSA_GROUP, axis=0)
    for half in range(N_SEL // LANE):
        qaug_ref[half, :, 0:NSA_HEAD_DIM] = q
        qaug_ref[half, :, NSA_HEAD_DIM:NSA_HEAD_DIM + LANE] = bias[:, half * LANE:(half + 1) * LANE]

    _flash_init(m_ref, l_ref, acc_ref)
    last = (q0 + NSA_TQ - 1) // NSA_TK

    def sel_step(kt, mask):
        k0 = pl.multiple_of(kt * NSA_TK, NSA_TK)
        qa = qaug_ref[kt // (_SEL_HALF_KEYS // NSA_TK)]
        _flash_step(qa, ks_ref[0, pl.ds(k0, NSA_TK), :], vs_ref[0, pl.ds(k0, NSA_TK), :],
                    m_ref, l_ref, acc_ref, mask=mask)

    def body(kt, carry):
        sel_step(kt, None)
        return carry

    lax.fori_loop(0, last, body, 0)
    kpos = last * NSA_TK + lax.broadcasted_iota(jnp.int32, (1, NSA_TK), 1)
    sel_step(last, kpos <= tq)
    o_s = acc_ref[...] * (1.0 / l_ref[...])

    kw = jnp.concatenate([r[0] for r in kw_refs], axis=0)
    vw = jnp.concatenate([r[0] for r in vw_refs], axis=0)
    s_w = _dot_nt(q, kw)
    wpos = q0 - WINDOW + lax.broadcasted_iota(jnp.int32, (1, WINDOW + NSA_TQ), 1)
    m_w = (wpos <= tq) & (wpos > tq - WINDOW) & (wpos >= 0)
    p_w = _masked_probs(s_w, m_w)
    o_w = _dot(p_w.astype(BF16), vw)

    gt = gt_ref[0]
    for g in range(NSA_GROUP):
        rs = slice(g * NSA_TQ, (g + 1) * NSA_TQ)
        out = (gt[:, g:g + 1] * o_c[rs]
               + gt[:, NSA_GROUP + g:NSA_GROUP + g + 1] * o_s[rs]
               + gt[:, 2 * NSA_GROUP + g:2 * NSA_GROUP + g + 1] * o_w[rs])
        o_ref[:, g * NSA_HEAD_DIM:(g + 1) * NSA_HEAD_DIM] = out.astype(o_ref.dtype)


def _nsa(q, kc, vc, ov, ks, vs, kw, vw, gates):
    s = q.shape[1]
    d = NSA_HEAD_DIM

    def win_spec(j):
        back = _WIN_TILES - 1 - j
        return pl.BlockSpec((1, NSA_TQ, d), lambda h, i: (h, jnp.maximum(i - back, 0), 0))

    win_specs = [win_spec(j) for j in range(_WIN_TILES)]
    return pl.pallas_call(
        _nsa_kernel,
        grid=(NSA_KV_HEADS, s // NSA_TQ),
        in_specs=[
            pl.BlockSpec((NSA_GROUP, NSA_TQ, d), lambda h, i: (h, i, 0)),
            pl.BlockSpec((1, N_CMP_PAD, d), lambda h, i: (h, 0, 0)),
            pl.BlockSpec((1, N_CMP_PAD, d), lambda h, i: (h, 0, 0)),
            pl.BlockSpec((N_CMP_PAD, N_SEL), lambda h, i: (0, 0)),
            pl.BlockSpec((1, s, 2 * LANE), lambda h, i: (h, 0, 0)),
            pl.BlockSpec((1, s, d), lambda h, i: (h, 0, 0)),
            *win_specs, *win_specs,
            pl.BlockSpec((1, NSA_TQ, LANE), lambda h, i: (h, i, 0)),
        ],
        out_specs=pl.BlockSpec((NSA_TQ, NSA_GROUP * d), lambda h, i: (i, h)),
        out_shape=jax.ShapeDtypeStruct((s, NSA_HEADS * d), BF16),
        scratch_shapes=[pltpu.VMEM((N_SEL // LANE, NSA_ROWS, NSA_HEAD_DIM + LANE), BF16),
                        pltpu.VMEM((NSA_ROWS, 1), F32), pltpu.VMEM((NSA_ROWS, 1), F32),
                        pltpu.VMEM((NSA_ROWS, d), F32)],
        compiler_params=pltpu.CompilerParams(
            dimension_semantics=("parallel", "arbitrary"), vmem_limit_bytes=VMEM_LIMIT),
        name="nsa",
    )(q, kc, vc, ov, ks, vs, *([kw] * _WIN_TILES), *([vw] * _WIN_TILES), gates)


OUT_TM = 512


def _out_ln_kernel(om_ref, on_ref, wa_ref, wb_ref, x_ref, g_ref, b_ref, o_ref):
    mix = _dot(om_ref[...], wa_ref[...]) + _dot(on_ref[...], wb_ref[...])
    o_ref[...] = _layer_norm(DEEPNORM_ALPHA * x_ref[...] + mix, g_ref[...], b_ref[...])


def _out_ln(o_mla, o_nsa, wa, wb, x1, g, b):
    s, d = x1.shape
    tm = OUT_TM
    return pl.pallas_call(
        _out_ln_kernel,
        grid=(s // tm,),
        in_specs=[
            pl.BlockSpec((tm, o_mla.shape[1]), lambda i: (i, 0)),
            pl.BlockSpec((tm, o_nsa.shape[1]), lambda i: (i, 0)),
            _const_spec(wa.shape), _const_spec(wb.shape),
            pl.BlockSpec((tm, d), lambda i: (i, 0)),
            _const_spec(g.shape), _const_spec(b.shape),
        ],
        out_specs=pl.BlockSpec((tm, d), lambda i: (i, 0)),
        out_shape=jax.ShapeDtypeStruct((s, d), F32),
        compiler_params=pltpu.CompilerParams(
            dimension_semantics=("parallel",), vmem_limit_bytes=VMEM_LIMIT),
        name="out_ln",
    )(o_mla, o_nsa, wa, wb, x1, g, b)


def _pad_cols(w, n):
    return jnp.pad(w, ((0, 0), (0, n - w.shape[1])))


def _relayout_w_in(w_in, gate_b):
    sizes = (MLA_Q_RANK, MLA_KV_RANK, MLA_ROPE_DIM, NSA_HEADS * NSA_HEAD_DIM) + (NSA_KV_HEADS * NSA_HEAD_DIM,) * 6 \
        + (N_BRANCH * NSA_HEADS,)
    offs = [int(v) for v in np.cumsum(sizes)[:-1]]
    c_q, c_kv, k_r, q_n, k_c, v_c, k_s, v_s, k_w, v_w, g = jnp.split(w_in, offs, axis=1)
    half = MLA_ROPE_DIM // 2
    k_r_sw = jnp.concatenate([k_r[:, half:], k_r[:, :half]], axis=1)
    gate_idx = np.array([[br * NSA_HEADS + hk * NSA_GROUP + gi for br in range(N_BRANCH) for gi in range(NSA_GROUP)]
                         for hk in range(NSA_KV_HEADS)])
    g_chunks = [_pad_cols(g[:, gate_idx[hk]], LANE) for hk in range(NSA_KV_HEADS)]
    w = jnp.concatenate([c_q, c_kv, q_n, k_c, v_c, k_s, v_s, k_w, v_w,
                         _pad_cols(k_r, LANE), _pad_cols(k_r_sw, LANE)] + g_chunks, axis=1)
    assert w.shape[1] == _PROJ_COLS
    gb = jnp.stack([jnp.pad(gate_b[gate_idx[hk]], (0, LANE - gate_idx.shape[1])) for hk in range(NSA_KV_HEADS)])
    return w.astype(BF16), gb.reshape(NSA_KV_HEADS, 1, LANE)


def _relayout_w_uq(w_uq):
    dh = MLA_NOPE_DIM + MLA_ROPE_DIM
    half = MLA_ROPE_DIM // 2
    w = w_uq.reshape(MLA_Q_RANK, MLA_HEADS, dh)
    nope = w[:, :, :MLA_NOPE_DIM]
    r = w[:, :, MLA_NOPE_DIM:]
    r_sw = jnp.concatenate([r[:, :, half:], r[:, :, :half]], axis=2)
    pad = lambda t: jnp.pad(t, ((0, 0), (0, 0), (0, LANE - MLA_ROPE_DIM)))
    parts = [nope, pad(r), pad(r_sw)]
    return jnp.concatenate([p.reshape(MLA_Q_RANK, MLA_HEADS * LANE) for p in parts], axis=1).astype(BF16)


def _relayout_w_ukv(w_ukv):
    w = w_ukv.reshape(MLA_KV_RANK, MLA_HEADS, MLA_NOPE_DIM + MLA_V_DIM)
    k = w[:, :, :MLA_NOPE_DIM].reshape(MLA_KV_RANK, MLA_HEADS * MLA_NOPE_DIM)
    v = w[:, :, MLA_NOPE_DIM:].reshape(MLA_KV_RANK, MLA_HEADS * MLA_V_DIM)
    return jnp.concatenate([k, v], axis=1).astype(BF16)


def _rope_tables(s):
    pos = jnp.arange(s).astype(F32)[:, None]

    def cs(d):
        half = d // 2
        inv = ROPE_THETA ** (-jnp.arange(half, dtype=F32) * (2.0 / d))
        ang = pos * inv[None, :]
        return jnp.cos(ang), jnp.sin(ang)

    cm, sm = cs(MLA_ROPE_DIM)
    cn, sn = cs(NSA_HEAD_DIM)
    cosm = _pad_cols(jnp.concatenate([cm, cm], axis=1), LANE)
    sinm = _pad_cols(jnp.concatenate([-sm, sm], axis=1), LANE)
    cosn = jnp.concatenate([cn, cn], axis=1)
    sinn = jnp.concatenate([-sn, sn], axis=1)
    return cosm, sinm, cosn, sinn


def _overlap_weights():
    ci = np.arange(N_CMP_PAD)[:, None] * CMP_STRIDE
    sj = np.arange(N_SEL)[None, :] * SEL_BLOCK
    ov = np.clip(np.minimum(ci + CMP_BLOCK, sj + SEL_BLOCK) - np.maximum(ci, sj), 0, None)
    ov[N_CMP:] = 0
    return jnp.asarray(ov.astype(np.float32) / CMP_STRIDE)


def kernel(x, ffn1_w_gate, ffn1_w_up, ffn1_w_down, ln1_g, ln1_b, w_in, mla_q_norm_g, mla_w_uq, mla_kv_norm_g,
           mla_w_ukv, nsa_gate_b, nsa_cmp_pe_k, nsa_cmp_w1_k, nsa_cmp_w2_k, nsa_cmp_pe_v, nsa_cmp_w1_v,
           nsa_cmp_w2_v, w_out, ln2_g, ln2_b, ffn2_w_gate, ffn2_w_up, ffn2_w_down, ln3_g, ln3_b):
    b, s, d = x.shape
    assert (b, s, d) == (1, SEQ, D_MODEL) and ffn1_w_gate.shape[0] == DEPTH
    l = 0
    row = lambda v: v.reshape(1, -1)
    xs = x.reshape(s, d)

    x1 = _ffn_ln(xs, ffn1_w_gate[l].astype(BF16), ffn1_w_up[l].astype(BF16), ffn1_w_down[l].astype(BF16),
                 row(ln1_g[l]), row(ln1_b[l]))

    win_p, gb_p = _relayout_w_in(w_in[l], nsa_gate_b[l])
    cosm, sinm, cosn, sinn = _rope_tables(s)
    (q_m, k_m, v_m, q_n, kc_in, vc_in, ks, vs, kw, vw, gates) = _proj(
        x1, win_p, row(mla_q_norm_g[l]), row(mla_kv_norm_g[l]),
        _relayout_w_uq(mla_w_uq[l]), _relayout_w_ukv(mla_w_ukv[l]), gb_p, cosm, sinm, cosn, sinn)

    def compress(t, pe, w1, w2):
        t16 = t.reshape(NSA_KV_HEADS, N_CMP_PAD, _CMP_HALF)
        pe8 = jnp.broadcast_to(pe.reshape(1, CMP_BLOCK * NSA_HEAD_DIM), (8, CMP_BLOCK * NSA_HEAD_DIM))
        return _compress(t16, pe8, w1, w2)

    kc = compress(kc_in, nsa_cmp_pe_k[l], nsa_cmp_w1_k[l], nsa_cmp_w2_k[l])
    vc = compress(vc_in, nsa_cmp_pe_v[l], nsa_cmp_w1_v[l], nsa_cmp_w2_v[l])

    o_mla = _mla(q_m, k_m, v_m)
    o_nsa = _nsa(q_n, kc, vc, _overlap_weights(), ks, vs, kw, vw, gates)

    n_mla = MLA_HEADS * MLA_V_DIM
    w_o = w_out[l].astype(BF16)
    x2 = _out_ln(o_mla, o_nsa, w_o[:n_mla], w_o[n_mla:], x1, row(ln2_g[l]), row(ln2_b[l]))

    x3 = _ffn_ln(x2, ffn2_w_gate[l].astype(BF16), ffn2_w_up[l].astype(BF16), ffn2_w_down[l].astype(BF16),
                 row(ln3_g[l]), row(ln3_b[l]))
    return x3.reshape(b, s, d)
```

```python
import functools

import jax
import jax.numpy as jnp
import numpy as np
from jax import lax
from jax.experimental import pallas as pl
from jax.experimental.pallas import tpu as pltpu

F32 = jnp.float32
BF16 = jnp.bfloat16

D_MODEL = 2048
SEQ = 16384
DEPTH = 1
DEEPNORM_ALPHA = (2.0 * DEPTH) ** 0.25
LN_EPS = 1e-5
RMS_EPS = 1e-6
ROPE_THETA = 10000.0
NEG_INF = -1e30
D_FF = 5632
FFN_RES_WEIGHT = 0.5

MLA_HEADS = 8
MLA_Q_RANK = 512
MLA_KV_RANK = 256
MLA_NOPE_DIM = 128
MLA_ROPE_DIM = 64
MLA_V_DIM = 128
MLA_QK_PAD = 256

NSA_HEADS = 8
NSA_KV_HEADS = 2
NSA_HEAD_DIM = 128
NSA_GROUP = NSA_HEADS // NSA_KV_HEADS
CMP_BLOCK = 32
CMP_STRIDE = 16
CMP_HIDDEN = 256
SEL_BLOCK = 64
SEL_TOPK = 16
WINDOW = 512
N_BRANCH = 3
FORCE_BONUS = 1e4
N_CMP = (SEQ - CMP_BLOCK) // CMP_STRIDE + 1
N_CMP_PAD = SEQ // CMP_STRIDE
N_SEL = SEQ // SEL_BLOCK

LANE = 128
MASK_BIAS = -1e9
M_INIT = -1e30

VMEM_LIMIT = 56 * 1024 * 1024

NT_DIMS = (((1,), (1,)), ((), ()))


def _dot(a, b, **kw):
    return jnp.dot(a, b, preferred_element_type=F32, **kw)


def _dot_nt(a, b):
    return lax.dot_general(a, b, NT_DIMS, preferred_element_type=F32)


def _layer_norm(y, g, b):
    mu = jnp.mean(y, axis=-1, keepdims=True)
    yc = y - mu
    var = jnp.mean(yc * yc, axis=-1, keepdims=True)
    return yc * lax.rsqrt(var + LN_EPS) * g + b


def _rms_norm(x, g):
    ms = jnp.mean(x * x, axis=-1, keepdims=True)
    return x * lax.rsqrt(ms + RMS_EPS) * g


FFN_TM = 512
FFN_TF = 512


def _ffn_ln_kernel(x_ref, wg_ref, wu_ref, wd_ref, g_ref, b_ref, o_ref, xb_ref, acc_ref):
    k = pl.program_id(1)

    @pl.when(k == 0)
    def _():
        xb_ref[...] = x_ref[...].astype(BF16)
        acc_ref[...] = jnp.zeros_like(acc_ref)

    xb = xb_ref[...]
    gate = _dot(xb, wg_ref[...])
    up = _dot(xb, wu_ref[...])
    h = (jax.nn.silu(gate) * up).astype(BF16)
    acc_ref[...] += _dot(h, wd_ref[...])

    @pl.when(k == pl.num_programs(1) - 1)
    def _():
        y = DEEPNORM_ALPHA * x_ref[...] + FFN_RES_WEIGHT * acc_ref[...]
        o_ref[...] = _layer_norm(y, g_ref[...], b_ref[...])


def _ffn_ln(x, wg, wu, wd, g, b):
    s, d = x.shape
    f = wg.shape[1]
    grid = (s // FFN_TM, f // FFN_TF)
    return pl.pallas_call(
        _ffn_ln_kernel,
        grid=grid,
        in_specs=[
            pl.BlockSpec((FFN_TM, d), lambda i, k: (i, 0)),
            pl.BlockSpec((d, FFN_TF), lambda i, k: (0, k)),
            pl.BlockSpec((d, FFN_TF), lambda i, k: (0, k)),
            pl.BlockSpec((FFN_TF, d), lambda i, k: (k, 0)),
            pl.BlockSpec((1, d), lambda i, k: (0, 0)),
            pl.BlockSpec((1, d), lambda i, k: (0, 0)),
        ],
        out_specs=pl.BlockSpec((FFN_TM, d), lambda i, k: (i, 0)),
        out_shape=jax.ShapeDtypeStruct((s, d), F32),
        scratch_shapes=[pltpu.VMEM((FFN_TM, d), BF16), pltpu.VMEM((FFN_TM, d), F32)],
        compiler_params=pltpu.CompilerParams(
            dimension_semantics=("parallel", "arbitrary"), vmem_limit_bytes=VMEM_LIMIT),
        name="ffn_ln",
    )(x, wg, wu, wd, g, b)


PROJ_TM = 256
_O_CQ = 0
_O_CKV = _O_CQ + MLA_Q_RANK
_O_NQ = _O_CKV + MLA_KV_RANK
_O_KC = _O_NQ + NSA_HEADS * NSA_HEAD_DIM
_O_VC = _O_KC + NSA_KV_HEADS * NSA_HEAD_DIM
_O_KS = _O_VC + NSA_KV_HEADS * NSA_HEAD_DIM
_O_VS = _O_KS + NSA_KV_HEADS * NSA_HEAD_DIM
_O_KW = _O_VS + NSA_KV_HEADS * NSA_HEAD_DIM
_O_VW = _O_KW + NSA_KV_HEADS * NSA_HEAD_DIM
_O_KR = _O_VW + NSA_KV_HEADS * NSA_HEAD_DIM
_O_KRS = _O_KR + LANE
_O_G = _O_KRS + LANE
_PROJ_COLS = _O_G + NSA_KV_HEADS * LANE
GATE_ROWS = 16

LOG2E = 1.4426950408889634
MLA_SCALE = (MLA_NOPE_DIM + MLA_ROPE_DIM) ** -0.5 * LOG2E
NSA_SCALE = NSA_HEAD_DIM ** -0.5 * LOG2E


def _proj_kernel(x_ref, win_ref, gq_ref, gkv_ref, wuq_ref, wukv_ref, gb_ref,
                 cosm_ref, sinm_ref, cosn_ref, sinn_ref,
                 qm_ref, km_ref, vm_ref, qn_ref, kc_ref, vc_ref, ks_ref, vs_ref, kw_ref, vw_ref, gt_ref):
    tm = x_ref.shape[0]
    xb = x_ref[...].astype(BF16)
    h = _dot(xb, win_ref[...])
    cosm, sinm = cosm_ref[...], sinm_ref[...]
    cosn, sinn = cosn_ref[...], sinn_ref[...]

    cqn = _rms_norm(h[:, _O_CQ:_O_CQ + MLA_Q_RANK], gq_ref[...]).astype(BF16)
    qall = _dot(cqn, wuq_ref[...])
    hw = MLA_HEADS * LANE
    for hd in range(MLA_HEADS):
        sl = slice(hd * LANE, (hd + 1) * LANE)
        qm_ref[hd, :, 0:LANE] = (qall[:, sl] * MLA_SCALE).astype(BF16)
        qr = qall[:, hw + hd * LANE: hw + (hd + 1) * LANE]
        qrs = qall[:, 2 * hw + hd * LANE: 2 * hw + (hd + 1) * LANE]
        qm_ref[hd, :, LANE:2 * LANE] = ((qr * cosm + qrs * sinm) * MLA_SCALE).astype(BF16)

    ckvn = _rms_norm(h[:, _O_CKV:_O_CKV + MLA_KV_RANK], gkv_ref[...]).astype(BF16)
    kv = _dot(ckvn, wukv_ref[...])
    kr = (h[:, _O_KR:_O_KR + LANE] * cosm + h[:, _O_KRS:_O_KRS + LANE] * sinm).astype(BF16)
    for hd in range(MLA_HEADS):
        km_ref[hd, :, 0:LANE] = kv[:, hd * LANE:(hd + 1) * LANE].astype(BF16)
        km_ref[hd, :, LANE:2 * LANE] = kr
        vm_ref[hd] = kv[:, hw + hd * LANE: hw + (hd + 1) * LANE].T.astype(BF16)

    def rope128(t):
        return t * cosn + pltpu.roll(t, NSA_HEAD_DIM // 2, 1) * sinn

    for hd in range(NSA_HEADS):
        t = h[:, _O_NQ + hd * LANE:_O_NQ + (hd + 1) * LANE]
        qn_ref[hd] = (rope128(t) * NSA_SCALE).astype(BF16)

    row0 = pl.program_id(0) * tm
    kpos = row0 + lax.broadcasted_iota(jnp.int32, (tm, LANE), 0)
    lane = lax.broadcasted_iota(jnp.int32, (tm, LANE), 1)
    onehot = jnp.where(((kpos // SEL_BLOCK) % LANE) == lane, 1.0, 0.0).astype(BF16)
    for hk in range(NSA_KV_HEADS):
        sl = lambda off: slice(off + hk * LANE, off + (hk + 1) * LANE)
        kc_ref[hk] = rope128(h[:, sl(_O_KC)])
        vc_ref[hk] = h[:, sl(_O_VC)]
        ks_ref[hk, :, 0:LANE] = rope128(h[:, sl(_O_KS)]).astype(BF16)
        ks_ref[hk, :, LANE:2 * LANE] = onehot
        vs_ref[hk] = h[:, sl(_O_VS)].T.astype(BF16)
        kw_ref[hk] = rope128(h[:, sl(_O_KW)]).astype(BF16)
        vw_ref[hk] = h[:, sl(_O_VW)].T.astype(BF16)
        gt_ref[hk] = jax.nn.sigmoid(h[:, sl(_O_G)] + gb_ref[hk]).T[0:GATE_ROWS]


def _const_spec(shape):
    nd = len(shape)
    return pl.BlockSpec(shape, lambda i: (0,) * nd)


def _proj(x1, win_p, gq, gkv, wuq_all, wukv_p, gb_p, cosm, sinm, cosn, sinn):
    s, d = x1.shape
    tm = PROJ_TM
    grid = (s // tm,)
    row_spec = lambda w: pl.BlockSpec((tm, w), lambda i: (i, 0))
    head_spec = lambda nh, w: pl.BlockSpec((nh, tm, w), lambda i: (0, i, 0))
    out_shape = (
        jax.ShapeDtypeStruct((MLA_HEADS, s, MLA_QK_PAD), BF16),
        jax.ShapeDtypeStruct((MLA_HEADS, s, MLA_QK_PAD), BF16),
        jax.ShapeDtypeStruct((MLA_HEADS, MLA_V_DIM, s), BF16),
        jax.ShapeDtypeStruct((NSA_HEADS, s, NSA_HEAD_DIM), BF16),
        jax.ShapeDtypeStruct((NSA_KV_HEADS, s, NSA_HEAD_DIM), F32),
        jax.ShapeDtypeStruct((NSA_KV_HEADS, s, NSA_HEAD_DIM), F32),
        jax.ShapeDtypeStruct((NSA_KV_HEADS, s, 2 * LANE), BF16),
        jax.ShapeDtypeStruct((NSA_KV_HEADS, NSA_HEAD_DIM, s), BF16),
        jax.ShapeDtypeStruct((NSA_KV_HEADS, s, NSA_HEAD_DIM), BF16),
        jax.ShapeDtypeStruct((NSA_KV_HEADS, NSA_HEAD_DIM, s), BF16),
        jax.ShapeDtypeStruct((NSA_KV_HEADS, GATE_ROWS, s), F32),
    )
    head_spec_t = lambda nh, r: pl.BlockSpec((nh, r, tm), lambda i: (0, 0, i))
    out_specs = (
        head_spec(MLA_HEADS, MLA_QK_PAD), head_spec(MLA_HEADS, MLA_QK_PAD), head_spec_t(MLA_HEADS, MLA_V_DIM),
        head_spec(NSA_HEADS, NSA_HEAD_DIM),
        head_spec(NSA_KV_HEADS, NSA_HEAD_DIM), head_spec(NSA_KV_HEADS, NSA_HEAD_DIM),
        head_spec(NSA_KV_HEADS, 2 * LANE), head_spec_t(NSA_KV_HEADS, NSA_HEAD_DIM),
        head_spec(NSA_KV_HEADS, NSA_HEAD_DIM), head_spec_t(NSA_KV_HEADS, NSA_HEAD_DIM),
        head_spec_t(NSA_KV_HEADS, GATE_ROWS),
    )
    return pl.pallas_call(
        _proj_kernel,
        grid=grid,
        in_specs=[
            row_spec(d),
            _const_spec(win_p.shape), _const_spec(gq.shape), _const_spec(gkv.shape),
            _const_spec(wuq_all.shape), _const_spec(wukv_p.shape), _const_spec(gb_p.shape),
            row_spec(LANE), row_spec(LANE), row_spec(LANE), row_spec(LANE),
        ],
        out_specs=out_specs,
        out_shape=out_shape,
        compiler_params=pltpu.CompilerParams(
            dimension_semantics=("parallel",), vmem_limit_bytes=VMEM_LIMIT),
        name="proj",
    )(x1, win_p, gq, gkv, wuq_all, wukv_p, gb_p, cosm, sinm, cosn, sinn)


_CMP_HALF = CMP_STRIDE * NSA_HEAD_DIM


def _compress_kernel(t_ref, pe_ref, w1_ref, w2_ref, o_ref, *, transpose_out):
    hi = lax.Precision.HIGHEST
    t = t_ref[0]
    w1a = w1_ref[0:_CMP_HALF, :]
    w1b = w1_ref[_CMP_HALF:2 * _CMP_HALF, :]
    first = _dot(t, w1a, precision=hi)
    second = _dot(t, w1b, precision=hi)
    pe_row = _dot(pe_ref[...], w1_ref[...], precision=hi)[0:1]
    hid = first + pltpu.roll(second, N_CMP_PAD - 1, 0) + pe_row
    out = _dot(jax.nn.gelu(hid), w2_ref[...], precision=hi)
    o_ref[0] = (out.T if transpose_out else out).astype(o_ref.dtype)


def _compress(t16, pe_flat8, w1, w2, transpose_out):
    out_dims = (NSA_HEAD_DIM, N_CMP_PAD) if transpose_out else (N_CMP_PAD, NSA_HEAD_DIM)
    return pl.pallas_call(
        functools.partial(_compress_kernel, transpose_out=transpose_out),
        grid=(NSA_KV_HEADS,),
        in_specs=[
            pl.BlockSpec((1, N_CMP_PAD, _CMP_HALF), lambda i: (i, 0, 0)),
            _const_spec(pe_flat8.shape), _const_spec(w1.shape), _const_spec(w2.shape),
        ],
        out_specs=pl.BlockSpec((1,) + out_dims, lambda i: (i, 0, 0)),
        out_shape=jax.ShapeDtypeStruct((NSA_KV_HEADS,) + out_dims, BF16),
        compiler_params=pltpu.CompilerParams(
            dimension_semantics=("parallel",), vmem_limit_bytes=VMEM_LIMIT),
        name="compress",
    )(t16, pe_flat8, w1, w2)


def _flash_init(m_ref, l_ref, acc_ref):
    m_ref[...] = jnp.full_like(m_ref, M_INIT)
    l_ref[...] = jnp.zeros_like(l_ref)
    acc_ref[...] = jnp.zeros_like(acc_ref)


def _flash_update(st, vt, m_ref, l_ref, acc_ref):
    m_prev = m_ref[...]
    m_new = jnp.maximum(m_prev, jnp.max(st, axis=0, keepdims=True))
    alpha = jnp.exp2(m_prev - m_new)
    p = jnp.exp2(st - m_new)
    l_ref[...] = alpha * l_ref[...] + jnp.sum(p, axis=0, keepdims=True)
    acc_ref[...] = alpha * acc_ref[...] + _dot(vt, p.astype(BF16))
    m_ref[...] = m_new


def _flash_causal(scores, values, causal_mask, n_plain, sa_ref, sb_ref, m_ref, l_ref, acc_ref):
    _flash_init(m_ref, l_ref, acc_ref)
    sa_ref[...] = scores(0)

    def body(j, carry):
        t = 2 * j
        sb_ref[...] = scores(t + 1)
        _flash_update(sa_ref[...], values(t), m_ref, l_ref, acc_ref)
        sa_ref[...] = scores(t + 2)
        _flash_update(sb_ref[...], values(t + 1), m_ref, l_ref, acc_ref)
        return carry

    n_pairs = n_plain // 2
    lax.fori_loop(0, n_pairs, body, 0)
    t = 2 * n_pairs
    sb_ref[...] = scores(t + 1)
    _flash_update(jnp.where(causal_mask(t), sa_ref[...], MASK_BIAS), values(t), m_ref, l_ref, acc_ref)
    _flash_update(jnp.where(causal_mask(t + 1), sb_ref[...], MASK_BIAS), values(t + 1), m_ref, l_ref, acc_ref)


def _masked_probs_t(st, mask):
    st = jnp.where(mask, st, NEG_INF)
    m = jnp.max(st, axis=0, keepdims=True)
    p = jnp.where(mask, jnp.exp2(st - m), 0.0)
    return p * (1.0 / jnp.maximum(jnp.sum(p, axis=0, keepdims=True), 1e-30))


MLA_TQ = 1024
MLA_TK = 512


def _mla_kernel(q_ref, k_ref, vt_ref, o_ref, sa_ref, sb_ref, m_ref, l_ref, acc_ref):
    i = pl.program_id(1)
    q = q_ref[0]

    def scores(kt):
        k0 = pl.multiple_of(kt * MLA_TK, MLA_TK)
        return _dot_nt(k_ref[0, pl.ds(k0, MLA_TK), :], q)

    def values(kt):
        return vt_ref[0, :, pl.ds(pl.multiple_of(kt * MLA_TK, MLA_TK), MLA_TK)]

    qpos = i * MLA_TQ + lax.broadcasted_iota(jnp.int32, (1, MLA_TQ), 1)

    def causal_mask(kt):
        return kt * MLA_TK + lax.broadcasted_iota(jnp.int32, (MLA_TK, 1), 0) <= qpos

    _flash_causal(scores, values, causal_mask, i * (MLA_TQ // MLA_TK), sa_ref, sb_ref, m_ref, l_ref, acc_ref)
    o_ref[...] = (acc_ref[...] * (1.0 / l_ref[...])).T.astype(o_ref.dtype)


def _mla(q, k, vt):
    nh, s, dq = q.shape
    dv = vt.shape[1]
    assert MLA_TQ == 2 * MLA_TK and s % MLA_TQ == 0
    return pl.pallas_call(
        _mla_kernel,
        grid=(nh, s // MLA_TQ),
        in_specs=[
            pl.BlockSpec((1, MLA_TQ, dq), lambda h, i: (h, i, 0)),
            pl.BlockSpec((1, s, dq), lambda h, i: (h, 0, 0)),
            pl.BlockSpec((1, dv, s), lambda h, i: (h, 0, 0)),
        ],
        out_specs=pl.BlockSpec((MLA_TQ, dv), lambda h, i: (i, h)),
        out_shape=jax.ShapeDtypeStruct((s, nh * dv), BF16),
        scratch_shapes=[pltpu.VMEM((MLA_TK, MLA_TQ), F32), pltpu.VMEM((MLA_TK, MLA_TQ), F32),
                        pltpu.VMEM((1, MLA_TQ), F32), pltpu.VMEM((1, MLA_TQ), F32),
                        pltpu.VMEM((dv, MLA_TQ), F32)],
        compiler_params=pltpu.CompilerParams(
            dimension_semantics=("parallel", "arbitrary"), vmem_limit_bytes=VMEM_LIMIT),
        name="mla",
    )(q, k, vt)


NSA_TQ = 128
NSA_ROWS = NSA_GROUP * NSA_TQ
NSA_TK = 512
_SEL_HALF_KEYS = LANE * SEL_BLOCK
_WIN_TILES = WINDOW // NSA_TQ + 1


def _nsa_kernel(q_ref, kc_ref, vct_ref, ovt_ref, ks_ref, vst_ref, *rest):
    kw_refs = rest[:_WIN_TILES]
    vwt_refs = rest[_WIN_TILES:2 * _WIN_TILES]
    gt_ref, o_ref, qaug_ref, sa_ref, sb_ref, m_ref, l_ref, acc_ref = rest[2 * _WIN_TILES:]

    i = pl.program_id(1)
    q0 = i * NSA_TQ
    q = q_ref[...].reshape(NSA_ROWS, NSA_HEAD_DIM)
    tq = q0 + (lax.broadcasted_iota(jnp.int32, (1, NSA_ROWS), 1) & (NSA_TQ - 1))

    s_c = _dot_nt(kc_ref[0], q)
    cend = lax.broadcasted_iota(jnp.int32, (N_CMP_PAD, 1), 0) * CMP_STRIDE + (CMP_BLOCK - 1)
    p_c = _masked_probs_t(s_c, cend <= tq)
    o_c = _dot(vct_ref[0], p_c.astype(BF16))

    kw = jnp.concatenate([r[0] for r in kw_refs], axis=0)
    vwt = jnp.concatenate([r[0] for r in vwt_refs], axis=1)
    wpos = q0 - WINDOW + lax.broadcasted_iota(jnp.int32, (WINDOW + NSA_TQ, 1), 0)
    back = tq - jnp.where(wpos >= 0, wpos, SEQ + WINDOW)
    p_w = _masked_probs_t(_dot_nt(kw, q), pltpu.bitcast(back, jnp.uint32) < WINDOW)
    o_w = _dot(vwt, p_w.astype(BF16))

    psum = p_c[:, 0:NSA_TQ]
    for g in range(1, NSA_GROUP):
        psum = psum + p_c[:, g * NSA_TQ:(g + 1) * NSA_TQ]
    p_hi = psum.astype(BF16)
    p_lo = (psum - p_hi.astype(F32)).astype(BF16)
    imp = _dot(ovt_ref[...], p_hi) + _dot(ovt_ref[...], p_lo)

    tq1 = q0 + lax.broadcasted_iota(jnp.int32, (1, NSA_TQ), 1)
    bj = lax.broadcasted_iota(jnp.int32, (N_SEL, NSA_TQ), 0)
    cur = tq1 // SEL_BLOCK
    valid = bj * SEL_BLOCK <= tq1
    forced = (bj == 0) | (bj == cur) | (bj == cur - 1)
    score = jnp.where(valid, imp + FORCE_BONUS * jnp.where(forced, 1.0, 0.0), NEG_INF)
    bjf = bj.astype(F32)
    work = score
    picked = jnp.zeros(score.shape, F32)
    for _ in range(SEL_TOPK):
        mx = jnp.max(work, axis=0, keepdims=True)
        first = jnp.min(jnp.where(work == mx, bjf, float(N_SEL)), axis=0, keepdims=True)
        hit = bjf == first
        picked = jnp.where(hit, 1.0, picked)
        work = jnp.where(hit, -3e38, work)
    selected = (picked > 0.5) & (score > 0.5 * NEG_INF)
    bias = jnp.where(selected, 0.0, MASK_BIAS).T.astype(BF16)
    bias = jnp.concatenate([bias] * NSA_GROUP, axis=0)
    for half in range(N_SEL // LANE):
        qaug_ref[half, :, 0:NSA_HEAD_DIM] = q
        qaug_ref[half, :, NSA_HEAD_DIM:NSA_HEAD_DIM + LANE] = bias[:, half * LANE:(half + 1) * LANE]

    last = (q0 + NSA_TQ - 1) // NSA_TK

    def scores(kt):
        k0 = pl.multiple_of(kt * NSA_TK, NSA_TK)
        qa = qaug_ref[kt // (_SEL_HALF_KEYS // NSA_TK)]
        return _dot_nt(ks_ref[0, pl.ds(k0, NSA_TK), :], qa)

    def values(kt):
        return vst_ref[0, :, pl.ds(pl.multiple_of(kt * NSA_TK, NSA_TK), NSA_TK)]

    def causal_mask(kt):
        return kt * NSA_TK + lax.broadcasted_iota(jnp.int32, (NSA_TK, 1), 0) <= tq

    _flash_causal(scores, values, causal_mask, last, sa_ref, sb_ref, m_ref, l_ref, acc_ref)
    o_s = acc_ref[...] * (1.0 / l_ref[...])

    gt = gt_ref[0]
    gate = lambda br: jnp.concatenate(
        [gt[br * NSA_GROUP + g:br * NSA_GROUP + g + 1, :] for g in range(NSA_GROUP)], axis=1)
    out = (gate(0) * o_c + gate(1) * o_s + gate(2) * o_w).T
    for g in range(NSA_GROUP):
        o_ref[:, g * NSA_HEAD_DIM:(g + 1) * NSA_HEAD_DIM] = out[g * NSA_TQ:(g + 1) * NSA_TQ].astype(o_ref.dtype)


def _nsa(q, kc, vct, ovt, ks, vst, kw, vwt, gates):
    s = q.shape[1]
    d = NSA_HEAD_DIM
    assert (s // NSA_TK) % 2 == 0

    def win_specs(transposed):
        def spec(j):
            back = _WIN_TILES - 1 - j
            if transposed:
                return pl.BlockSpec((1, d, NSA_TQ), lambda h, i: (h, 0, jnp.maximum(i - back, 0)))
            return pl.BlockSpec((1, NSA_TQ, d), lambda h, i: (h, jnp.maximum(i - back, 0), 0))
        return [spec(j) for j in range(_WIN_TILES)]

    return pl.pallas_call(
        _nsa_kernel,
        grid=(NSA_KV_HEADS, s // NSA_TQ),
        in_specs=[
            pl.BlockSpec((NSA_GROUP, NSA_TQ, d), lambda h, i: (h, i, 0)),
            pl.BlockSpec((1, N_CMP_PAD, d), lambda h, i: (h, 0, 0)),
            pl.BlockSpec((1, d, N_CMP_PAD), lambda h, i: (h, 0, 0)),
            pl.BlockSpec((N_SEL, N_CMP_PAD), lambda h, i: (0, 0)),
            pl.BlockSpec((1, s, 2 * LANE), lambda h, i: (h, 0, 0)),
            pl.BlockSpec((1, d, s), lambda h, i: (h, 0, 0)),
            *win_specs(False), *win_specs(True),
            pl.BlockSpec((1, GATE_ROWS, NSA_TQ), lambda h, i: (h, 0, i)),
        ],
        out_specs=pl.BlockSpec((NSA_TQ, NSA_GROUP * d), lambda h, i: (i, h)),
        out_shape=jax.ShapeDtypeStruct((s, NSA_HEADS * d), BF16),
        scratch_shapes=[pltpu.VMEM((N_SEL // LANE, NSA_ROWS, NSA_HEAD_DIM + LANE), BF16),
                        pltpu.VMEM((NSA_TK, NSA_ROWS), F32), pltpu.VMEM((NSA_TK, NSA_ROWS), F32),
                        pltpu.VMEM((1, NSA_ROWS), F32), pltpu.VMEM((1, NSA_ROWS), F32),
                        pltpu.VMEM((d, NSA_ROWS), F32)],
        compiler_params=pltpu.CompilerParams(
            dimension_semantics=("parallel", "arbitrary"), vmem_limit_bytes=VMEM_LIMIT),
        name="nsa",
    )(q, kc, vct, ovt, ks, vst, *([kw] * _WIN_TILES), *([vwt] * _WIN_TILES), gates)


OUT_TM = 512


def _out_ln_kernel(om_ref, on_ref, wa_ref, wb_ref, x_ref, g_ref, b_ref, o_ref):
    mix = _dot(om_ref[...], wa_ref[...]) + _dot(on_ref[...], wb_ref[...])
    o_ref[...] = _layer_norm(DEEPNORM_ALPHA * x_ref[...] + mix, g_ref[...], b_ref[...])


def _out_ln(o_mla, o_nsa, wa, wb, x1, g, b):
    s, d = x1.shape
    tm = OUT_TM
    return pl.pallas_call(
        _out_ln_kernel,
        grid=(s // tm,),
        in_specs=[
            pl.BlockSpec((tm, o_mla.shape[1]), lambda i: (i, 0)),
            pl.BlockSpec((tm, o_nsa.shape[1]), lambda i: (i, 0)),
            _const_spec(wa.shape), _const_spec(wb.shape),
            pl.BlockSpec((tm, d), lambda i: (i, 0)),
            _const_spec(g.shape), _const_spec(b.shape),
        ],
        out_specs=pl.BlockSpec((tm, d), lambda i: (i, 0)),
        out_shape=jax.ShapeDtypeStruct((s, d), F32),
        compiler_params=pltpu.CompilerParams(
            dimension_semantics=("parallel",), vmem_limit_bytes=VMEM_LIMIT),
        name="out_ln",
    )(o_mla, o_nsa, wa, wb, x1, g, b)


def _pad_cols(w, n):
    return jnp.pad(w, ((0, 0), (0, n - w.shape[1])))


def _relayout_w_in(w_in, gate_b):
    sizes = (MLA_Q_RANK, MLA_KV_RANK, MLA_ROPE_DIM, NSA_HEADS * NSA_HEAD_DIM) + (NSA_KV_HEADS * NSA_HEAD_DIM,) * 6 \
        + (N_BRANCH * NSA_HEADS,)
    offs = [int(v) for v in np.cumsum(sizes)[:-1]]
    c_q, c_kv, k_r, q_n, k_c, v_c, k_s, v_s, k_w, v_w, g = jnp.split(w_in, offs, axis=1)
    half = MLA_ROPE_DIM // 2
    k_r_sw = jnp.concatenate([k_r[:, half:], k_r[:, :half]], axis=1)
    gate_idx = np.array([[br * NSA_HEADS + hk * NSA_GROUP + gi for br in range(N_BRANCH) for gi in range(NSA_GROUP)]
                         for hk in range(NSA_KV_HEADS)])
    g_chunks = [_pad_cols(g[:, gate_idx[hk]], LANE) for hk in range(NSA_KV_HEADS)]
    w = jnp.concatenate([c_q, c_kv, q_n, k_c, v_c, k_s, v_s, k_w, v_w,
                         _pad_cols(k_r, LANE), _pad_cols(k_r_sw, LANE)] + g_chunks, axis=1)
    assert w.shape[1] == _PROJ_COLS
    gb = jnp.stack([jnp.pad(gate_b[gate_idx[hk]], (0, LANE - gate_idx.shape[1])) for hk in range(NSA_KV_HEADS)])
    return w.astype(BF16), gb.reshape(NSA_KV_HEADS, 1, LANE)


def _relayout_w_uq(w_uq):
    dh = MLA_NOPE_DIM + MLA_ROPE_DIM
    half = MLA_ROPE_DIM // 2
    w = w_uq.reshape(MLA_Q_RANK, MLA_HEADS, dh)
    nope = w[:, :, :MLA_NOPE_DIM]
    r = w[:, :, MLA_NOPE_DIM:]
    r_sw = jnp.concatenate([r[:, :, half:], r[:, :, :half]], axis=2)
    pad = lambda t: jnp.pad(t, ((0, 0), (0, 0), (0, LANE - MLA_ROPE_DIM)))
    parts = [nope, pad(r), pad(r_sw)]
    return jnp.concatenate([p.reshape(MLA_Q_RANK, MLA_HEADS * LANE) for p in parts], axis=1).astype(BF16)


def _relayout_w_ukv(w_ukv):
    w = w_ukv.reshape(MLA_KV_RANK, MLA_HEADS, MLA_NOPE_DIM + MLA_V_DIM)
    k = w[:, :, :MLA_NOPE_DIM].reshape(MLA_KV_RANK, MLA_HEADS * MLA_NOPE_DIM)
    v = w[:, :, MLA_NOPE_DIM:].reshape(MLA_KV_RANK, MLA_HEADS * MLA_V_DIM)
    return jnp.concatenate([k, v], axis=1).astype(BF16)


def _rope_tables(s):
    pos = jnp.arange(s).astype(F32)[:, None]

    def cs(d):
        half = d // 2
        inv = ROPE_THETA ** (-jnp.arange(half, dtype=F32) * (2.0 / d))
        ang = pos * inv[None, :]
        return jnp.cos(ang), jnp.sin(ang)

    cm, sm = cs(MLA_ROPE_DIM)
    cn, sn = cs(NSA_HEAD_DIM)
    cosm = _pad_cols(jnp.concatenate([cm, cm], axis=1), LANE)
    sinm = _pad_cols(jnp.concatenate([-sm, sm], axis=1), LANE)
    cosn = jnp.concatenate([cn, cn], axis=1)
    sinn = jnp.concatenate([-sn, sn], axis=1)
    return cosm, sinm, cosn, sinn


def _overlap_weights():
    ci = np.arange(N_CMP_PAD)[:, None] * CMP_STRIDE
    sj = np.arange(N_SEL)[None, :] * SEL_BLOCK
    ov = np.clip(np.minimum(ci + CMP_BLOCK, sj + SEL_BLOCK) - np.maximum(ci, sj), 0, None)
    ov[N_CMP:] = 0
    return jnp.asarray((ov.astype(np.float32) / CMP_STRIDE).T, dtype=BF16)


def kernel(x, ffn1_w_gate, ffn1_w_up, ffn1_w_down, ln1_g, ln1_b, w_in, mla_q_norm_g, mla_w_uq, mla_kv_norm_g,
           mla_w_ukv, nsa_gate_b, nsa_cmp_pe_k, nsa_cmp_w1_k, nsa_cmp_w2_k, nsa_cmp_pe_v, nsa_cmp_w1_v,
           nsa_cmp_w2_v, w_out, ln2_g, ln2_b, ffn2_w_gate, ffn2_w_up, ffn2_w_down, ln3_g, ln3_b):
    b, s, d = x.shape
    assert (b, s, d) == (1, SEQ, D_MODEL) and ffn1_w_gate.shape[0] == DEPTH
    l = 0
    row = lambda v: v.reshape(1, -1)
    xs = x.reshape(s, d)

    x1 = _ffn_ln(xs, ffn1_w_gate[l].astype(BF16), ffn1_w_up[l].astype(BF16), ffn1_w_down[l].astype(BF16),
                 row(ln1_g[l]), row(ln1_b[l]))

    win_p, gb_p = _relayout_w_in(w_in[l], nsa_gate_b[l])
    cosm, sinm, cosn, sinn = _rope_tables(s)
    (q_m, k_m, v_m, q_n, kc_in, vc_in, ks, vs, kw, vw, gates) = _proj(
        x1, win_p, row(mla_q_norm_g[l]), row(mla_kv_norm_g[l]),
        _relayout_w_uq(mla_w_uq[l]), _relayout_w_ukv(mla_w_ukv[l]), gb_p, cosm, sinm, cosn, sinn)

    def compress(t, pe, w1, w2, transpose_out):
        t16 = t.reshape(NSA_KV_HEADS, N_CMP_PAD, _CMP_HALF)
        pe8 = jnp.broadcast_to(pe.reshape(1, CMP_BLOCK * NSA_HEAD_DIM), (8, CMP_BLOCK * NSA_HEAD_DIM))
        return _compress(t16, pe8, w1, w2, transpose_out)

    kc = compress(kc_in, nsa_cmp_pe_k[l], nsa_cmp_w1_k[l], nsa_cmp_w2_k[l], False)
    vc = compress(vc_in, nsa_cmp_pe_v[l], nsa_cmp_w1_v[l], nsa_cmp_w2_v[l], True)

    o_mla = _mla(q_m, k_m, v_m)
    o_nsa = _nsa(q_n, kc, vc, _overlap_weights(), ks, vs, kw, vw, gates)

    n_mla = MLA_HEADS * MLA_V_DIM
    w_o = w_out[l].astype(BF16)
    x2 = _out_ln(o_mla, o_nsa, w_o[:n_mla], w_o[n_mla:], x1, row(ln2_g[l]), row(ln2_b[l]))

    x3 = _ffn_ln(x2, ffn2_w_gate[l].astype(BF16), ffn2_w_up[l].astype(BF16), ffn2_w_down[l].astype(BF16),
                 row(ln3_g[l]), row(ln3_b[l]))
    return x3.reshape(b, s, d)
```

```python
import functools

import jax
import jax.numpy as jnp
import numpy as np
from jax import lax
from jax.experimental import pallas as pl
from jax.experimental.pallas import tpu as pltpu

F32 = jnp.float32
BF16 = jnp.bfloat16

D_MODEL = 2048
SEQ = 16384
DEPTH = 1
DEEPNORM_ALPHA = (2.0 * DEPTH) ** 0.25
LN_EPS = 1e-5
RMS_EPS = 1e-6
ROPE_THETA = 10000.0
NEG_INF = -1e30
D_FF = 5632
FFN_RES_WEIGHT = 0.5

MLA_HEADS = 8
MLA_Q_RANK = 512
MLA_KV_RANK = 256
MLA_NOPE_DIM = 128
MLA_ROPE_DIM = 64
MLA_V_DIM = 128
MLA_QK_PAD = 256

NSA_HEADS = 8
NSA_KV_HEADS = 2
NSA_HEAD_DIM = 128
NSA_GROUP = NSA_HEADS // NSA_KV_HEADS
CMP_BLOCK = 32
CMP_STRIDE = 16
CMP_HIDDEN = 256
SEL_BLOCK = 64
SEL_TOPK = 16
WINDOW = 512
N_BRANCH = 3
FORCE_BONUS = 1e4
N_CMP = (SEQ - CMP_BLOCK) // CMP_STRIDE + 1
N_CMP_PAD = SEQ // CMP_STRIDE
N_SEL = SEQ // SEL_BLOCK

LANE = 128
MASK_BIAS = NEG_INF
M_INIT = -1e30

VMEM_LIMIT = 56 * 1024 * 1024

NT_DIMS = (((1,), (1,)), ((), ()))


def _dot(a, b, **kw):
    return jnp.dot(a, b, preferred_element_type=F32, **kw)


def _dot_nt(a, b):
    return lax.dot_general(a, b, NT_DIMS, preferred_element_type=F32)


def _layer_norm(y, g, b):
    mu = jnp.mean(y, axis=-1, keepdims=True)
    yc = y - mu
    var = jnp.mean(yc * yc, axis=-1, keepdims=True)
    return yc * lax.rsqrt(var + LN_EPS) * g + b


def _rms_norm(x, g):
    ms = jnp.mean(x * x, axis=-1, keepdims=True)
    return x * lax.rsqrt(ms + RMS_EPS) * g


FFN_TM = 512
FFN_TF = 512


def _ffn_ln_kernel(x_ref, wg_ref, wu_ref, wd_ref, g_ref, b_ref, o_ref, xb_ref, acc_ref):
    k = pl.program_id(1)

    @pl.when(k == 0)
    def _():
        xb_ref[...] = x_ref[...].astype(BF16)
        acc_ref[...] = jnp.zeros_like(acc_ref)

    xb = xb_ref[...]
    gate = _dot(xb, wg_ref[...])
    up = _dot(xb, wu_ref[...])
    h = (jax.nn.silu(gate) * up).astype(BF16)
    acc_ref[...] += _dot(h, wd_ref[...])

    @pl.when(k == pl.num_programs(1) - 1)
    def _():
        y = DEEPNORM_ALPHA * x_ref[...] + FFN_RES_WEIGHT * acc_ref[...]
        o_ref[...] = _layer_norm(y, g_ref[...], b_ref[...])


def _ffn_ln(x, wg, wu, wd, g, b):
    s, d = x.shape
    f = wg.shape[1]
    grid = (s // FFN_TM, f // FFN_TF)
    return pl.pallas_call(
        _ffn_ln_kernel,
        grid=grid,
        in_specs=[
            pl.BlockSpec((FFN_TM, d), lambda i, k: (i, 0)),
            pl.BlockSpec((d, FFN_TF), lambda i, k: (0, k)),
            pl.BlockSpec((d, FFN_TF), lambda i, k: (0, k)),
            pl.BlockSpec((FFN_TF, d), lambda i, k: (k, 0)),
            pl.BlockSpec((1, d), lambda i, k: (0, 0)),
            pl.BlockSpec((1, d), lambda i, k: (0, 0)),
        ],
        out_specs=pl.BlockSpec((FFN_TM, d), lambda i, k: (i, 0)),
        out_shape=jax.ShapeDtypeStruct((s, d), F32),
        scratch_shapes=[pltpu.VMEM((FFN_TM, d), BF16), pltpu.VMEM((FFN_TM, d), F32)],
        compiler_params=pltpu.CompilerParams(
            dimension_semantics=("parallel", "arbitrary"), vmem_limit_bytes=VMEM_LIMIT),
        name="ffn_ln",
    )(x, wg, wu, wd, g, b)


PROJ_TM = 256
_O_CQ = 0
_O_CKV = _O_CQ + MLA_Q_RANK
_O_NQ = _O_CKV + MLA_KV_RANK
_O_KC = _O_NQ + NSA_HEADS * NSA_HEAD_DIM
_O_VC = _O_KC + NSA_KV_HEADS * NSA_HEAD_DIM
_O_KS = _O_VC + NSA_KV_HEADS * NSA_HEAD_DIM
_O_VS = _O_KS + NSA_KV_HEADS * NSA_HEAD_DIM
_O_KW = _O_VS + NSA_KV_HEADS * NSA_HEAD_DIM
_O_VW = _O_KW + NSA_KV_HEADS * NSA_HEAD_DIM
_O_KR = _O_VW + NSA_KV_HEADS * NSA_HEAD_DIM
_O_KRS = _O_KR + LANE
_O_G = _O_KRS + LANE
_PROJ_COLS = _O_G + NSA_KV_HEADS * LANE
GATE_ROWS = 16

LOG2E = 1.4426950408889634
MLA_SCALE = (MLA_NOPE_DIM + MLA_ROPE_DIM) ** -0.5 * LOG2E
NSA_SCALE = NSA_HEAD_DIM ** -0.5 * LOG2E


def _proj_kernel(x_ref, win_ref, gq_ref, gkv_ref, wuq_ref, wukv_ref, gb_ref,
                 cosm_ref, sinm_ref, cosn_ref, sinn_ref,
                 qm_ref, km_ref, vm_ref, qn_ref, kc_ref, vc_ref, ks_ref, vs_ref, kw_ref, vw_ref, gt_ref):
    tm = x_ref.shape[0]
    xb = x_ref[...].astype(BF16)
    h = _dot(xb, win_ref[...])
    cosm, sinm = cosm_ref[...], sinm_ref[...]
    cosn, sinn = cosn_ref[...], sinn_ref[...]

    cqn = _rms_norm(h[:, _O_CQ:_O_CQ + MLA_Q_RANK], gq_ref[...]).astype(BF16)
    qall = _dot(cqn, wuq_ref[...])
    hw = MLA_HEADS * LANE
    for hd in range(MLA_HEADS):
        sl = slice(hd * LANE, (hd + 1) * LANE)
        qm_ref[hd, :, 0:LANE] = (qall[:, sl] * MLA_SCALE).astype(BF16)
        qr = qall[:, hw + hd * LANE: hw + (hd + 1) * LANE]
        qrs = qall[:, 2 * hw + hd * LANE: 2 * hw + (hd + 1) * LANE]
        qm_ref[hd, :, LANE:2 * LANE] = ((qr * cosm + qrs * sinm) * MLA_SCALE).astype(BF16)

    ckvn = _rms_norm(h[:, _O_CKV:_O_CKV + MLA_KV_RANK], gkv_ref[...]).astype(BF16)
    kv = _dot(ckvn, wukv_ref[...])
    kr = (h[:, _O_KR:_O_KR + LANE] * cosm + h[:, _O_KRS:_O_KRS + LANE] * sinm).astype(BF16)
    for hd in range(MLA_HEADS):
        km_ref[hd, :, 0:LANE] = kv[:, hd * LANE:(hd + 1) * LANE].astype(BF16)
        km_ref[hd, :, LANE:2 * LANE] = kr
        vm_ref[hd] = kv[:, hw + hd * LANE: hw + (hd + 1) * LANE].T.astype(BF16)

    def rope128(t):
        return t * cosn + pltpu.roll(t, NSA_HEAD_DIM // 2, 1) * sinn

    for hd in range(NSA_HEADS):
        t = h[:, _O_NQ + hd * LANE:_O_NQ + (hd + 1) * LANE]
        qn_ref[hd] = (rope128(t) * NSA_SCALE).astype(BF16)

    row0 = pl.program_id(0) * tm
    kpos = row0 + lax.broadcasted_iota(jnp.int32, (tm, LANE), 0)
    lane = lax.broadcasted_iota(jnp.int32, (tm, LANE), 1)
    onehot = jnp.where(((kpos // SEL_BLOCK) % LANE) == lane, 1.0, 0.0).astype(BF16)
    for hk in range(NSA_KV_HEADS):
        sl = lambda off: slice(off + hk * LANE, off + (hk + 1) * LANE)
        kc_ref[hk] = rope128(h[:, sl(_O_KC)])
        vc_ref[hk] = h[:, sl(_O_VC)]
        ks_ref[hk, :, 0:LANE] = rope128(h[:, sl(_O_KS)]).astype(BF16)
        ks_ref[hk, :, LANE:2 * LANE] = onehot
        vs_ref[hk] = h[:, sl(_O_VS)].T.astype(BF16)
        kw_ref[hk] = rope128(h[:, sl(_O_KW)]).astype(BF16)
        vw_ref[hk] = h[:, sl(_O_VW)].T.astype(BF16)
        gt_ref[hk] = jax.nn.sigmoid(h[:, sl(_O_G)] + gb_ref[hk]).T[0:GATE_ROWS]


def _const_spec(shape):
    nd = len(shape)
    return pl.BlockSpec(shape, lambda i: (0,) * nd)


def _proj(x1, win_p, gq, gkv, wuq_all, wukv_p, gb_p, cosm, sinm, cosn, sinn):
    s, d = x1.shape
    tm = PROJ_TM
    grid = (s // tm,)
    row_spec = lambda w: pl.BlockSpec((tm, w), lambda i: (i, 0))
    head_spec = lambda nh, w: pl.BlockSpec((nh, tm, w), lambda i: (0, i, 0))
    out_shape = (
        jax.ShapeDtypeStruct((MLA_HEADS, s, MLA_QK_PAD), BF16),
        jax.ShapeDtypeStruct((MLA_HEADS, s, MLA_QK_PAD), BF16),
        jax.ShapeDtypeStruct((MLA_HEADS, MLA_V_DIM, s), BF16),
        jax.ShapeDtypeStruct((NSA_HEADS, s, NSA_HEAD_DIM), BF16),
        jax.ShapeDtypeStruct((NSA_KV_HEADS, s, NSA_HEAD_DIM), F32),
        jax.ShapeDtypeStruct((NSA_KV_HEADS, s, NSA_HEAD_DIM), F32),
        jax.ShapeDtypeStruct((NSA_KV_HEADS, s, 2 * LANE), BF16),
        jax.ShapeDtypeStruct((NSA_KV_HEADS, NSA_HEAD_DIM, s), BF16),
        jax.ShapeDtypeStruct((NSA_KV_HEADS, s, NSA_HEAD_DIM), BF16),
        jax.ShapeDtypeStruct((NSA_KV_HEADS, NSA_HEAD_DIM, s), BF16),
        jax.ShapeDtypeStruct((NSA_KV_HEADS, GATE_ROWS, s), F32),
    )
    head_spec_t = lambda nh, r: pl.BlockSpec((nh, r, tm), lambda i: (0, 0, i))
    out_specs = (
        head_spec(MLA_HEADS, MLA_QK_PAD), head_spec(MLA_HEADS, MLA_QK_PAD), head_spec_t(MLA_HEADS, MLA_V_DIM),
        head_spec(NSA_HEADS, NSA_HEAD_DIM),
        head_spec(NSA_KV_HEADS, NSA_HEAD_DIM), head_spec(NSA_KV_HEADS, NSA_HEAD_DIM),
        head_spec(NSA_KV_HEADS, 2 * LANE), head_spec_t(NSA_KV_HEADS, NSA_HEAD_DIM),
        head_spec(NSA_KV_HEADS, NSA_HEAD_DIM), head_spec_t(NSA_KV_HEADS, NSA_HEAD_DIM),
        head_spec_t(NSA_KV_HEADS, GATE_ROWS),
    )
    return pl.pallas_call(
        _proj_kernel,
        grid=grid,
        in_specs=[
            row_spec(d),
            _const_spec(win_p.shape), _const_spec(gq.shape), _const_spec(gkv.shape),
            _const_spec(wuq_all.shape), _const_spec(wukv_p.shape), _const_spec(gb_p.shape),
            row_spec(LANE), row_spec(LANE), row_spec(LANE), row_spec(LANE),
        ],
        out_specs=out_specs,
        out_shape=out_shape,
        compiler_params=pltpu.CompilerParams(
            dimension_semantics=("parallel",), vmem_limit_bytes=VMEM_LIMIT),
        name="proj",
    )(x1, win_p, gq, gkv, wuq_all, wukv_p, gb_p, cosm, sinm, cosn, sinn)


_CMP_HALF = CMP_STRIDE * NSA_HEAD_DIM


def _compress_kernel(t_ref, pe_ref, w1_ref, w2_ref, o_ref, *, transpose_out):
    hi = lax.Precision.HIGHEST
    t = t_ref[0]
    w1a = w1_ref[0:_CMP_HALF, :]
    w1b = w1_ref[_CMP_HALF:2 * _CMP_HALF, :]
    first = _dot(t, w1a, precision=hi)
    second = _dot(t, w1b, precision=hi)
    pe_row = _dot(pe_ref[...], w1_ref[...], precision=hi)[0:1]
    hid = first + pltpu.roll(second, N_CMP_PAD - 1, 0) + pe_row
    out = _dot(jax.nn.gelu(hid), w2_ref[...], precision=hi)
    o_ref[0] = (out.T if transpose_out else out).astype(o_ref.dtype)


def _compress(t16, pe_flat8, w1, w2, transpose_out):
    out_dims = (NSA_HEAD_DIM, N_CMP_PAD) if transpose_out else (N_CMP_PAD, NSA_HEAD_DIM)
    return pl.pallas_call(
        functools.partial(_compress_kernel, transpose_out=transpose_out),
        grid=(NSA_KV_HEADS,),
        in_specs=[
            pl.BlockSpec((1, N_CMP_PAD, _CMP_HALF), lambda i: (i, 0, 0)),
            _const_spec(pe_flat8.shape), _const_spec(w1.shape), _const_spec(w2.shape),
        ],
        out_specs=pl.BlockSpec((1,) + out_dims, lambda i: (i, 0, 0)),
        out_shape=jax.ShapeDtypeStruct((NSA_KV_HEADS,) + out_dims, BF16),
        compiler_params=pltpu.CompilerParams(
            dimension_semantics=("parallel",), vmem_limit_bytes=VMEM_LIMIT),
        name="compress",
    )(t16, pe_flat8, w1, w2)


def _flash_init(m_ref, l_ref, acc_ref):
    m_ref[...] = jnp.full_like(m_ref, M_INIT)
    l_ref[...] = jnp.zeros_like(l_ref)
    acc_ref[...] = jnp.zeros_like(acc_ref)


def _flash_update(st, vt, m_ref, l_ref, acc_ref):
    m_prev = m_ref[...]
    m_new = jnp.maximum(m_prev, jnp.max(st, axis=0, keepdims=True))
    alpha = jnp.exp2(m_prev - m_new)
    p = jnp.exp2(st - m_new)
    l_ref[...] = alpha * l_ref[...] + jnp.sum(p, axis=0, keepdims=True)
    acc_ref[...] = alpha * acc_ref[...] + _dot(vt, p.astype(BF16))
    m_ref[...] = m_new


FLASH_PAIRS_PER_TRIP = 2


def _flash_causal(scores, values, causal_mask, n_plain, sa_ref, sb_ref, m_ref, l_ref, acc_ref):
    _flash_init(m_ref, l_ref, acc_ref)
    sa_ref[...] = scores(0)

    def plain_pair(t):
        sb_ref[...] = scores(t + 1)
        _flash_update(sa_ref[...], values(t), m_ref, l_ref, acc_ref)
        sa_ref[...] = scores(t + 2)
        _flash_update(sb_ref[...], values(t + 1), m_ref, l_ref, acc_ref)

    def body(j, carry):
        for u in range(FLASH_PAIRS_PER_TRIP):
            plain_pair(2 * (FLASH_PAIRS_PER_TRIP * j + u))
        return carry

    n_pairs = n_plain // 2
    n_trips = n_pairs // FLASH_PAIRS_PER_TRIP
    lax.fori_loop(0, n_trips, body, 0)
    for u in range(FLASH_PAIRS_PER_TRIP - 1):
        @pl.when(n_trips * FLASH_PAIRS_PER_TRIP + u < n_pairs)
        def _():
            plain_pair(2 * (n_trips * FLASH_PAIRS_PER_TRIP + u))
    t = 2 * n_pairs
    sb_ref[...] = scores(t + 1)
    _flash_update(jnp.where(causal_mask(t), sa_ref[...], MASK_BIAS), values(t), m_ref, l_ref, acc_ref)
    _flash_update(jnp.where(causal_mask(t + 1), sb_ref[...], MASK_BIAS), values(t + 1), m_ref, l_ref, acc_ref)


def _masked_probs_t(st, mask):
    st = jnp.where(mask, st, NEG_INF)
    m = jnp.max(st, axis=0, keepdims=True)
    p = jnp.where(mask, jnp.exp2(st - m), 0.0)
    return p * (1.0 / jnp.maximum(jnp.sum(p, axis=0, keepdims=True), 1e-30))


MLA_TQ = 1024
MLA_TK = 512


def _mla_kernel(q_ref, k_ref, vt_ref, o_ref, sa_ref, sb_ref, m_ref, l_ref, acc_ref):
    i = pl.program_id(1)
    q = q_ref[0]

    def scores(kt):
        k0 = pl.multiple_of(kt * MLA_TK, MLA_TK)
        return _dot_nt(k_ref[0, pl.ds(k0, MLA_TK), :], q)

    def values(kt):
        return vt_ref[0, :, pl.ds(pl.multiple_of(kt * MLA_TK, MLA_TK), MLA_TK)]

    qpos = i * MLA_TQ + lax.broadcasted_iota(jnp.int32, (1, MLA_TQ), 1)

    def causal_mask(kt):
        return kt * MLA_TK + lax.broadcasted_iota(jnp.int32, (MLA_TK, 1), 0) <= qpos

    _flash_causal(scores, values, causal_mask, i * (MLA_TQ // MLA_TK), sa_ref, sb_ref, m_ref, l_ref, acc_ref)
    o_ref[...] = (acc_ref[...] * (1.0 / l_ref[...])).T.astype(o_ref.dtype)


def _mla(q, k, vt):
    nh, s, dq = q.shape
    dv = vt.shape[1]
    assert MLA_TQ == 2 * MLA_TK and s % MLA_TQ == 0
    return pl.pallas_call(
        _mla_kernel,
        grid=(nh, s // MLA_TQ),
        in_specs=[
            pl.BlockSpec((1, MLA_TQ, dq), lambda h, i: (h, i, 0)),
            pl.BlockSpec((1, s, dq), lambda h, i: (h, 0, 0)),
            pl.BlockSpec((1, dv, s), lambda h, i: (h, 0, 0)),
        ],
        out_specs=pl.BlockSpec((MLA_TQ, dv), lambda h, i: (i, h)),
        out_shape=jax.ShapeDtypeStruct((s, nh * dv), BF16),
        scratch_shapes=[pltpu.VMEM((MLA_TK, MLA_TQ), F32), pltpu.VMEM((MLA_TK, MLA_TQ), F32),
                        pltpu.VMEM((1, MLA_TQ), F32), pltpu.VMEM((1, MLA_TQ), F32),
                        pltpu.VMEM((dv, MLA_TQ), F32)],
        compiler_params=pltpu.CompilerParams(
            dimension_semantics=("parallel", "arbitrary"), vmem_limit_bytes=VMEM_LIMIT),
        name="mla",
    )(q, k, vt)


NSA_TQ = 128
NSA_ROWS = NSA_GROUP * NSA_TQ
NSA_TK = 512
_SEL_HALF_KEYS = LANE * SEL_BLOCK
_WIN_TILES = WINDOW // NSA_TQ + 1


def _nsa_kernel(q_ref, kc_ref, vct_ref, ovt_ref, ks_ref, vst_ref, *rest):
    kw_refs = rest[:_WIN_TILES]
    vwt_refs = rest[_WIN_TILES:2 * _WIN_TILES]
    wbias_ref, qoh_ref, gt_ref, o_ref, qaug_ref, sa_ref, sb_ref, m_ref, l_ref, acc_ref = rest[2 * _WIN_TILES:]

    i = pl.program_id(1)
    q0 = i * NSA_TQ
    q = q_ref[...].reshape(NSA_ROWS, NSA_HEAD_DIM)
    tq = q0 + (lax.broadcasted_iota(jnp.int32, (1, NSA_ROWS), 1) & (NSA_TQ - 1))
    tq1 = q0 + lax.broadcasted_iota(jnp.int32, (1, NSA_TQ), 1)

    q_oh = jnp.concatenate([q, qoh_ref[...]], axis=1)

    cend = lax.broadcasted_iota(jnp.int32, (N_CMP_PAD, NSA_TQ), 0) * CMP_STRIDE + (CMP_BLOCK - 1)
    cbias = jnp.where(cend <= tq1, 0.0, NEG_INF).astype(BF16)
    s_c = _dot_nt(jnp.concatenate([kc_ref[0], cbias], axis=1), q_oh)
    e_c = jnp.exp2(s_c - jnp.max(s_c, axis=0, keepdims=True))
    inv_c = jnp.where(tq >= CMP_BLOCK - 1, 1.0 / jnp.sum(e_c, axis=0, keepdims=True), 0.0)
    p_c = e_c * inv_c
    o_c = _dot(vct_ref[0], p_c.astype(BF16))

    kw = jnp.concatenate([r[0] for r in kw_refs], axis=0)
    vwt = jnp.concatenate([r[0] for r in vwt_refs], axis=1)
    s_w = _dot_nt(jnp.concatenate([kw, wbias_ref[0]], axis=1), q_oh)
    e_w = jnp.exp2(s_w - jnp.max(s_w, axis=0, keepdims=True))
    o_w = _dot(vwt, e_w.astype(BF16)) * (1.0 / jnp.sum(e_w, axis=0, keepdims=True))

    psum = p_c[:, 0:NSA_TQ]
    for g in range(1, NSA_GROUP):
        psum = psum + p_c[:, g * NSA_TQ:(g + 1) * NSA_TQ]
    p_hi = psum.astype(BF16)
    p_lo = (psum - p_hi.astype(F32)).astype(BF16)
    imp = _dot(ovt_ref[...], p_hi) + _dot(ovt_ref[...], p_lo)

    bj = lax.broadcasted_iota(jnp.int32, (N_SEL, NSA_TQ), 0)
    cur = tq1 // SEL_BLOCK
    valid = bj * SEL_BLOCK <= tq1
    forced = (bj == 0) | (bj == cur) | (bj == cur - 1)
    score = jnp.where(valid, imp + FORCE_BONUS * jnp.where(forced, 1.0, 0.0), NEG_INF)
    bjf = bj.astype(F32)
    work = score
    picked = jnp.zeros(score.shape, F32)
    for _ in range(SEL_TOPK):
        mx = jnp.max(work, axis=0, keepdims=True)
        first = jnp.min(jnp.where(work == mx, bjf, float(N_SEL)), axis=0, keepdims=True)
        hit = bjf == first
        picked = jnp.where(hit, 1.0, picked)
        work = jnp.where(hit, -3e38, work)
    selected = (picked > 0.5) & (score > 0.5 * NEG_INF)
    bias = jnp.where(selected, 0.0, MASK_BIAS).T.astype(BF16)
    bias = jnp.concatenate([bias] * NSA_GROUP, axis=0)
    for half in range(N_SEL // LANE):
        qaug_ref[half, :, 0:NSA_HEAD_DIM] = q
        qaug_ref[half, :, NSA_HEAD_DIM:NSA_HEAD_DIM + LANE] = bias[:, half * LANE:(half + 1) * LANE]

    last = (q0 + NSA_TQ - 1) // NSA_TK

    def scores(kt):
        k0 = pl.multiple_of(kt * NSA_TK, NSA_TK)
        qa = qaug_ref[kt // (_SEL_HALF_KEYS // NSA_TK)]
        return _dot_nt(ks_ref[0, pl.ds(k0, NSA_TK), :], qa)

    def values(kt):
        return vst_ref[0, :, pl.ds(pl.multiple_of(kt * NSA_TK, NSA_TK), NSA_TK)]

    def causal_mask(kt):
        return kt * NSA_TK + lax.broadcasted_iota(jnp.int32, (NSA_TK, 1), 0) <= tq

    _flash_causal(scores, values, causal_mask, last, sa_ref, sb_ref, m_ref, l_ref, acc_ref)
    o_s = acc_ref[...] * (1.0 / l_ref[...])

    gt = gt_ref[0]
    gate = lambda br: jnp.concatenate(
        [gt[br * NSA_GROUP + g:br * NSA_GROUP + g + 1, :] for g in range(NSA_GROUP)], axis=1)
    out = (gate(0) * o_c + gate(1) * o_s + gate(2) * o_w).T
    for g in range(NSA_GROUP):
        o_ref[:, g * NSA_HEAD_DIM:(g + 1) * NSA_HEAD_DIM] = out[g * NSA_TQ:(g + 1) * NSA_TQ].astype(o_ref.dtype)


def _nsa(q, kc, vct, ovt, ks, vst, kw, vwt, gates):
    s = q.shape[1]
    d = NSA_HEAD_DIM
    assert (s // NSA_TK) % 2 == 0

    def win_specs(transposed):
        def spec(j):
            back = _WIN_TILES - 1 - j
            if transposed:
                return pl.BlockSpec((1, d, NSA_TQ), lambda h, i: (h, 0, jnp.maximum(i - back, 0)))
            return pl.BlockSpec((1, NSA_TQ, d), lambda h, i: (h, jnp.maximum(i - back, 0), 0))
        return [spec(j) for j in range(_WIN_TILES)]

    return pl.pallas_call(
        _nsa_kernel,
        grid=(NSA_KV_HEADS, s // NSA_TQ),
        in_specs=[
            pl.BlockSpec((NSA_GROUP, NSA_TQ, d), lambda h, i: (h, i, 0)),
            pl.BlockSpec((1, N_CMP_PAD, d), lambda h, i: (h, 0, 0)),
            pl.BlockSpec((1, d, N_CMP_PAD), lambda h, i: (h, 0, 0)),
            pl.BlockSpec((N_SEL, N_CMP_PAD), lambda h, i: (0, 0)),
            pl.BlockSpec((1, s, 2 * LANE), lambda h, i: (h, 0, 0)),
            pl.BlockSpec((1, d, s), lambda h, i: (h, 0, 0)),
            *win_specs(False), *win_specs(True),
            pl.BlockSpec((1, WINDOW + NSA_TQ, NSA_TQ), lambda h, i: (jnp.minimum(i, _WIN_TILES - 1), 0, 0)),
            pl.BlockSpec((NSA_ROWS, NSA_TQ), lambda h, i: (0, 0)),
            pl.BlockSpec((1, GATE_ROWS, NSA_TQ), lambda h, i: (h, 0, i)),
        ],
        out_specs=pl.BlockSpec((NSA_TQ, NSA_GROUP * d), lambda h, i: (i, h)),
        out_shape=jax.ShapeDtypeStruct((s, NSA_HEADS * d), BF16),
        scratch_shapes=[pltpu.VMEM((N_SEL // LANE, NSA_ROWS, NSA_HEAD_DIM + LANE), BF16),
                        pltpu.VMEM((NSA_TK, NSA_ROWS), F32), pltpu.VMEM((NSA_TK, NSA_ROWS), F32),
                        pltpu.VMEM((1, NSA_ROWS), F32), pltpu.VMEM((1, NSA_ROWS), F32),
                        pltpu.VMEM((d, NSA_ROWS), F32)],
        compiler_params=pltpu.CompilerParams(
            dimension_semantics=("parallel", "arbitrary"), vmem_limit_bytes=VMEM_LIMIT),
        name="nsa",
    )(q, kc, vct, ovt, ks, vst, *([kw] * _WIN_TILES), *([vwt] * _WIN_TILES), _window_bias(), _query_onehot(), gates)


def _window_bias():
    r = np.arange(WINDOW + NSA_TQ)[None, :, None]
    c = np.arange(NSA_TQ)[None, None, :]
    q0 = (np.arange(_WIN_TILES) * NSA_TQ)[:, None, None]
    back = WINDOW + c - r
    ok = (back >= 0) & (back < WINDOW) & ((q0 - WINDOW + r >= 0) | (q0 >= WINDOW))
    return jnp.asarray(np.where(ok, 0.0, NEG_INF), dtype=BF16)


def _query_onehot():
    return jnp.asarray(np.tile(np.eye(NSA_TQ, dtype=np.float32), (NSA_GROUP, 1)), dtype=BF16)


OUT_TM = 512


def _out_ln_kernel(om_ref, on_ref, wa_ref, wb_ref, x_ref, g_ref, b_ref, o_ref):
    mix = _dot(om_ref[...], wa_ref[...]) + _dot(on_ref[...], wb_ref[...])
    o_ref[...] = _layer_norm(DEEPNORM_ALPHA * x_ref[...] + mix, g_ref[...], b_ref[...])


def _out_ln(o_mla, o_nsa, wa, wb, x1, g, b):
    s, d = x1.shape
    tm = OUT_TM
    return pl.pallas_call(
        _out_ln_kernel,
        grid=(s // tm,),
        in_specs=[
            pl.BlockSpec((tm, o_mla.shape[1]), lambda i: (i, 0)),
            pl.BlockSpec((tm, o_nsa.shape[1]), lambda i: (i, 0)),
            _const_spec(wa.shape), _const_spec(wb.shape),
            pl.BlockSpec((tm, d), lambda i: (i, 0)),
            _const_spec(g.shape), _const_spec(b.shape),
        ],
        out_specs=pl.BlockSpec((tm, d), lambda i: (i, 0)),
        out_shape=jax.ShapeDtypeStruct((s, d), F32),
        compiler_params=pltpu.CompilerParams(
            dimension_semantics=("parallel",), vmem_limit_bytes=VMEM_LIMIT),
        name="out_ln",
    )(o_mla, o_nsa, wa, wb, x1, g, b)


def _pad_cols(w, n):
    return jnp.pad(w, ((0, 0), (0, n - w.shape[1])))


def _relayout_w_in(w_in, gate_b):
    sizes = (MLA_Q_RANK, MLA_KV_RANK, MLA_ROPE_DIM, NSA_HEADS * NSA_HEAD_DIM) + (NSA_KV_HEADS * NSA_HEAD_DIM,) * 6 \
        + (N_BRANCH * NSA_HEADS,)
    offs = [int(v) for v in np.cumsum(sizes)[:-1]]
    c_q, c_kv, k_r, q_n, k_c, v_c, k_s, v_s, k_w, v_w, g = jnp.split(w_in, offs, axis=1)
    half = MLA_ROPE_DIM // 2
    k_r_sw = jnp.concatenate([k_r[:, half:], k_r[:, :half]], axis=1)
    gate_idx = np.array([[br * NSA_HEADS + hk * NSA_GROUP + gi for br in range(N_BRANCH) for gi in range(NSA_GROUP)]
                         for hk in range(NSA_KV_HEADS)])
    g_chunks = [_pad_cols(g[:, gate_idx[hk]], LANE) for hk in range(NSA_KV_HEADS)]
    w = jnp.concatenate([c_q, c_kv, q_n, k_c, v_c, k_s, v_s, k_w, v_w,
                         _pad_cols(k_r, LANE), _pad_cols(k_r_sw, LANE)] + g_chunks, axis=1)
    assert w.shape[1] == _PROJ_COLS
    gb = jnp.stack([jnp.pad(gate_b[gate_idx[hk]], (0, LANE - gate_idx.shape[1])) for hk in range(NSA_KV_HEADS)])
    return w.astype(BF16), gb.reshape(NSA_KV_HEADS, 1, LANE)


def _relayout_w_uq(w_uq):
    dh = MLA_NOPE_DIM + MLA_ROPE_DIM
    half = MLA_ROPE_DIM // 2
    w = w_uq.reshape(MLA_Q_RANK, MLA_HEADS, dh)
    nope = w[:, :, :MLA_NOPE_DIM]
    r = w[:, :, MLA_NOPE_DIM:]
    r_sw = jnp.concatenate([r[:, :, half:], r[:, :, :half]], axis=2)
    pad = lambda t: jnp.pad(t, ((0, 0), (0, 0), (0, LANE - MLA_ROPE_DIM)))
    parts = [nope, pad(r), pad(r_sw)]
    return jnp.concatenate([p.reshape(MLA_Q_RANK, MLA_HEADS * LANE) for p in parts], axis=1).astype(BF16)


def _relayout_w_ukv(w_ukv):
    w = w_ukv.reshape(MLA_KV_RANK, MLA_HEADS, MLA_NOPE_DIM + MLA_V_DIM)
    k = w[:, :, :MLA_NOPE_DIM].reshape(MLA_KV_RANK, MLA_HEADS * MLA_NOPE_DIM)
    v = w[:, :, MLA_NOPE_DIM:].reshape(MLA_KV_RANK, MLA_HEADS * MLA_V_DIM)
    return jnp.concatenate([k, v], axis=1).astype(BF16)


def _rope_tables(s):
    pos = jnp.arange(s).astype(F32)[:, None]

    def cs(d):
        half = d // 2
        inv = ROPE_THETA ** (-jnp.arange(half, dtype=F32) * (2.0 / d))
        ang = pos * inv[None, :]
        return jnp.cos(ang), jnp.sin(ang)

    cm, sm = cs(MLA_ROPE_DIM)
    cn, sn = cs(NSA_HEAD_DIM)
    cosm = _pad_cols(jnp.concatenate([cm, cm], axis=1), LANE)
    sinm = _pad_cols(jnp.concatenate([-sm, sm], axis=1), LANE)
    cosn = jnp.concatenate([cn, cn], axis=1)
    sinn = jnp.concatenate([-sn, sn], axis=1)
    return cosm, sinm, cosn, sinn


def _overlap_weights():
    ci = np.arange(N_CMP_PAD)[:, None] * CMP_STRIDE
    sj = np.arange(N_SEL)[None, :] * SEL_BLOCK
    ov = np.clip(np.minimum(ci + CMP_BLOCK, sj + SEL_BLOCK) - np.maximum(ci, sj), 0, None)
    ov[N_CMP:] = 0
    return jnp.asarray((ov.astype(np.float32) / CMP_STRIDE).T, dtype=BF16)


def kernel(x, ffn1_w_gate, ffn1_w_up, ffn1_w_down, ln1_g, ln1_b, w_in, mla_q_norm_g, mla_w_uq, mla_kv_norm_g,
           mla_w_ukv, nsa_gate_b, nsa_cmp_pe_k, nsa_cmp_w1_k, nsa_cmp_w2_k, nsa_cmp_pe_v, nsa_cmp_w1_v,
           nsa_cmp_w2_v, w_out, ln2_g, ln2_b, ffn2_w_gate, ffn2_w_up, ffn2_w_down, ln3_g, ln3_b):
    b, s, d = x.shape
    assert (b, s, d) == (1, SEQ, D_MODEL) and ffn1_w_gate.shape[0] == DEPTH
    l = 0
    row = lambda v: v.reshape(1, -1)
    xs = x.reshape(s, d)

    x1 = _ffn_ln(xs, ffn1_w_gate[l].astype(BF16), ffn1_w_up[l].astype(BF16), ffn1_w_down[l].astype(BF16),
                 row(ln1_g[l]), row(ln1_b[l]))

    win_p, gb_p = _relayout_w_in(w_in[l], nsa_gate_b[l])
    cosm, sinm, cosn, sinn = _rope_tables(s)
    (q_m, k_m, v_m, q_n, kc_in, vc_in, ks, vs, kw, vw, gates) = _proj(
        x1, win_p, row(mla_q_norm_g[l]), row(mla_kv_norm_g[l]),
        _relayout_w_uq(mla_w_uq[l]), _relayout_w_ukv(mla_w_ukv[l]), gb_p, cosm, sinm, cosn, sinn)

    def compress(t, pe, w1, w2, transpose_out):
        t16 = t.reshape(NSA_KV_HEADS, N_CMP_PAD, _CMP_HALF)
        pe8 = jnp.broadcast_to(pe.reshape(1, CMP_BLOCK * NSA_HEAD_DIM), (8, CMP_BLOCK * NSA_HEAD_DIM))
        return _compress(t16, pe8, w1, w2, transpose_out)

    kc = compress(kc_in, nsa_cmp_pe_k[l], nsa_cmp_w1_k[l], nsa_cmp_w2_k[l], False)
    vc = compress(vc_in, nsa_cmp_pe_v[l], nsa_cmp_w1_v[l], nsa_cmp_w2_v[l], True)

    o_mla = _mla(q_m, k_m, v_m)
    o_nsa = _nsa(q_n, kc, vc, _overlap_weights(), ks, vs, kw, vw, gates)

    n_mla = MLA_HEADS * MLA_V_DIM
    w_o = w_out[l].astype(BF16)
    x2 = _out_ln(o_mla, o_nsa, w_o[:n_mla], w_o[n_mla:], x1, row(ln2_g[l]), row(ln2_b[l]))

    x3 = _ffn_ln(x2, ffn2_w_gate[l].astype(BF16), ffn2_w_up[l].astype(BF16), ffn2_w_down[l].astype(BF16),
                 row(ln3_g[l]), row(ln3_b[l]))
    return x3.reshape(b, s, d)
```

```python
import functools

import jax
import jax.numpy as jnp
import numpy as np
from jax import lax
from jax.experimental import pallas as pl
from jax.experimental.pallas import tpu as pltpu

F32 = jnp.float32
BF16 = jnp.bfloat16

D_MODEL = 2048
SEQ = 16384
DEPTH = 1
DEEPNORM_ALPHA = (2.0 * DEPTH) ** 0.25
LN_EPS = 1e-5
RMS_EPS = 1e-6
ROPE_THETA = 10000.0
NEG_INF = -1e30
D_FF = 5632
FFN_RES_WEIGHT = 0.5

MLA_HEADS = 8
MLA_Q_RANK = 512
MLA_KV_RANK = 256
MLA_NOPE_DIM = 128
MLA_ROPE_DIM = 64
MLA_V_DIM = 128
MLA_QK_PAD = 256

NSA_HEADS = 8
NSA_KV_HEADS = 2
NSA_HEAD_DIM = 128
NSA_GROUP = NSA_HEADS // NSA_KV_HEADS
CMP_BLOCK = 32
CMP_STRIDE = 16
CMP_HIDDEN = 256
SEL_BLOCK = 64
SEL_TOPK = 16
WINDOW = 512
N_BRANCH = 3
FORCE_BONUS = 1e4
N_CMP = (SEQ - CMP_BLOCK) // CMP_STRIDE + 1
N_CMP_PAD = SEQ // CMP_STRIDE
N_SEL = SEQ // SEL_BLOCK

LANE = 128
MASK_BIAS = NEG_INF
TOPK_TAKEN = -3e38
M_INIT = -1e30

VMEM_LIMIT = 56 * 1024 * 1024

NT_DIMS = (((1,), (1,)), ((), ()))


def _dot(a, b, **kw):
    return jnp.dot(a, b, preferred_element_type=F32, **kw)


def _dot_nt(a, b):
    return lax.dot_general(a, b, NT_DIMS, preferred_element_type=F32)


def _layer_norm(y, g, b):
    mu = jnp.mean(y, axis=-1, keepdims=True)
    yc = y - mu
    var = jnp.mean(yc * yc, axis=-1, keepdims=True)
    return yc * lax.rsqrt(var + LN_EPS) * g + b


def _rms_norm(x, g):
    ms = jnp.mean(x * x, axis=-1, keepdims=True)
    return x * lax.rsqrt(ms + RMS_EPS) * g


FFN_TM = 512
FFN_TF = 512


def _ffn_ln_kernel(x_ref, wg_ref, wu_ref, wd_ref, g_ref, b_ref, o_ref, xb_ref, acc_ref):
    k = pl.program_id(1)

    @pl.when(k == 0)
    def _():
        xb_ref[...] = x_ref[...].astype(BF16)
        acc_ref[...] = jnp.zeros_like(acc_ref)

    xb = xb_ref[...]
    gate = _dot(xb, wg_ref[...])
    up = _dot(xb, wu_ref[...])
    h = (jax.nn.silu(gate) * up).astype(BF16)
    acc_ref[...] += _dot(h, wd_ref[...])

    @pl.when(k == pl.num_programs(1) - 1)
    def _():
        y = DEEPNORM_ALPHA * x_ref[...] + FFN_RES_WEIGHT * acc_ref[...]
        o_ref[...] = _layer_norm(y, g_ref[...], b_ref[...])


def _ffn_ln(x, wg, wu, wd, g, b):
    s, d = x.shape
    f = wg.shape[1]
    grid = (s // FFN_TM, f // FFN_TF)
    return pl.pallas_call(
        _ffn_ln_kernel,
        grid=grid,
        in_specs=[
            pl.BlockSpec((FFN_TM, d), lambda i, k: (i, 0)),
            pl.BlockSpec((d, FFN_TF), lambda i, k: (0, k)),
            pl.BlockSpec((d, FFN_TF), lambda i, k: (0, k)),
            pl.BlockSpec((FFN_TF, d), lambda i, k: (k, 0)),
            pl.BlockSpec((1, d), lambda i, k: (0, 0)),
            pl.BlockSpec((1, d), lambda i, k: (0, 0)),
        ],
        out_specs=pl.BlockSpec((FFN_TM, d), lambda i, k: (i, 0)),
        out_shape=jax.ShapeDtypeStruct((s, d), F32),
        scratch_shapes=[pltpu.VMEM((FFN_TM, d), BF16), pltpu.VMEM((FFN_TM, d), F32)],
        compiler_params=pltpu.CompilerParams(
            dimension_semantics=("parallel", "arbitrary"), vmem_limit_bytes=VMEM_LIMIT),
        name="ffn_ln",
    )(x, wg, wu, wd, g, b)


PROJ_TM = 256
_O_CQ = 0
_O_CKV = _O_CQ + MLA_Q_RANK
_O_NQ = _O_CKV + MLA_KV_RANK
_O_KC = _O_NQ + NSA_HEADS * NSA_HEAD_DIM
_O_VC = _O_KC + NSA_KV_HEADS * NSA_HEAD_DIM
_O_KS = _O_VC + NSA_KV_HEADS * NSA_HEAD_DIM
_O_VS = _O_KS + NSA_KV_HEADS * NSA_HEAD_DIM
_O_KW = _O_VS + NSA_KV_HEADS * NSA_HEAD_DIM
_O_VW = _O_KW + NSA_KV_HEADS * NSA_HEAD_DIM
_O_KR = _O_VW + NSA_KV_HEADS * NSA_HEAD_DIM
_O_KRS = _O_KR + LANE
_O_G = _O_KRS + LANE
_PROJ_COLS = _O_G + NSA_KV_HEADS * LANE
GATE_ROWS = 16

LOG2E = 1.4426950408889634
MLA_SCALE = (MLA_NOPE_DIM + MLA_ROPE_DIM) ** -0.5 * LOG2E
NSA_SCALE = NSA_HEAD_DIM ** -0.5 * LOG2E


def _proj_kernel(x_ref, win_ref, gq_ref, gkv_ref, wuq_ref, wukv_ref, gb_ref,
                 cosm_ref, sinm_ref, cosn_ref, sinn_ref,
                 qm_ref, km_ref, vm_ref, qn_ref, kc_ref, vc_ref, ks_ref, vs_ref, kw_ref, vw_ref, gt_ref):
    tm = x_ref.shape[0]
    xb = x_ref[...].astype(BF16)
    h = _dot(xb, win_ref[...])
    cosm, sinm = cosm_ref[...], sinm_ref[...]
    cosn, sinn = cosn_ref[...], sinn_ref[...]

    cqn = _rms_norm(h[:, _O_CQ:_O_CQ + MLA_Q_RANK], gq_ref[...]).astype(BF16)
    qall = _dot(cqn, wuq_ref[...])
    hw = MLA_HEADS * LANE
    for hd in range(MLA_HEADS):
        sl = slice(hd * LANE, (hd + 1) * LANE)
        qm_ref[hd, :, 0:LANE] = (qall[:, sl] * MLA_SCALE).astype(BF16)
        qr = qall[:, hw + hd * LANE: hw + (hd + 1) * LANE]
        qrs = qall[:, 2 * hw + hd * LANE: 2 * hw + (hd + 1) * LANE]
        qm_ref[hd, :, LANE:2 * LANE] = ((qr * cosm + qrs * sinm) * MLA_SCALE).astype(BF16)

    ckvn = _rms_norm(h[:, _O_CKV:_O_CKV + MLA_KV_RANK], gkv_ref[...]).astype(BF16)
    kv = _dot(ckvn, wukv_ref[...])
    kr = (h[:, _O_KR:_O_KR + LANE] * cosm + h[:, _O_KRS:_O_KRS + LANE] * sinm).astype(BF16)
    for hd in range(MLA_HEADS):
        km_ref[hd, :, 0:LANE] = kv[:, hd * LANE:(hd + 1) * LANE].astype(BF16)
        km_ref[hd, :, LANE:2 * LANE] = kr
        vm_ref[hd] = kv[:, hw + hd * LANE: hw + (hd + 1) * LANE].T.astype(BF16)

    def rope128(t):
        return t * cosn + pltpu.roll(t, NSA_HEAD_DIM // 2, 1) * sinn

    for hd in range(NSA_HEADS):
        t = h[:, _O_NQ + hd * LANE:_O_NQ + (hd + 1) * LANE]
        qn_ref[hd] = (rope128(t) * NSA_SCALE).astype(BF16)

    row0 = pl.program_id(0) * tm
    kpos = row0 + lax.broadcasted_iota(jnp.int32, (tm, LANE), 0)
    lane = lax.broadcasted_iota(jnp.int32, (tm, LANE), 1)
    onehot = jnp.where(((kpos // SEL_BLOCK) % LANE) == lane, 1.0, 0.0).astype(BF16)
    for hk in range(NSA_KV_HEADS):
        sl = lambda off: slice(off + hk * LANE, off + (hk + 1) * LANE)
        kc_ref[hk] = rope128(h[:, sl(_O_KC)])
        vc_ref[hk] = h[:, sl(_O_VC)]
        ks_ref[hk, :, 0:LANE] = rope128(h[:, sl(_O_KS)]).astype(BF16)
        ks_ref[hk, :, LANE:2 * LANE] = onehot
        vs_ref[hk] = h[:, sl(_O_VS)].T.astype(BF16)
        kw_ref[hk] = rope128(h[:, sl(_O_KW)]).astype(BF16)
        vw_ref[hk] = h[:, sl(_O_VW)].T.astype(BF16)
        gt_ref[hk] = jax.nn.sigmoid(h[:, sl(_O_G)] + gb_ref[hk]).T[0:GATE_ROWS]


def _const_spec(shape):
    nd = len(shape)
    return pl.BlockSpec(shape, lambda i: (0,) * nd)


def _proj(x1, win_p, gq, gkv, wuq_all, wukv_p, gb_p, cosm, sinm, cosn, sinn):
    s, d = x1.shape
    tm = PROJ_TM
    grid = (s // tm,)
    row_spec = lambda w: pl.BlockSpec((tm, w), lambda i: (i, 0))
    head_spec = lambda nh, w: pl.BlockSpec((nh, tm, w), lambda i: (0, i, 0))
    out_shape = (
        jax.ShapeDtypeStruct((MLA_HEADS, s, MLA_QK_PAD), BF16),
        jax.ShapeDtypeStruct((MLA_HEADS, s, MLA_QK_PAD), BF16),
        jax.ShapeDtypeStruct((MLA_HEADS, MLA_V_DIM, s), BF16),
        jax.ShapeDtypeStruct((NSA_HEADS, s, NSA_HEAD_DIM), BF16),
        jax.ShapeDtypeStruct((NSA_KV_HEADS, s, NSA_HEAD_DIM), F32),
        jax.ShapeDtypeStruct((NSA_KV_HEADS, s, NSA_HEAD_DIM), F32),
        jax.ShapeDtypeStruct((NSA_KV_HEADS, s, 2 * LANE), BF16),
        jax.ShapeDtypeStruct((NSA_KV_HEADS, NSA_HEAD_DIM, s), BF16),
        jax.ShapeDtypeStruct((NSA_KV_HEADS, s, NSA_HEAD_DIM), BF16),
        jax.ShapeDtypeStruct((NSA_KV_HEADS, NSA_HEAD_DIM, s), BF16),
        jax.ShapeDtypeStruct((NSA_KV_HEADS, GATE_ROWS, s), F32),
    )
    head_spec_t = lambda nh, r: pl.BlockSpec((nh, r, tm), lambda i: (0, 0, i))
    out_specs = (
        head_spec(MLA_HEADS, MLA_QK_PAD), head_spec(MLA_HEADS, MLA_QK_PAD), head_spec_t(MLA_HEADS, MLA_V_DIM),
        head_spec(NSA_HEADS, NSA_HEAD_DIM),
        head_spec(NSA_KV_HEADS, NSA_HEAD_DIM), head_spec(NSA_KV_HEADS, NSA_HEAD_DIM),
        head_spec(NSA_KV_HEADS, 2 * LANE), head_spec_t(NSA_KV_HEADS, NSA_HEAD_DIM),
        head_spec(NSA_KV_HEADS, NSA_HEAD_DIM), head_spec_t(NSA_KV_HEADS, NSA_HEAD_DIM),
        head_spec_t(NSA_KV_HEADS, GATE_ROWS),
    )
    return pl.pallas_call(
        _proj_kernel,
        grid=grid,
        in_specs=[
            row_spec(d),
            _const_spec(win_p.shape), _const_spec(gq.shape), _const_spec(gkv.shape),
            _const_spec(wuq_all.shape), _const_spec(wukv_p.shape), _const_spec(gb_p.shape),
            row_spec(LANE), row_spec(LANE), row_spec(LANE), row_spec(LANE),
        ],
        out_specs=out_specs,
        out_shape=out_shape,
        compiler_params=pltpu.CompilerParams(
            dimension_semantics=("parallel",), vmem_limit_bytes=VMEM_LIMIT),
        name="proj",
    )(x1, win_p, gq, gkv, wuq_all, wukv_p, gb_p, cosm, sinm, cosn, sinn)


_CMP_HALF = CMP_STRIDE * NSA_HEAD_DIM


def _compress_kernel(t_ref, pe_ref, w1_ref, w2_ref, o_ref, *, transpose_out):
    hi = lax.Precision.HIGHEST
    t = t_ref[0]
    w1a = w1_ref[0:_CMP_HALF, :]
    w1b = w1_ref[_CMP_HALF:2 * _CMP_HALF, :]
    first = _dot(t, w1a, precision=hi)
    second = _dot(t, w1b, precision=hi)
    pe_row = _dot(pe_ref[...], w1_ref[...], precision=hi)[0:1]
    hid = first + pltpu.roll(second, N_CMP_PAD - 1, 0) + pe_row
    out = _dot(jax.nn.gelu(hid), w2_ref[...], precision=hi)
    o_ref[0] = (out.T if transpose_out else out).astype(o_ref.dtype)


def _compress(t16, pe_flat8, w1, w2, transpose_out):
    out_dims = (NSA_HEAD_DIM, N_CMP_PAD) if transpose_out else (N_CMP_PAD, NSA_HEAD_DIM)
    return pl.pallas_call(
        functools.partial(_compress_kernel, transpose_out=transpose_out),
        grid=(NSA_KV_HEADS,),
        in_specs=[
            pl.BlockSpec((1, N_CMP_PAD, _CMP_HALF), lambda i: (i, 0, 0)),
            _const_spec(pe_flat8.shape), _const_spec(w1.shape), _const_spec(w2.shape),
        ],
        out_specs=pl.BlockSpec((1,) + out_dims, lambda i: (i, 0, 0)),
        out_shape=jax.ShapeDtypeStruct((NSA_KV_HEADS,) + out_dims, BF16),
        compiler_params=pltpu.CompilerParams(
            dimension_semantics=("parallel",), vmem_limit_bytes=VMEM_LIMIT),
        name="compress",
    )(t16, pe_flat8, w1, w2)


def _flash_init(m_ref, l_ref, acc_ref):
    m_ref[...] = jnp.full_like(m_ref, M_INIT)
    l_ref[...] = jnp.zeros_like(l_ref)
    acc_ref[...] = jnp.zeros_like(acc_ref)


def _flash_update(st, vt, m_ref, l_ref, acc_ref):
    m_prev = m_ref[...]
    m_new = jnp.maximum(m_prev, jnp.max(st, axis=0, keepdims=True))
    alpha = jnp.exp2(m_prev - m_new)
    p = jnp.exp2(st - m_new)
    l_ref[...] = alpha * l_ref[...] + jnp.sum(p, axis=0, keepdims=True)
    acc_ref[...] = alpha * acc_ref[...] + _dot(vt, p.astype(BF16))
    m_ref[...] = m_new


FLASH_PAIRS_PER_TRIP = 2


def _flash_causal(scores, values, causal_mask, n_plain, last_tile, two_diagonal_tiles,
                  sa_ref, sb_ref, m_ref, l_ref, acc_ref):
    _flash_init(m_ref, l_ref, acc_ref)
    sa_ref[...] = scores(0)

    def plain_pair(t):
        sb_ref[...] = scores(t + 1)
        _flash_update(sa_ref[...], values(t), m_ref, l_ref, acc_ref)
        sa_ref[...] = scores(t + 2)
        _flash_update(sb_ref[...], values(t + 1), m_ref, l_ref, acc_ref)

    def body(j, carry):
        for u in range(FLASH_PAIRS_PER_TRIP):
            plain_pair(2 * (FLASH_PAIRS_PER_TRIP * j + u))
        return carry

    n_pairs = n_plain // 2
    n_trips = n_pairs // FLASH_PAIRS_PER_TRIP
    lax.fori_loop(0, n_trips, body, 0)
    for u in range(FLASH_PAIRS_PER_TRIP - 1):
        @pl.when(n_trips * FLASH_PAIRS_PER_TRIP + u < n_pairs)
        def _():
            plain_pair(2 * (n_trips * FLASH_PAIRS_PER_TRIP + u))
    t = 2 * n_pairs

    def masked_update(tile, st):
        _flash_update(jnp.where(causal_mask(tile), st, MASK_BIAS), values(tile), m_ref, l_ref, acc_ref)

    if two_diagonal_tiles:
        sb_ref[...] = scores(t + 1)
        masked_update(t, sa_ref[...])
        masked_update(t + 1, sb_ref[...])
    else:
        masked_update(t, sa_ref[...])

        @pl.when(t + 1 <= last_tile)
        def _():
            masked_update(t + 1, scores(t + 1))


def _masked_probs_t(st, mask):
    st = jnp.where(mask, st, NEG_INF)
    m = jnp.max(st, axis=0, keepdims=True)
    p = jnp.where(mask, jnp.exp2(st - m), 0.0)
    return p * (1.0 / jnp.maximum(jnp.sum(p, axis=0, keepdims=True), 1e-30))


MLA_TQ = 1024
MLA_TK = 512


def _mla_kernel(q_ref, k_ref, vt_ref, o_ref, sa_ref, sb_ref, m_ref, l_ref, acc_ref):
    i = pl.program_id(1)
    q = q_ref[0]

    def scores(kt):
        k0 = pl.multiple_of(kt * MLA_TK, MLA_TK)
        return _dot_nt(k_ref[0, pl.ds(k0, MLA_TK), :], q)

    def values(kt):
        return vt_ref[0, :, pl.ds(pl.multiple_of(kt * MLA_TK, MLA_TK), MLA_TK)]

    qpos = i * MLA_TQ + lax.broadcasted_iota(jnp.int32, (1, MLA_TQ), 1)

    def causal_mask(kt):
        return kt * MLA_TK + lax.broadcasted_iota(jnp.int32, (MLA_TK, 1), 0) <= qpos

    tiles_per_q = MLA_TQ // MLA_TK
    _flash_causal(scores, values, causal_mask, i * tiles_per_q, (i + 1) * tiles_per_q - 1, tiles_per_q == 2,
                  sa_ref, sb_ref, m_ref, l_ref, acc_ref)
    o_ref[...] = (acc_ref[...] * (1.0 / l_ref[...])).T.astype(o_ref.dtype)


def _mla(q, k, vt):
    nh, s, dq = q.shape
    dv = vt.shape[1]
    assert MLA_TQ in (MLA_TK, 2 * MLA_TK) and s % MLA_TQ == 0
    return pl.pallas_call(
        _mla_kernel,
        grid=(nh, s // MLA_TQ),
        in_specs=[
            pl.BlockSpec((1, MLA_TQ, dq), lambda h, i: (h, i, 0)),
            pl.BlockSpec((1, s, dq), lambda h, i: (h, 0, 0)),
            pl.BlockSpec((1, dv, s), lambda h, i: (h, 0, 0)),
        ],
        out_specs=pl.BlockSpec((MLA_TQ, dv), lambda h, i: (i, h)),
        out_shape=jax.ShapeDtypeStruct((s, nh * dv), BF16),
        scratch_shapes=[pltpu.VMEM((MLA_TK, MLA_TQ), F32), pltpu.VMEM((MLA_TK, MLA_TQ), F32),
                        pltpu.VMEM((1, MLA_TQ), F32), pltpu.VMEM((1, MLA_TQ), F32),
                        pltpu.VMEM((dv, MLA_TQ), F32)],
        compiler_params=pltpu.CompilerParams(
            dimension_semantics=("parallel", "arbitrary"), vmem_limit_bytes=VMEM_LIMIT),
        name="mla",
    )(q, k, vt)


NSA_TQ = 256
NSA_ROWS = NSA_GROUP * NSA_TQ
NSA_TK = 512
_SEL_HALF_KEYS = LANE * SEL_BLOCK
_WIN_TILES = WINDOW // NSA_TQ + 1


def _nsa_kernel(q_ref, kc_ref, vct_ref, ovt_ref, ks_ref, vst_ref, *rest):
    kw_refs = rest[:_WIN_TILES]
    vwt_refs = rest[_WIN_TILES:2 * _WIN_TILES]
    wbias_ref, qoh_ref, gt_ref, o_ref, qaug_ref, sa_ref, sb_ref, m_ref, l_ref, acc_ref = rest[2 * _WIN_TILES:]

    i = pl.program_id(1)
    q0 = i * NSA_TQ
    q = q_ref[...].reshape(NSA_ROWS, NSA_HEAD_DIM)
    tq = q0 + (lax.broadcasted_iota(jnp.int32, (1, NSA_ROWS), 1) & (NSA_TQ - 1))
    tq1 = q0 + lax.broadcasted_iota(jnp.int32, (1, NSA_TQ), 1)

    q_oh = jnp.concatenate([q, qoh_ref[...]], axis=1)

    cend = lax.broadcasted_iota(jnp.int32, (N_CMP_PAD, NSA_TQ), 0) * CMP_STRIDE + (CMP_BLOCK - 1)
    cbias = jnp.where(cend <= tq1, 0.0, NEG_INF).astype(BF16)
    s_c = _dot_nt(jnp.concatenate([kc_ref[0], cbias], axis=1), q_oh)
    e_c = jnp.exp2(s_c - jnp.max(s_c, axis=0, keepdims=True))
    inv_c = jnp.where(tq >= CMP_BLOCK - 1, 1.0 / jnp.sum(e_c, axis=0, keepdims=True), 0.0)
    p_c = e_c * inv_c
    o_c = _dot(vct_ref[0], p_c.astype(BF16))

    kw = jnp.concatenate([r[0] for r in kw_refs], axis=0)
    vwt = jnp.concatenate([r[0] for r in vwt_refs], axis=1)
    s_w = _dot_nt(jnp.concatenate([kw, wbias_ref[0]], axis=1), q_oh)
    e_w = jnp.exp2(s_w - jnp.max(s_w, axis=0, keepdims=True))
    o_w = _dot(vwt, e_w.astype(BF16)) * (1.0 / jnp.sum(e_w, axis=0, keepdims=True))

    psum = p_c[:, 0:NSA_TQ]
    for g in range(1, NSA_GROUP):
        psum = psum + p_c[:, g * NSA_TQ:(g + 1) * NSA_TQ]
    p_hi = psum.astype(BF16)
    p_lo = (psum - p_hi.astype(F32)).astype(BF16)
    imp = _dot(ovt_ref[...], p_hi) + _dot(ovt_ref[...], p_lo)

    bj = lax.broadcasted_iota(jnp.int32, (N_SEL, NSA_TQ), 0)
    cur = tq1 // SEL_BLOCK
    valid = bj * SEL_BLOCK <= tq1
    forced = (bj == 0) | (bj == cur) | (bj == cur - 1)
    score = jnp.where(valid, imp + FORCE_BONUS * jnp.where(forced, 1.0, 0.0), NEG_INF)
    bjf = bj.astype(F32)
    work = score
    for _ in range(SEL_TOPK):
        mx = jnp.max(work, axis=0, keepdims=True)
        first = jnp.min(jnp.where(work == mx, bjf, float(N_SEL)), axis=0, keepdims=True)
        work = jnp.where(bjf == first, TOPK_TAKEN, work)
    selected = (work == TOPK_TAKEN) & (score > 0.5 * NEG_INF)
    bias = jnp.where(selected, 0.0, MASK_BIAS).T.astype(BF16)
    bias = jnp.concatenate([bias] * NSA_GROUP, axis=0)
    for half in range(N_SEL // LANE):
        qaug_ref[half, :, 0:NSA_HEAD_DIM] = q
        qaug_ref[half, :, NSA_HEAD_DIM:NSA_HEAD_DIM + LANE] = bias[:, half * LANE:(half + 1) * LANE]

    last = (q0 + NSA_TQ - 1) // NSA_TK

    def scores(kt):
        k0 = pl.multiple_of(kt * NSA_TK, NSA_TK)
        qa = qaug_ref[kt // (_SEL_HALF_KEYS // NSA_TK)]
        return _dot_nt(ks_ref[0, pl.ds(k0, NSA_TK), :], qa)

    def values(kt):
        return vst_ref[0, :, pl.ds(pl.multiple_of(kt * NSA_TK, NSA_TK), NSA_TK)]

    def causal_mask(kt):
        return kt * NSA_TK + lax.broadcasted_iota(jnp.int32, (NSA_TK, 1), 0) <= tq

    _flash_causal(scores, values, causal_mask, last, last, False, sa_ref, sb_ref, m_ref, l_ref, acc_ref)
    o_s = acc_ref[...] * (1.0 / l_ref[...])

    gt = gt_ref[0]
    gate = lambda br: jnp.concatenate(
        [gt[br * NSA_GROUP + g:br * NSA_GROUP + g + 1, :] for g in range(NSA_GROUP)], axis=1)
    out = (gate(0) * o_c + gate(1) * o_s + gate(2) * o_w).T
    for g in range(NSA_GROUP):
        o_ref[:, g * NSA_HEAD_DIM:(g + 1) * NSA_HEAD_DIM] = out[g * NSA_TQ:(g + 1) * NSA_TQ].astype(o_ref.dtype)


def _nsa(q, kc, vct, ovt, ks, vst, kw, vwt, gates):
    s = q.shape[1]
    d = NSA_HEAD_DIM
    assert NSA_TQ <= NSA_TK and NSA_TK % NSA_TQ == 0

    def win_specs(transposed):
        def spec(j):
            back = _WIN_TILES - 1 - j
            if transposed:
                return pl.BlockSpec((1, d, NSA_TQ), lambda h, i: (h, 0, jnp.maximum(i - back, 0)))
            return pl.BlockSpec((1, NSA_TQ, d), lambda h, i: (h, jnp.maximum(i - back, 0), 0))
        return [spec(j) for j in range(_WIN_TILES)]

    return pl.pallas_call(
        _nsa_kernel,
        grid=(NSA_KV_HEADS, s // NSA_TQ),
        in_specs=[
            pl.BlockSpec((NSA_GROUP, NSA_TQ, d), lambda h, i: (h, i, 0)),
            pl.BlockSpec((1, N_CMP_PAD, d), lambda h, i: (h, 0, 0)),
            pl.BlockSpec((1, d, N_CMP_PAD), lambda h, i: (h, 0, 0)),
            pl.BlockSpec((N_SEL, N_CMP_PAD), lambda h, i: (0, 0)),
            pl.BlockSpec((1, s, 2 * LANE), lambda h, i: (h, 0, 0)),
            pl.BlockSpec((1, d, s), lambda h, i: (h, 0, 0)),
            *win_specs(False), *win_specs(True),
            pl.BlockSpec((1, WINDOW + NSA_TQ, NSA_TQ), lambda h, i: (jnp.minimum(i, _WIN_TILES - 1), 0, 0)),
            pl.BlockSpec((NSA_ROWS, NSA_TQ), lambda h, i: (0, 0)),
            pl.BlockSpec((1, GATE_ROWS, NSA_TQ), lambda h, i: (h, 0, i)),
        ],
        out_specs=pl.BlockSpec((NSA_TQ, NSA_GROUP * d), lambda h, i: (i, h)),
        out_shape=jax.ShapeDtypeStruct((s, NSA_HEADS * d), BF16),
        scratch_shapes=[pltpu.VMEM((N_SEL // LANE, NSA_ROWS, NSA_HEAD_DIM + LANE), BF16),
                        pltpu.VMEM((NSA_TK, NSA_ROWS), F32), pltpu.VMEM((NSA_TK, NSA_ROWS), F32),
                        pltpu.VMEM((1, NSA_ROWS), F32), pltpu.VMEM((1, NSA_ROWS), F32),
                        pltpu.VMEM((d, NSA_ROWS), F32)],
        compiler_params=pltpu.CompilerParams(
            dimension_semantics=("parallel", "arbitrary"), vmem_limit_bytes=VMEM_LIMIT),
        name="nsa",
    )(q, kc, vct, ovt, ks, vst, *([kw] * _WIN_TILES), *([vwt] * _WIN_TILES), _window_bias(), _query_onehot(), gates)


def _window_bias():
    r = np.arange(WINDOW + NSA_TQ)[None, :, None]
    c = np.arange(NSA_TQ)[None, None, :]
    q0 = (np.arange(_WIN_TILES) * NSA_TQ)[:, None, None]
    back = WINDOW + c - r
    ok = (back >= 0) & (back < WINDOW) & ((q0 - WINDOW + r >= 0) | (q0 >= WINDOW))
    return jnp.asarray(np.where(ok, 0.0, NEG_INF), dtype=BF16)


def _query_onehot():
    return jnp.asarray(np.tile(np.eye(NSA_TQ, dtype=np.float32), (NSA_GROUP, 1)), dtype=BF16)


OUT_TM = 512


def _out_ln_kernel(om_ref, on_ref, wa_ref, wb_ref, x_ref, g_ref, b_ref, o_ref):
    mix = _dot(om_ref[...], wa_ref[...]) + _dot(on_ref[...], wb_ref[...])
    o_ref[...] = _layer_norm(DEEPNORM_ALPHA * x_ref[...] + mix, g_ref[...], b_ref[...])


def _out_ln(o_mla, o_nsa, wa, wb, x1, g, b):
    s, d = x1.shape
    tm = OUT_TM
    return pl.pallas_call(
        _out_ln_kernel,
        grid=(s // tm,),
        in_specs=[
            pl.BlockSpec((tm, o_mla.shape[1]), lambda i: (i, 0)),
            pl.BlockSpec((tm, o_nsa.shape[1]), lambda i: (i, 0)),
            _const_spec(wa.shape), _const_spec(wb.shape),
            pl.BlockSpec((tm, d), lambda i: (i, 0)),
            _const_spec(g.shape), _const_spec(b.shape),
        ],
        out_specs=pl.BlockSpec((tm, d), lambda i: (i, 0)),
        out_shape=jax.ShapeDtypeStruct((s, d), F32),
        compiler_params=pltpu.CompilerParams(
            dimension_semantics=("parallel",), vmem_limit_bytes=VMEM_LIMIT),
        name="out_ln",
    )(o_mla, o_nsa, wa, wb, x1, g, b)


def _pad_cols(w, n):
    return jnp.pad(w, ((0, 0), (0, n - w.shape[1])))


def _relayout_w_in(w_in, gate_b):
    sizes = (MLA_Q_RANK, MLA_KV_RANK, MLA_ROPE_DIM, NSA_HEADS * NSA_HEAD_DIM) + (NSA_KV_HEADS * NSA_HEAD_DIM,) * 6 \
        + (N_BRANCH * NSA_HEADS,)
    offs = [int(v) for v in np.cumsum(sizes)[:-1]]
    c_q, c_kv, k_r, q_n, k_c, v_c, k_s, v_s, k_w, v_w, g = jnp.split(w_in, offs, axis=1)
    half = MLA_ROPE_DIM // 2
    k_r_sw = jnp.concatenate([k_r[:, half:], k_r[:, :half]], axis=1)
    gate_idx = np.array([[br * NSA_HEADS + hk * NSA_GROUP + gi for br in range(N_BRANCH) for gi in range(NSA_GROUP)]
                         for hk in range(NSA_KV_HEADS)])
    g_chunks = [_pad_cols(g[:, gate_idx[hk]], LANE) for hk in range(NSA_KV_HEADS)]
    w = jnp.concatenate([c_q, c_kv, q_n, k_c, v_c, k_s, v_s, k_w, v_w,
                         _pad_cols(k_r, LANE), _pad_cols(k_r_sw, LANE)] + g_chunks, axis=1)
    assert w.shape[1] == _PROJ_COLS
    gb = jnp.stack([jnp.pad(gate_b[gate_idx[hk]], (0, LANE - gate_idx.shape[1])) for hk in range(NSA_KV_HEADS)])
    return w.astype(BF16), gb.reshape(NSA_KV_HEADS, 1, LANE)


def _relayout_w_uq(w_uq):
    dh = MLA_NOPE_DIM + MLA_ROPE_DIM
    half = MLA_ROPE_DIM // 2
    w = w_uq.reshape(MLA_Q_RANK, MLA_HEADS, dh)
    nope = w[:, :, :MLA_NOPE_DIM]
    r = w[:, :, MLA_NOPE_DIM:]
    r_sw = jnp.concatenate([r[:, :, half:], r[:, :, :half]], axis=2)
    pad = lambda t: jnp.pad(t, ((0, 0), (0, 0), (0, LANE - MLA_ROPE_DIM)))
    parts = [nope, pad(r), pad(r_sw)]
    return jnp.concatenate([p.reshape(MLA_Q_RANK, MLA_HEADS * LANE) for p in parts], axis=1).astype(BF16)


def _relayout_w_ukv(w_ukv):
    w = w_ukv.reshape(MLA_KV_RANK, MLA_HEADS, MLA_NOPE_DIM + MLA_V_DIM)
    k = w[:, :, :MLA_NOPE_DIM].reshape(MLA_KV_RANK, MLA_HEADS * MLA_NOPE_DIM)
    v = w[:, :, MLA_NOPE_DIM:].reshape(MLA_KV_RANK, MLA_HEADS * MLA_V_DIM)
    return jnp.concatenate([k, v], axis=1).astype(BF16)


def _rope_tables(s):
    pos = jnp.arange(s).astype(F32)[:, None]

    def cs(d):
        half = d // 2
        inv = ROPE_THETA ** (-jnp.arange(half, dtype=F32) * (2.0 / d))
        ang = pos * inv[None, :]
        return jnp.cos(ang), jnp.sin(ang)

    cm, sm = cs(MLA_ROPE_DIM)
    cn, sn = cs(NSA_HEAD_DIM)
    cosm = _pad_cols(jnp.concatenate([cm, cm], axis=1), LANE)
    sinm = _pad_cols(jnp.concatenate([-sm, sm], axis=1), LANE)
    cosn = jnp.concatenate([cn, cn], axis=1)
    sinn = jnp.concatenate([-sn, sn], axis=1)
    return cosm, sinm, cosn, sinn


def _overlap_weights():
    ci = np.arange(N_CMP_PAD)[:, None] * CMP_STRIDE
    sj = np.arange(N_SEL)[None, :] * SEL_BLOCK
    ov = np.clip(np.minimum(ci + CMP_BLOCK, sj + SEL_BLOCK) - np.maximum(ci, sj), 0, None)
    ov[N_CMP:] = 0
    return jnp.asarray((ov.astype(np.float32) / CMP_STRIDE).T, dtype=BF16)


def kernel(x, ffn1_w_gate, ffn1_w_up, ffn1_w_down, ln1_g, ln1_b, w_in, mla_q_norm_g, mla_w_uq, mla_kv_norm_g,
           mla_w_ukv, nsa_gate_b, nsa_cmp_pe_k, nsa_cmp_w1_k, nsa_cmp_w2_k, nsa_cmp_pe_v, nsa_cmp_w1_v,
           nsa_cmp_w2_v, w_out, ln2_g, ln2_b, ffn2_w_gate, ffn2_w_up, ffn2_w_down, ln3_g, ln3_b):
    b, s, d = x.shape
    assert (b, s, d) == (1, SEQ, D_MODEL) and ffn1_w_gate.shape[0] == DEPTH
    l = 0
    row = lambda v: v.reshape(1, -1)
    xs = x.reshape(s, d)

    x1 = _ffn_ln(xs, ffn1_w_gate[l].astype(BF16), ffn1_w_up[l].astype(BF16), ffn1_w_down[l].astype(BF16),
                 row(ln1_g[l]), row(ln1_b[l]))

    win_p, gb_p = _relayout_w_in(w_in[l], nsa_gate_b[l])
    cosm, sinm, cosn, sinn = _rope_tables(s)
    (q_m, k_m, v_m, q_n, kc_in, vc_in, ks, vs, kw, vw, gates) = _proj(
        x1, win_p, row(mla_q_norm_g[l]), row(mla_kv_norm_g[l]),
        _relayout_w_uq(mla_w_uq[l]), _relayout_w_ukv(mla_w_ukv[l]), gb_p, cosm, sinm, cosn, sinn)

    def compress(t, pe, w1, w2, transpose_out):
        t16 = t.reshape(NSA_KV_HEADS, N_CMP_PAD, _CMP_HALF)
        pe8 = jnp.broadcast_to(pe.reshape(1, CMP_BLOCK * NSA_HEAD_DIM), (8, CMP_BLOCK * NSA_HEAD_DIM))
        return _compress(t16, pe8, w1, w2, transpose_out)

    kc = compress(kc_in, nsa_cmp_pe_k[l], nsa_cmp_w1_k[l], nsa_cmp_w2_k[l], False)
    vc = compress(vc_in, nsa_cmp_pe_v[l], nsa_cmp_w1_v[l], nsa_cmp_w2_v[l], True)

    o_mla = _mla(q_m, k_m, v_m)
    o_nsa = _nsa(q_n, kc, vc, _overlap_weights(), ks, vs, kw, vw, gates)

    n_mla = MLA_HEADS * MLA_V_DIM
    w_o = w_out[l].astype(BF16)
    x2 = _out_ln(o_mla, o_nsa, w_o[:n_mla], w_o[n_mla:], x1, row(ln2_g[l]), row(ln2_b[l]))

    x3 = _ffn_ln(x2, ffn2_w_gate[l].astype(BF16), ffn2_w_up[l].astype(BF16), ffn2_w_down[l].astype(BF16),
                 row(ln3_g[l]), row(ln3_b[l]))
    return x3.reshape(b, s, d)
```

```python
import functools

import jax
import jax.numpy as jnp
import numpy as np
from jax import lax
from jax.experimental import pallas as pl
from jax.experimental.pallas import tpu as pltpu

F32 = jnp.float32
BF16 = jnp.bfloat16

D_MODEL = 2048
SEQ = 16384
DEPTH = 1
DEEPNORM_ALPHA = (2.0 * DEPTH) ** 0.25
LN_EPS = 1e-5
RMS_EPS = 1e-6
ROPE_THETA = 10000.0
NEG_INF = -1e30
D_FF = 5632
FFN_RES_WEIGHT = 0.5

MLA_HEADS = 8
MLA_Q_RANK = 512
MLA_KV_RANK = 256
MLA_NOPE_DIM = 128
MLA_ROPE_DIM = 64
MLA_V_DIM = 128
MLA_QK_PAD = 256

NSA_HEADS = 8
NSA_KV_HEADS = 2
NSA_HEAD_DIM = 128
NSA_GROUP = NSA_HEADS // NSA_KV_HEADS
CMP_BLOCK = 32
CMP_STRIDE = 16
CMP_HIDDEN = 256
SEL_BLOCK = 64
SEL_TOPK = 16
WINDOW = 512
N_BRANCH = 3
FORCE_BONUS = 1e4
N_CMP = (SEQ - CMP_BLOCK) // CMP_STRIDE + 1
N_CMP_PAD = SEQ // CMP_STRIDE
N_SEL = SEQ // SEL_BLOCK

LANE = 128
MASK_BIAS = NEG_INF
TOPK_TAKEN = -2.0 ** 127
M_INIT = -1e30

VMEM_LIMIT = 56 * 1024 * 1024

NT_DIMS = (((1,), (1,)), ((), ()))


def _dot(a, b, **kw):
    return jnp.dot(a, b, preferred_element_type=F32, **kw)


def _dot_nt(a, b):
    return lax.dot_general(a, b, NT_DIMS, preferred_element_type=F32)


def _layer_norm(y, g, b):
    mu = jnp.mean(y, axis=-1, keepdims=True)
    yc = y - mu
    var = jnp.mean(yc * yc, axis=-1, keepdims=True)
    return yc * lax.rsqrt(var + LN_EPS) * g + b


def _rms_norm(x, g):
    ms = jnp.mean(x * x, axis=-1, keepdims=True)
    return x * lax.rsqrt(ms + RMS_EPS) * g


FFN_TM = 512
FFN_TF = 512


def _ffn_ln_kernel(x_ref, wg_ref, wu_ref, wd_ref, g_ref, b_ref, o_ref, xb_ref, acc_ref):
    k = pl.program_id(1)

    @pl.when(k == 0)
    def _():
        xb_ref[...] = x_ref[...].astype(BF16)
        acc_ref[...] = jnp.zeros_like(acc_ref)

    xb = xb_ref[...]
    gate = _dot(xb, wg_ref[...])
    up = _dot(xb, wu_ref[...])
    h = (jax.nn.silu(gate) * up).astype(BF16)
    acc_ref[...] += _dot(h, wd_ref[...])

    @pl.when(k == pl.num_programs(1) - 1)
    def _():
        y = DEEPNORM_ALPHA * x_ref[...] + FFN_RES_WEIGHT * acc_ref[...]
        o_ref[...] = _layer_norm(y, g_ref[...], b_ref[...])


def _ffn_ln(x, wg, wu, wd, g, b):
    s, d = x.shape
    f = wg.shape[1]
    grid = (s // FFN_TM, f // FFN_TF)
    return pl.pallas_call(
        _ffn_ln_kernel,
        grid=grid,
        in_specs=[
            pl.BlockSpec((FFN_TM, d), lambda i, k: (i, 0)),
            pl.BlockSpec((d, FFN_TF), lambda i, k: (0, k)),
            pl.BlockSpec((d, FFN_TF), lambda i, k: (0, k)),
            pl.BlockSpec((FFN_TF, d), lambda i, k: (k, 0)),
            pl.BlockSpec((1, d), lambda i, k: (0, 0)),
            pl.BlockSpec((1, d), lambda i, k: (0, 0)),
        ],
        out_specs=pl.BlockSpec((FFN_TM, d), lambda i, k: (i, 0)),
        out_shape=jax.ShapeDtypeStruct((s, d), F32),
        scratch_shapes=[pltpu.VMEM((FFN_TM, d), BF16), pltpu.VMEM((FFN_TM, d), F32)],
        compiler_params=pltpu.CompilerParams(
            dimension_semantics=("parallel", "arbitrary"), vmem_limit_bytes=VMEM_LIMIT),
        name="ffn_ln",
    )(x, wg, wu, wd, g, b)


PROJ_TM = 256
_O_CQ = 0
_O_CKV = _O_CQ + MLA_Q_RANK
_O_NQ = _O_CKV + MLA_KV_RANK
_O_KC = _O_NQ + NSA_HEADS * NSA_HEAD_DIM
_O_VC = _O_KC + NSA_KV_HEADS * NSA_HEAD_DIM
_O_KS = _O_VC + NSA_KV_HEADS * NSA_HEAD_DIM
_O_VS = _O_KS + NSA_KV_HEADS * NSA_HEAD_DIM
_O_KW = _O_VS + NSA_KV_HEADS * NSA_HEAD_DIM
_O_VW = _O_KW + NSA_KV_HEADS * NSA_HEAD_DIM
_O_KR = _O_VW + NSA_KV_HEADS * NSA_HEAD_DIM
_O_KRS = _O_KR + LANE
_O_G = _O_KRS + LANE
_PROJ_COLS = _O_G + NSA_KV_HEADS * LANE
GATE_ROWS = 16

LOG2E = 1.4426950408889634
MLA_SCALE = (MLA_NOPE_DIM + MLA_ROPE_DIM) ** -0.5 * LOG2E
NSA_SCALE = NSA_HEAD_DIM ** -0.5 * LOG2E


def _proj_kernel(x_ref, win_ref, gq_ref, gkv_ref, wuq_ref, wukv_ref, gb_ref,
                 cosm_ref, sinm_ref, cosn_ref, sinn_ref,
                 qm_ref, km_ref, vm_ref, qn_ref, kc_ref, vc_ref, ks_ref, vs_ref, kw_ref, vw_ref, gt_ref):
    tm = x_ref.shape[0]
    xb = x_ref[...].astype(BF16)
    h = _dot(xb, win_ref[...])
    cosm, sinm = cosm_ref[...], sinm_ref[...]
    cosn, sinn = cosn_ref[...], sinn_ref[...]

    cqn = _rms_norm(h[:, _O_CQ:_O_CQ + MLA_Q_RANK], gq_ref[...]).astype(BF16)
    qall = _dot(cqn, wuq_ref[...])
    hw = MLA_HEADS * LANE
    for hd in range(MLA_HEADS):
        sl = slice(hd * LANE, (hd + 1) * LANE)
        qm_ref[hd, :, 0:LANE] = (qall[:, sl] * MLA_SCALE).astype(BF16)
        qr = qall[:, hw + hd * LANE: hw + (hd + 1) * LANE]
        qrs = qall[:, 2 * hw + hd * LANE: 2 * hw + (hd + 1) * LANE]
        qm_ref[hd, :, LANE:2 * LANE] = ((qr * cosm + qrs * sinm) * MLA_SCALE).astype(BF16)

    ckvn = _rms_norm(h[:, _O_CKV:_O_CKV + MLA_KV_RANK], gkv_ref[...]).astype(BF16)
    kv = _dot(ckvn, wukv_ref[...])
    kr = (h[:, _O_KR:_O_KR + LANE] * cosm + h[:, _O_KRS:_O_KRS + LANE] * sinm).astype(BF16)
    for hd in range(MLA_HEADS):
        km_ref[hd, :, 0:LANE] = kv[:, hd * LANE:(hd + 1) * LANE].astype(BF16)
        km_ref[hd, :, LANE:2 * LANE] = kr
        vm_ref[hd] = kv[:, hw + hd * LANE: hw + (hd + 1) * LANE].T.astype(BF16)

    def rope128(t):
        return t * cosn + pltpu.roll(t, NSA_HEAD_DIM // 2, 1) * sinn

    for hd in range(NSA_HEADS):
        t = h[:, _O_NQ + hd * LANE:_O_NQ + (hd + 1) * LANE]
        qn_ref[hd] = (rope128(t) * NSA_SCALE).astype(BF16)

    row0 = pl.program_id(0) * tm
    kpos = row0 + lax.broadcasted_iota(jnp.int32, (tm, LANE), 0)
    lane = lax.broadcasted_iota(jnp.int32, (tm, LANE), 1)
    onehot = jnp.where(((kpos // SEL_BLOCK) % LANE) == lane, 1.0, 0.0).astype(BF16)
    for hk in range(NSA_KV_HEADS):
        sl = lambda off: slice(off + hk * LANE, off + (hk + 1) * LANE)
        kc_ref[hk] = rope128(h[:, sl(_O_KC)])
        vc_ref[hk] = h[:, sl(_O_VC)]
        ks_ref[hk, :, 0:LANE] = rope128(h[:, sl(_O_KS)]).astype(BF16)
        ks_ref[hk, :, LANE:2 * LANE] = onehot
        vs_ref[hk] = h[:, sl(_O_VS)].T.astype(BF16)
        kw_ref[hk] = rope128(h[:, sl(_O_KW)]).astype(BF16)
        vw_ref[hk] = h[:, sl(_O_VW)].T.astype(BF16)
        gt_ref[hk] = jax.nn.sigmoid(h[:, sl(_O_G)] + gb_ref[hk]).T[0:GATE_ROWS]


def _const_spec(shape):
    nd = len(shape)
    return pl.BlockSpec(shape, lambda i: (0,) * nd)


def _proj(x1, win_p, gq, gkv, wuq_all, wukv_p, gb_p, cosm, sinm, cosn, sinn):
    s, d = x1.shape
    tm = PROJ_TM
    grid = (s // tm,)
    row_spec = lambda w: pl.BlockSpec((tm, w), lambda i: (i, 0))
    head_spec = lambda nh, w: pl.BlockSpec((nh, tm, w), lambda i: (0, i, 0))
    out_shape = (
        jax.ShapeDtypeStruct((MLA_HEADS, s, MLA_QK_PAD), BF16),
        jax.ShapeDtypeStruct((MLA_HEADS, s, MLA_QK_PAD), BF16),
        jax.ShapeDtypeStruct((MLA_HEADS, MLA_V_DIM, s), BF16),
        jax.ShapeDtypeStruct((NSA_HEADS, s, NSA_HEAD_DIM), BF16),
        jax.ShapeDtypeStruct((NSA_KV_HEADS, s, NSA_HEAD_DIM), F32),
        jax.ShapeDtypeStruct((NSA_KV_HEADS, s, NSA_HEAD_DIM), F32),
        jax.ShapeDtypeStruct((NSA_KV_HEADS, s, 2 * LANE), BF16),
        jax.ShapeDtypeStruct((NSA_KV_HEADS, NSA_HEAD_DIM, s), BF16),
        jax.ShapeDtypeStruct((NSA_KV_HEADS, s, NSA_HEAD_DIM), BF16),
        jax.ShapeDtypeStruct((NSA_KV_HEADS, NSA_HEAD_DIM, s), BF16),
        jax.ShapeDtypeStruct((NSA_KV_HEADS, GATE_ROWS, s), F32),
    )
    head_spec_t = lambda nh, r: pl.BlockSpec((nh, r, tm), lambda i: (0, 0, i))
    out_specs = (
        head_spec(MLA_HEADS, MLA_QK_PAD), head_spec(MLA_HEADS, MLA_QK_PAD), head_spec_t(MLA_HEADS, MLA_V_DIM),
        head_spec(NSA_HEADS, NSA_HEAD_DIM),
        head_spec(NSA_KV_HEADS, NSA_HEAD_DIM), head_spec(NSA_KV_HEADS, NSA_HEAD_DIM),
        head_spec(NSA_KV_HEADS, 2 * LANE), head_spec_t(NSA_KV_HEADS, NSA_HEAD_DIM),
        head_spec(NSA_KV_HEADS, NSA_HEAD_DIM), head_spec_t(NSA_KV_HEADS, NSA_HEAD_DIM),
        head_spec_t(NSA_KV_HEADS, GATE_ROWS),
    )
    return pl.pallas_call(
        _proj_kernel,
        grid=grid,
        in_specs=[
            row_spec(d),
            _const_spec(win_p.shape), _const_spec(gq.shape), _const_spec(gkv.shape),
            _const_spec(wuq_all.shape), _const_spec(wukv_p.shape), _const_spec(gb_p.shape),
            row_spec(LANE), row_spec(LANE), row_spec(LANE), row_spec(LANE),
        ],
        out_specs=out_specs,
        out_shape=out_shape,
        compiler_params=pltpu.CompilerParams(
            dimension_semantics=("parallel",), vmem_limit_bytes=VMEM_LIMIT),
        name="proj",
    )(x1, win_p, gq, gkv, wuq_all, wukv_p, gb_p, cosm, sinm, cosn, sinn)


_CMP_HALF = CMP_STRIDE * NSA_HEAD_DIM


def _compress_kernel(t_ref, pe_ref, w1_ref, w2_ref, o_ref, *, transpose_out):
    hi = lax.Precision.HIGHEST
    t = t_ref[0]
    w1a = w1_ref[0:_CMP_HALF, :]
    w1b = w1_ref[_CMP_HALF:2 * _CMP_HALF, :]
    first = _dot(t, w1a, precision=hi)
    second = _dot(t, w1b, precision=hi)
    pe_row = _dot(pe_ref[...], w1_ref[...], precision=hi)[0:1]
    hid = first + pltpu.roll(second, N_CMP_PAD - 1, 0) + pe_row
    out = _dot(jax.nn.gelu(hid), w2_ref[...], precision=hi)
    o_ref[0] = (out.T if transpose_out else out).astype(o_ref.dtype)


def _compress(t16, pe_flat8, w1, w2, transpose_out):
    out_dims = (NSA_HEAD_DIM, N_CMP_PAD) if transpose_out else (N_CMP_PAD, NSA_HEAD_DIM)
    return pl.pallas_call(
        functools.partial(_compress_kernel, transpose_out=transpose_out),
        grid=(NSA_KV_HEADS,),
        in_specs=[
            pl.BlockSpec((1, N_CMP_PAD, _CMP_HALF), lambda i: (i, 0, 0)),
            _const_spec(pe_flat8.shape), _const_spec(w1.shape), _const_spec(w2.shape),
        ],
        out_specs=pl.BlockSpec((1,) + out_dims, lambda i: (i, 0, 0)),
        out_shape=jax.ShapeDtypeStruct((NSA_KV_HEADS,) + out_dims, BF16),
        compiler_params=pltpu.CompilerParams(
            dimension_semantics=("parallel",), vmem_limit_bytes=VMEM_LIMIT),
        name="compress",
    )(t16, pe_flat8, w1, w2)


def _flash_init(m_ref, l_ref, acc_ref):
    m_ref[...] = jnp.full_like(m_ref, M_INIT)
    l_ref[...] = jnp.zeros_like(l_ref)
    acc_ref[...] = jnp.zeros_like(acc_ref)


def _flash_update(st, vt, m_ref, l_ref, acc_ref):
    m_prev = m_ref[...]
    m_new = jnp.maximum(m_prev, jnp.max(st, axis=0, keepdims=True))
    alpha = jnp.exp2(m_prev - m_new)
    p = jnp.exp2(st - m_new)
    l_ref[...] = alpha * l_ref[...] + jnp.sum(p, axis=0, keepdims=True)
    acc_ref[...] = alpha * acc_ref[...] + _dot(vt, p.astype(BF16))
    m_ref[...] = m_new


FLASH_PAIRS_PER_TRIP = 2


def _flash_causal(scores, values, causal_mask, n_plain, last_tile, two_diagonal_tiles,
                  sa_ref, sb_ref, m_ref, l_ref, acc_ref):
    _flash_init(m_ref, l_ref, acc_ref)
    sa_ref[...] = scores(0)

    def plain_pair(t):
        sb_ref[...] = scores(t + 1)
        _flash_update(sa_ref[...], values(t), m_ref, l_ref, acc_ref)
        sa_ref[...] = scores(t + 2)
        _flash_update(sb_ref[...], values(t + 1), m_ref, l_ref, acc_ref)

    def body(j, carry):
        for u in range(FLASH_PAIRS_PER_TRIP):
            plain_pair(2 * (FLASH_PAIRS_PER_TRIP * j + u))
        return carry

    n_pairs = n_plain // 2
    n_trips = n_pairs // FLASH_PAIRS_PER_TRIP
    lax.fori_loop(0, n_trips, body, 0)
    for u in range(FLASH_PAIRS_PER_TRIP - 1):
        @pl.when(n_trips * FLASH_PAIRS_PER_TRIP + u < n_pairs)
        def _():
            plain_pair(2 * (n_trips * FLASH_PAIRS_PER_TRIP + u))
    t = 2 * n_pairs

    def masked_update(tile, st):
        _flash_update(jnp.where(causal_mask(tile), st, MASK_BIAS), values(tile), m_ref, l_ref, acc_ref)

    if two_diagonal_tiles:
        sb_ref[...] = scores(t + 1)
        masked_update(t, sa_ref[...])
        masked_update(t + 1, sb_ref[...])
    else:
        masked_update(t, sa_ref[...])

        @pl.when(t + 1 <= last_tile)
        def _():
            masked_update(t + 1, scores(t + 1))


def _masked_probs_t(st, mask):
    st = jnp.where(mask, st, NEG_INF)
    m = jnp.max(st, axis=0, keepdims=True)
    p = jnp.where(mask, jnp.exp2(st - m), 0.0)
    return p * (1.0 / jnp.maximum(jnp.sum(p, axis=0, keepdims=True), 1e-30))


MLA_TQ = 1024
MLA_TK = 512


def _mla_kernel(q_ref, k_ref, vt_ref, o_ref, sa_ref, sb_ref, m_ref, l_ref, acc_ref):
    i = pl.program_id(1)
    q = q_ref[0]

    def scores(kt):
        k0 = pl.multiple_of(kt * MLA_TK, MLA_TK)
        return _dot_nt(k_ref[0, pl.ds(k0, MLA_TK), :], q)

    def values(kt):
        return vt_ref[0, :, pl.ds(pl.multiple_of(kt * MLA_TK, MLA_TK), MLA_TK)]

    qpos = i * MLA_TQ + lax.broadcasted_iota(jnp.int32, (1, MLA_TQ), 1)

    def causal_mask(kt):
        return kt * MLA_TK + lax.broadcasted_iota(jnp.int32, (MLA_TK, 1), 0) <= qpos

    tiles_per_q = MLA_TQ // MLA_TK
    _flash_causal(scores, values, causal_mask, i * tiles_per_q, (i + 1) * tiles_per_q - 1, tiles_per_q == 2,
                  sa_ref, sb_ref, m_ref, l_ref, acc_ref)
    o_ref[...] = (acc_ref[...] * (1.0 / l_ref[...])).T.astype(o_ref.dtype)


def _mla(q, k, vt):
    nh, s, dq = q.shape
    dv = vt.shape[1]
    assert MLA_TQ in (MLA_TK, 2 * MLA_TK) and s % MLA_TQ == 0
    return pl.pallas_call(
        _mla_kernel,
        grid=(nh, s // MLA_TQ),
        in_specs=[
            pl.BlockSpec((1, MLA_TQ, dq), lambda h, i: (h, i, 0)),
            pl.BlockSpec((1, s, dq), lambda h, i: (h, 0, 0)),
            pl.BlockSpec((1, dv, s), lambda h, i: (h, 0, 0)),
        ],
        out_specs=pl.BlockSpec((MLA_TQ, dv), lambda h, i: (i, h)),
        out_shape=jax.ShapeDtypeStruct((s, nh * dv), BF16),
        scratch_shapes=[pltpu.VMEM((MLA_TK, MLA_TQ), F32), pltpu.VMEM((MLA_TK, MLA_TQ), F32),
                        pltpu.VMEM((1, MLA_TQ), F32), pltpu.VMEM((1, MLA_TQ), F32),
                        pltpu.VMEM((dv, MLA_TQ), F32)],
        compiler_params=pltpu.CompilerParams(
            dimension_semantics=("parallel", "arbitrary"), vmem_limit_bytes=VMEM_LIMIT),
        name="mla",
    )(q, k, vt)


NSA_TQ = 256
NSA_ROWS = NSA_GROUP * NSA_TQ
NSA_TK = 512
_SEL_HALF_KEYS = LANE * SEL_BLOCK
_WIN_TILES = WINDOW // NSA_TQ + 1
SEL_CLASS_BLOCKS = 64


def _nsa_kernel(q_ref, kc_ref, vct_ref, ovt_ref, ks_ref, vst_ref, *rest):
    kw_refs = rest[:_WIN_TILES]
    vwt_refs = rest[_WIN_TILES:2 * _WIN_TILES]
    (wbias_ref, qoh_ref, gt_ref, o_ref,
     qaug_ref, oc_ref, sa_ref, sb_ref, m_ref, l_ref, acc_ref) = rest[2 * _WIN_TILES:]

    i = pl.program_id(1)
    q0 = i * NSA_TQ
    q = q_ref[...].reshape(NSA_ROWS, NSA_HEAD_DIM)
    tq = q0 + (lax.broadcasted_iota(jnp.int32, (1, NSA_ROWS), 1) & (NSA_TQ - 1))
    tq1 = q0 + lax.broadcasted_iota(jnp.int32, (1, NSA_TQ), 1)

    q_oh = jnp.concatenate([q, qoh_ref[...]], axis=1)

    kw = jnp.concatenate([r[0] for r in kw_refs], axis=0)
    vwt = jnp.concatenate([r[0] for r in vwt_refs], axis=1)
    s_w = _dot_nt(jnp.concatenate([kw, wbias_ref[0]], axis=1), q_oh)
    e_w = jnp.exp2(s_w - jnp.max(s_w, axis=0, keepdims=True))
    o_w = _dot(vwt, e_w.astype(BF16)) * (1.0 / jnp.sum(e_w, axis=0, keepdims=True))

    def compress_and_select(nb):
        nc = nb * (SEL_BLOCK // CMP_STRIDE)
        cend = lax.broadcasted_iota(jnp.int32, (nc, NSA_TQ), 0) * CMP_STRIDE + (CMP_BLOCK - 1)
        cbias = jnp.where(cend <= tq1, 0.0, NEG_INF).astype(BF16)
        s_c = _dot_nt(jnp.concatenate([kc_ref[0, 0:nc, :], cbias], axis=1), q_oh)
        e_c = jnp.exp2(s_c - jnp.max(s_c, axis=0, keepdims=True))
        inv_c = jnp.where(tq >= CMP_BLOCK - 1, 1.0 / jnp.sum(e_c, axis=0, keepdims=True), 0.0)
        p_c = e_c * inv_c
        oc_ref[...] = _dot(vct_ref[0, :, 0:nc], p_c.astype(BF16))

        psum = p_c[:, 0:NSA_TQ]
        for g in range(1, NSA_GROUP):
            psum = psum + p_c[:, g * NSA_TQ:(g + 1) * NSA_TQ]
        p_hi = psum.astype(BF16)
        p_lo = (psum - p_hi.astype(F32)).astype(BF16)
        ovt = ovt_ref[0:nb, 0:nc]
        imp = _dot(ovt, p_hi) + _dot(ovt, p_lo)

        bj = lax.broadcasted_iota(jnp.int32, (nb, NSA_TQ), 0)
        cur = tq1 // SEL_BLOCK
        valid = bj * SEL_BLOCK <= tq1
        forced = (bj == 0) | (bj == cur) | (bj == cur - 1)
        score = jnp.where(valid, imp + FORCE_BONUS * jnp.where(forced, 1.0, 0.0), NEG_INF)
        bjf = bj.astype(F32)
        work = score
        for _ in range(SEL_TOPK):
            mx = jnp.max(work, axis=0, keepdims=True)
            first = jnp.min(jnp.where(work == mx, bjf, float(N_SEL)), axis=0, keepdims=True)
            work = jnp.where(bjf == first, TOPK_TAKEN, work)
        selected = (work < 0.5 * TOPK_TAKEN) & (score > 0.5 * NEG_INF)
        bias = jnp.where(selected, 0.0, MASK_BIAS)
        if nb < N_SEL:
            bias = jnp.concatenate([bias, jnp.full((N_SEL - nb, NSA_TQ), MASK_BIAS, F32)], axis=0)
        bias = jnp.concatenate([bias.T.astype(BF16)] * NSA_GROUP, axis=0)
        for half in range(N_SEL // LANE):
            qaug_ref[half, :, 0:NSA_HEAD_DIM] = q
            qaug_ref[half, :, NSA_HEAD_DIM:NSA_HEAD_DIM + LANE] = bias[:, half * LANE:(half + 1) * LANE]

    size_class = (q0 + NSA_TQ - 1) // (SEL_BLOCK * SEL_CLASS_BLOCKS)
    for c in range(N_SEL // SEL_CLASS_BLOCKS):
        pl.when(size_class == c)(functools.partial(compress_and_select, (c + 1) * SEL_CLASS_BLOCKS))
    o_c = oc_ref[...]

    last = (q0 + NSA_TQ - 1) // NSA_TK

    def scores(kt):
        k0 = pl.multiple_of(kt * NSA_TK, NSA_TK)
        qa = qaug_ref[kt // (_SEL_HALF_KEYS // NSA_TK)]
        return _dot_nt(ks_ref[0, pl.ds(k0, NSA_TK), :], qa)

    def values(kt):
        return vst_ref[0, :, pl.ds(pl.multiple_of(kt * NSA_TK, NSA_TK), NSA_TK)]

    def causal_mask(kt):
        return kt * NSA_TK + lax.broadcasted_iota(jnp.int32, (NSA_TK, 1), 0) <= tq

    _flash_causal(scores, values, causal_mask, last, last, False, sa_ref, sb_ref, m_ref, l_ref, acc_ref)
    o_s = acc_ref[...] * (1.0 / l_ref[...])

    gt = gt_ref[0]
    gate = lambda br: jnp.concatenate(
        [gt[br * NSA_GROUP + g:br * NSA_GROUP + g + 1, :] for g in range(NSA_GROUP)], axis=1)
    out = (gate(0) * o_c + gate(1) * o_s + gate(2) * o_w).T
    for g in range(NSA_GROUP):
        o_ref[:, g * NSA_HEAD_DIM:(g + 1) * NSA_HEAD_DIM] = out[g * NSA_TQ:(g + 1) * NSA_TQ].astype(o_ref.dtype)


def _nsa(q, kc, vct, ovt, ks, vst, kw, vwt, gates):
    s = q.shape[1]
    d = NSA_HEAD_DIM
    assert NSA_TQ <= NSA_TK and NSA_TK % NSA_TQ == 0

    def win_specs(transposed):
        def spec(j):
            back = _WIN_TILES - 1 - j
            if transposed:
                return pl.BlockSpec((1, d, NSA_TQ), lambda h, i: (h, 0, jnp.maximum(i - back, 0)))
            return pl.BlockSpec((1, NSA_TQ, d), lambda h, i: (h, jnp.maximum(i - back, 0), 0))
        return [spec(j) for j in range(_WIN_TILES)]

    return pl.pallas_call(
        _nsa_kernel,
        grid=(NSA_KV_HEADS, s // NSA_TQ),
        in_specs=[
            pl.BlockSpec((NSA_GROUP, NSA_TQ, d), lambda h, i: (h, i, 0)),
            pl.BlockSpec((1, N_CMP_PAD, d), lambda h, i: (h, 0, 0)),
            pl.BlockSpec((1, d, N_CMP_PAD), lambda h, i: (h, 0, 0)),
            pl.BlockSpec((N_SEL, N_CMP_PAD), lambda h, i: (0, 0)),
            pl.BlockSpec((1, s, 2 * LANE), lambda h, i: (h, 0, 0)),
            pl.BlockSpec((1, d, s), lambda h, i: (h, 0, 0)),
            *win_specs(False), *win_specs(True),
            pl.BlockSpec((1, WINDOW + NSA_TQ, NSA_TQ), lambda h, i: (jnp.minimum(i, _WIN_TILES - 1), 0, 0)),
            pl.BlockSpec((NSA_ROWS, NSA_TQ), lambda h, i: (0, 0)),
            pl.BlockSpec((1, GATE_ROWS, NSA_TQ), lambda h, i: (h, 0, i)),
        ],
        out_specs=pl.BlockSpec((NSA_TQ, NSA_GROUP * d), lambda h, i: (i, h)),
        out_shape=jax.ShapeDtypeStruct((s, NSA_HEADS * d), BF16),
        scratch_shapes=[pltpu.VMEM((N_SEL // LANE, NSA_ROWS, NSA_HEAD_DIM + LANE), BF16),
                        pltpu.VMEM((d, NSA_ROWS), F32),
                        pltpu.VMEM((NSA_TK, NSA_ROWS), F32), pltpu.VMEM((NSA_TK, NSA_ROWS), F32),
                        pltpu.VMEM((1, NSA_ROWS), F32), pltpu.VMEM((1, NSA_ROWS), F32),
                        pltpu.VMEM((d, NSA_ROWS), F32)],
        compiler_params=pltpu.CompilerParams(
            dimension_semantics=("parallel", "arbitrary"), vmem_limit_bytes=VMEM_LIMIT),
        name="nsa",
    )(q, kc, vct, ovt, ks, vst, *([kw] * _WIN_TILES), *([vwt] * _WIN_TILES), _window_bias(), _query_onehot(), gates)


def _window_bias():
    r = np.arange(WINDOW + NSA_TQ)[None, :, None]
    c = np.arange(NSA_TQ)[None, None, :]
    q0 = (np.arange(_WIN_TILES) * NSA_TQ)[:, None, None]
    back = WINDOW + c - r
    ok = (back >= 0) & (back < WINDOW) & ((q0 - WINDOW + r >= 0) | (q0 >= WINDOW))
    return jnp.asarray(np.where(ok, 0.0, NEG_INF), dtype=BF16)


def _query_onehot():
    return jnp.asarray(np.tile(np.eye(NSA_TQ, dtype=np.float32), (NSA_GROUP, 1)), dtype=BF16)


OUT_TM = 512


def _out_ln_kernel(om_ref, on_ref, wa_ref, wb_ref, x_ref, g_ref, b_ref, o_ref):
    mix = _dot(om_ref[...], wa_ref[...]) + _dot(on_ref[...], wb_ref[...])
    o_ref[...] = _layer_norm(DEEPNORM_ALPHA * x_ref[...] + mix, g_ref[...], b_ref[...])


def _out_ln(o_mla, o_nsa, wa, wb, x1, g, b):
    s, d = x1.shape
    tm = OUT_TM
    return pl.pallas_call(
        _out_ln_kernel,
        grid=(s // tm,),
        in_specs=[
            pl.BlockSpec((tm, o_mla.shape[1]), lambda i: (i, 0)),
            pl.BlockSpec((tm, o_nsa.shape[1]), lambda i: (i, 0)),
            _const_spec(wa.shape), _const_spec(wb.shape),
            pl.BlockSpec((tm, d), lambda i: (i, 0)),
            _const_spec(g.shape), _const_spec(b.shape),
        ],
        out_specs=pl.BlockSpec((tm, d), lambda i: (i, 0)),
        out_shape=jax.ShapeDtypeStruct((s, d), F32),
        compiler_params=pltpu.CompilerParams(
            dimension_semantics=("parallel",), vmem_limit_bytes=VMEM_LIMIT),
        name="out_ln",
    )(o_mla, o_nsa, wa, wb, x1, g, b)


def _pad_cols(w, n):
    return jnp.pad(w, ((0, 0), (0, n - w.shape[1])))


def _relayout_w_in(w_in, gate_b):
    sizes = (MLA_Q_RANK, MLA_KV_RANK, MLA_ROPE_DIM, NSA_HEADS * NSA_HEAD_DIM) + (NSA_KV_HEADS * NSA_HEAD_DIM,) * 6 \
        + (N_BRANCH * NSA_HEADS,)
    offs = [int(v) for v in np.cumsum(sizes)[:-1]]
    c_q, c_kv, k_r, q_n, k_c, v_c, k_s, v_s, k_w, v_w, g = jnp.split(w_in, offs, axis=1)
    half = MLA_ROPE_DIM // 2
    k_r_sw = jnp.concatenate([k_r[:, half:], k_r[:, :half]], axis=1)
    gate_idx = np.array([[br * NSA_HEADS + hk * NSA_GROUP + gi for br in range(N_BRANCH) for gi in range(NSA_GROUP)]
                         for hk in range(NSA_KV_HEADS)])
    g_chunks = [_pad_cols(g[:, gate_idx[hk]], LANE) for hk in range(NSA_KV_HEADS)]
    w = jnp.concatenate([c_q, c_kv, q_n, k_c, v_c, k_s, v_s, k_w, v_w,
                         _pad_cols(k_r, LANE), _pad_cols(k_r_sw, LANE)] + g_chunks, axis=1)
    assert w.shape[1] == _PROJ_COLS
    gb = jnp.stack([jnp.pad(gate_b[gate_idx[hk]], (0, LANE - gate_idx.shape[1])) for hk in range(NSA_KV_HEADS)])
    return w.astype(BF16), gb.reshape(NSA_KV_HEADS, 1, LANE)


def _relayout_w_uq(w_uq):
    dh = MLA_NOPE_DIM + MLA_ROPE_DIM
    half = MLA_ROPE_DIM // 2
    w = w_uq.reshape(MLA_Q_RANK, MLA_HEADS, dh)
    nope = w[:, :, :MLA_NOPE_DIM]
    r = w[:, :, MLA_NOPE_DIM:]
    r_sw = jnp.concatenate([r[:, :, half:], r[:, :, :half]], axis=2)
    pad = lambda t: jnp.pad(t, ((0, 0), (0, 0), (0, LANE - MLA_ROPE_DIM)))
    parts = [nope, pad(r), pad(r_sw)]
    return jnp.concatenate([p.reshape(MLA_Q_RANK, MLA_HEADS * LANE) for p in parts], axis=1).astype(BF16)


def _relayout_w_ukv(w_ukv):
    w = w_ukv.reshape(MLA_KV_RANK, MLA_HEADS, MLA_NOPE_DIM + MLA_V_DIM)
    k = w[:, :, :MLA_NOPE_DIM].reshape(MLA_KV_RANK, MLA_HEADS * MLA_NOPE_DIM)
    v = w[:, :, MLA_NOPE_DIM:].reshape(MLA_KV_RANK, MLA_HEADS * MLA_V_DIM)
    return jnp.concatenate([k, v], axis=1).astype(BF16)


def _rope_tables(s):
    pos = jnp.arange(s).astype(F32)[:, None]

    def cs(d):
        half = d // 2
        inv = ROPE_THETA ** (-jnp.arange(half, dtype=F32) * (2.0 / d))
        ang = pos * inv[None, :]
        return jnp.cos(ang), jnp.sin(ang)

    cm, sm = cs(MLA_ROPE_DIM)
    cn, sn = cs(NSA_HEAD_DIM)
    cosm = _pad_cols(jnp.concatenate([cm, cm], axis=1), LANE)
    sinm = _pad_cols(jnp.concatenate([-sm, sm], axis=1), LANE)
    cosn = jnp.concatenate([cn, cn], axis=1)
    sinn = jnp.concatenate([-sn, sn], axis=1)
    return cosm, sinm, cosn, sinn


def _overlap_weights():
    ci = np.arange(N_CMP_PAD)[:, None] * CMP_STRIDE
    sj = np.arange(N_SEL)[None, :] * SEL_BLOCK
    ov = np.clip(np.minimum(ci + CMP_BLOCK, sj + SEL_BLOCK) - np.maximum(ci, sj), 0, None)
    ov[N_CMP:] = 0
    return jnp.asarray((ov.astype(np.float32) / CMP_STRIDE).T, dtype=BF16)


def kernel(x, ffn1_w_gate, ffn1_w_up, ffn1_w_down, ln1_g, ln1_b, w_in, mla_q_norm_g, mla_w_uq, mla_kv_norm_g,
           mla_w_ukv, nsa_gate_b, nsa_cmp_pe_k, nsa_cmp_w1_k, nsa_cmp_w2_k, nsa_cmp_pe_v, nsa_cmp_w1_v,
           nsa_cmp_w2_v, w_out, ln2_g, ln2_b, ffn2_w_gate, ffn2_w_up, ffn2_w_down, ln3_g, ln3_b):
    b, s, d = x.shape
    assert (b, s, d) == (1, SEQ, D_MODEL) and ffn1_w_gate.shape[0] == DEPTH
    l = 0
    row = lambda v: v.reshape(1, -1)
    xs = x.reshape(s, d)

    x1 = _ffn_ln(xs, ffn1_w_gate[l].astype(BF16), ffn1_w_up[l].astype(BF16), ffn1_w_down[l].astype(BF16),
                 row(ln1_g[l]), row(ln1_b[l]))

    win_p, gb_p = _relayout_w_in(w_in[l], nsa_gate_b[l])
    cosm, sinm, cosn, sinn = _rope_tables(s)
    (q_m, k_m, v_m, q_n, kc_in, vc_in, ks, vs, kw, vw, gates) = _proj(
        x1, win_p, row(mla_q_norm_g[l]), row(mla_kv_norm_g[l]),
        _relayout_w_uq(mla_w_uq[l]), _relayout_w_ukv(mla_w_ukv[l]), gb_p, cosm, sinm, cosn, sinn)

    def compress(t, pe, w1, w2, transpose_out):
        t16 = t.reshape(NSA_KV_HEADS, N_CMP_PAD, _CMP_HALF)
        pe8 = jnp.broadcast_to(pe.reshape(1, CMP_BLOCK * NSA_HEAD_DIM), (8, CMP_BLOCK * NSA_HEAD_DIM))
        return _compress(t16, pe8, w1, w2, transpose_out)

    kc = compress(kc_in, nsa_cmp_pe_k[l], nsa_cmp_w1_k[l], nsa_cmp_w2_k[l], False)
    vc = compress(vc_in, nsa_cmp_pe_v[l], nsa_cmp_w1_v[l], nsa_cmp_w2_v[l], True)

    o_mla = _mla(q_m, k_m, v_m)
    o_nsa = _nsa(q_n, kc, vc, _overlap_weights(), ks, vs, kw, vw, gates)

    n_mla = MLA_HEADS * MLA_V_DIM
    w_o = w_out[l].astype(BF16)
    x2 = _out_ln(o_mla, o_nsa, w_o[:n_mla], w_o[n_mla:], x1, row(ln2_g[l]), row(ln2_b[l]))

    x3 = _ffn_ln(x2, ffn2_w_gate[l].astype(BF16), ffn2_w_up[l].astype(BF16), ffn2_w_down[l].astype(BF16),
                 row(ln3_g[l]), row(ln3_b[l]))
    return x3.reshape(b, s, d)
```

```python
import functools

import jax
import jax.numpy as jnp
import numpy as np
from jax import lax
from jax.experimental import pallas as pl
from jax.experimental.pallas import tpu as pltpu

F32 = jnp.float32
BF16 = jnp.bfloat16

D_MODEL = 2048
SEQ = 16384
DEPTH = 1
DEEPNORM_ALPHA = (2.0 * DEPTH) ** 0.25
LN_EPS = 1e-5
RMS_EPS = 1e-6
ROPE_THETA = 10000.0
NEG_INF = -1e30
D_FF = 5632
FFN_RES_WEIGHT = 0.5

MLA_HEADS = 8
MLA_Q_RANK = 512
MLA_KV_RANK = 256
MLA_NOPE_DIM = 128
MLA_ROPE_DIM = 64
MLA_V_DIM = 128
MLA_QK_PAD = 256

NSA_HEADS = 8
NSA_KV_HEADS = 2
NSA_HEAD_DIM = 128
NSA_GROUP = NSA_HEADS // NSA_KV_HEADS
CMP_BLOCK = 32
CMP_STRIDE = 16
CMP_HIDDEN = 256
SEL_BLOCK = 64
SEL_TOPK = 16
WINDOW = 512
N_BRANCH = 3
FORCE_BONUS = 1e4
N_CMP = (SEQ - CMP_BLOCK) // CMP_STRIDE + 1
N_CMP_PAD = SEQ // CMP_STRIDE
N_SEL = SEQ // SEL_BLOCK

LANE = 128
MASK_BIAS = NEG_INF
TOPK_TAKEN = -2.0 ** 127
M_INIT = -1e30

VMEM_LIMIT = 56 * 1024 * 1024


def _dot(a, b, **kw):
    return jnp.dot(a, b, preferred_element_type=F32, **kw)


def _layer_norm(y, g, b):
    mu = jnp.mean(y, axis=-1, keepdims=True)
    yc = y - mu
    var = jnp.mean(yc * yc, axis=-1, keepdims=True)
    return yc * lax.rsqrt(var + LN_EPS) * g + b


def _rms_norm(x, g):
    ms = jnp.mean(x * x, axis=-1, keepdims=True)
    return x * lax.rsqrt(ms + RMS_EPS) * g


FFN_TM = 512
FFN_TF = 512


def _ffn_ln_kernel(x_ref, wg_ref, wu_ref, wd_ref, g_ref, b_ref, o_ref, xb_ref, acc_ref):
    k = pl.program_id(1)

    @pl.when(k == 0)
    def _():
        xb_ref[...] = x_ref[...].astype(BF16)
        acc_ref[...] = jnp.zeros_like(acc_ref)

    xb = xb_ref[...]
    gate = _dot(xb, wg_ref[...])
    up = _dot(xb, wu_ref[...])
    h = (jax.nn.silu(gate) * up).astype(BF16)
    acc_ref[...] += _dot(h, wd_ref[...])

    @pl.when(k == pl.num_programs(1) - 1)
    def _():
        y = DEEPNORM_ALPHA * x_ref[...] + FFN_RES_WEIGHT * acc_ref[...]
        o_ref[...] = _layer_norm(y, g_ref[...], b_ref[...])


def _ffn_ln(x, wg, wu, wd, g, b):
    s, d = x.shape
    f = wg.shape[1]
    grid = (s // FFN_TM, f // FFN_TF)
    return pl.pallas_call(
        _ffn_ln_kernel,
        grid=grid,
        in_specs=[
            pl.BlockSpec((FFN_TM, d), lambda i, k: (i, 0)),
            pl.BlockSpec((d, FFN_TF), lambda i, k: (0, k)),
            pl.BlockSpec((d, FFN_TF), lambda i, k: (0, k)),
            pl.BlockSpec((FFN_TF, d), lambda i, k: (k, 0)),
            pl.BlockSpec((1, d), lambda i, k: (0, 0)),
            pl.BlockSpec((1, d), lambda i, k: (0, 0)),
        ],
        out_specs=pl.BlockSpec((FFN_TM, d), lambda i, k: (i, 0)),
        out_shape=jax.ShapeDtypeStruct((s, d), F32),
        scratch_shapes=[pltpu.VMEM((FFN_TM, d), BF16), pltpu.VMEM((FFN_TM, d), F32)],
        compiler_params=pltpu.CompilerParams(
            dimension_semantics=("parallel", "arbitrary"), vmem_limit_bytes=VMEM_LIMIT),
        name="ffn_ln",
    )(x, wg, wu, wd, g, b)


PROJ_TM = 256
_O_CQ = 0
_O_CKV = _O_CQ + MLA_Q_RANK
_O_NQ = _O_CKV + MLA_KV_RANK
_O_KC = _O_NQ + NSA_HEADS * NSA_HEAD_DIM
_O_VC = _O_KC + NSA_KV_HEADS * NSA_HEAD_DIM
_O_KS = _O_VC + NSA_KV_HEADS * NSA_HEAD_DIM
_O_VS = _O_KS + NSA_KV_HEADS * NSA_HEAD_DIM
_O_KW = _O_VS + NSA_KV_HEADS * NSA_HEAD_DIM
_O_VW = _O_KW + NSA_KV_HEADS * NSA_HEAD_DIM
_O_KR = _O_VW + NSA_KV_HEADS * NSA_HEAD_DIM
_O_KRS = _O_KR + LANE
_O_G = _O_KRS + LANE
_PROJ_COLS = _O_G + NSA_KV_HEADS * LANE
GATE_ROWS = 16

LOG2E = 1.4426950408889634
MLA_SCALE = (MLA_NOPE_DIM + MLA_ROPE_DIM) ** -0.5 * LOG2E
NSA_SCALE = NSA_HEAD_DIM ** -0.5 * LOG2E


def _proj_kernel(x_ref, win_ref, gq_ref, gkv_ref, wuq_ref, wukv_ref, gb_ref,
                 cosm_ref, sinm_ref, cosn_ref, sinn_ref,
                 qm_ref, km_ref, vm_ref, qn_ref, kc_ref, vc_ref, ks_ref, vs_ref, kw_ref, vw_ref, gt_ref):
    tm = x_ref.shape[0]
    xb = x_ref[...].astype(BF16)
    h = _dot(xb, win_ref[...])
    cosm, sinm = cosm_ref[...], sinm_ref[...]
    cosn, sinn = cosn_ref[...], sinn_ref[...]

    cqn = _rms_norm(h[:, _O_CQ:_O_CQ + MLA_Q_RANK], gq_ref[...]).astype(BF16)
    qall = _dot(cqn, wuq_ref[...])
    hw = MLA_HEADS * LANE
    for hd in range(MLA_HEADS):
        sl = slice(hd * LANE, (hd + 1) * LANE)
        qm_ref[hd, :, 0:LANE] = (qall[:, sl] * MLA_SCALE).astype(BF16)
        qr = qall[:, hw + hd * LANE: hw + (hd + 1) * LANE]
        qrs = qall[:, 2 * hw + hd * LANE: 2 * hw + (hd + 1) * LANE]
        qm_ref[hd, :, LANE:2 * LANE] = ((qr * cosm + qrs * sinm) * MLA_SCALE).astype(BF16)

    ckvn = _rms_norm(h[:, _O_CKV:_O_CKV + MLA_KV_RANK], gkv_ref[...]).astype(BF16)
    kv = _dot(ckvn, wukv_ref[...])
    kr = (h[:, _O_KR:_O_KR + LANE] * cosm + h[:, _O_KRS:_O_KRS + LANE] * sinm).astype(BF16)
    for hd in range(MLA_HEADS):
        km_ref[hd, :, 0:LANE] = kv[:, hd * LANE:(hd + 1) * LANE].astype(BF16)
        km_ref[hd, :, LANE:2 * LANE] = kr
        vm_ref[hd] = kv[:, hw + hd * LANE: hw + (hd + 1) * LANE].T.astype(BF16)

    def rope128(t):
        return t * cosn + pltpu.roll(t, NSA_HEAD_DIM // 2, 1) * sinn

    for hd in range(NSA_HEADS):
        t = h[:, _O_NQ + hd * LANE:_O_NQ + (hd + 1) * LANE]
        qn_ref[hd] = (rope128(t) * NSA_SCALE).astype(BF16)

    row0 = pl.program_id(0) * tm
    kpos = row0 + lax.broadcasted_iota(jnp.int32, (tm, LANE), 0)
    lane = lax.broadcasted_iota(jnp.int32, (tm, LANE), 1)
    onehot = jnp.where(((kpos // SEL_BLOCK) % LANE) == lane, 1.0, 0.0).astype(BF16)
    for hk in range(NSA_KV_HEADS):
        sl = lambda off: slice(off + hk * LANE, off + (hk + 1) * LANE)
        kc_ref[hk] = rope128(h[:, sl(_O_KC)])
        vc_ref[hk] = h[:, sl(_O_VC)]
        ks_ref[hk, :, 0:LANE] = rope128(h[:, sl(_O_KS)]).astype(BF16)
        ks_ref[hk, :, LANE:2 * LANE] = onehot
        vs_ref[hk] = h[:, sl(_O_VS)].T.astype(BF16)
        kw_ref[hk] = rope128(h[:, sl(_O_KW)]).astype(BF16)
        vw_ref[hk] = h[:, sl(_O_VW)].T.astype(BF16)
        gt_ref[hk] = jax.nn.sigmoid(h[:, sl(_O_G)] + gb_ref[hk]).T[0:GATE_ROWS]


def _const_spec(shape):
    nd = len(shape)
    return pl.BlockSpec(shape, lambda i: (0,) * nd)


def _proj(x1, win_p, gq, gkv, wuq_all, wukv_p, gb_p, cosm, sinm, cosn, sinn):
    s, d = x1.shape
    tm = PROJ_TM
    grid = (s // tm,)
    row_spec = lambda w: pl.BlockSpec((tm, w), lambda i: (i, 0))
    head_spec = lambda nh, w: pl.BlockSpec((nh, tm, w), lambda i: (0, i, 0))
    out_shape = (
        jax.ShapeDtypeStruct((MLA_HEADS, s, MLA_QK_PAD), BF16),
        jax.ShapeDtypeStruct((MLA_HEADS, s, MLA_QK_PAD), BF16),
        jax.ShapeDtypeStruct((MLA_HEADS, MLA_V_DIM, s), BF16),
        jax.ShapeDtypeStruct((NSA_HEADS, s, NSA_HEAD_DIM), BF16),
        jax.ShapeDtypeStruct((NSA_KV_HEADS, s, NSA_HEAD_DIM), F32),
        jax.ShapeDtypeStruct((NSA_KV_HEADS, s, NSA_HEAD_DIM), F32),
        jax.ShapeDtypeStruct((NSA_KV_HEADS, s, 2 * LANE), BF16),
        jax.ShapeDtypeStruct((NSA_KV_HEADS, NSA_HEAD_DIM, s), BF16),
        jax.ShapeDtypeStruct((NSA_KV_HEADS, s, NSA_HEAD_DIM), BF16),
        jax.ShapeDtypeStruct((NSA_KV_HEADS, NSA_HEAD_DIM, s), BF16),
        jax.ShapeDtypeStruct((NSA_KV_HEADS, GATE_ROWS, s), F32),
    )
    head_spec_t = lambda nh, r: pl.BlockSpec((nh, r, tm), lambda i: (0, 0, i))
    out_specs = (
        head_spec(MLA_HEADS, MLA_QK_PAD), head_spec(MLA_HEADS, MLA_QK_PAD), head_spec_t(MLA_HEADS, MLA_V_DIM),
        head_spec(NSA_HEADS, NSA_HEAD_DIM),
        head_spec(NSA_KV_HEADS, NSA_HEAD_DIM), head_spec(NSA_KV_HEADS, NSA_HEAD_DIM),
        head_spec(NSA_KV_HEADS, 2 * LANE), head_spec_t(NSA_KV_HEADS, NSA_HEAD_DIM),
        head_spec(NSA_KV_HEADS, NSA_HEAD_DIM), head_spec_t(NSA_KV_HEADS, NSA_HEAD_DIM),
        head_spec_t(NSA_KV_HEADS, GATE_ROWS),
    )
    return pl.pallas_call(
        _proj_kernel,
        grid=grid,
        in_specs=[
            row_spec(d),
            _const_spec(win_p.shape), _const_spec(gq.shape), _const_spec(gkv.shape),
            _const_spec(wuq_all.shape), _const_spec(wukv_p.shape), _const_spec(gb_p.shape),
            row_spec(LANE), row_spec(LANE), row_spec(LANE), row_spec(LANE),
        ],
        out_specs=out_specs,
        out_shape=out_shape,
        compiler_params=pltpu.CompilerParams(
            dimension_semantics=("parallel",), vmem_limit_bytes=VMEM_LIMIT),
        name="proj",
    )(x1, win_p, gq, gkv, wuq_all, wukv_p, gb_p, cosm, sinm, cosn, sinn)


_CMP_HALF = CMP_STRIDE * NSA_HEAD_DIM


def _compress_kernel(t_ref, pe_ref, w1_ref, w2_ref, o_ref, *, transpose_out):
    hi = lax.Precision.HIGHEST
    t = t_ref[0]
    w1a = w1_ref[0:_CMP_HALF, :]
    w1b = w1_ref[_CMP_HALF:2 * _CMP_HALF, :]
    first = _dot(t, w1a, precision=hi)
    second = _dot(t, w1b, precision=hi)
    pe_row = _dot(pe_ref[...], w1_ref[...], precision=hi)[0:1]
    hid = first + pltpu.roll(second, N_CMP_PAD - 1, 0) + pe_row
    out = _dot(jax.nn.gelu(hid), w2_ref[...], precision=hi)
    o_ref[0] = (out.T if transpose_out else out).astype(o_ref.dtype)


def _compress(t16, pe_flat8, w1, w2, transpose_out):
    out_dims = (NSA_HEAD_DIM, N_CMP_PAD) if transpose_out else (N_CMP_PAD, NSA_HEAD_DIM)
    return pl.pallas_call(
        functools.partial(_compress_kernel, transpose_out=transpose_out),
        grid=(NSA_KV_HEADS,),
        in_specs=[
            pl.BlockSpec((1, N_CMP_PAD, _CMP_HALF), lambda i: (i, 0, 0)),
            _const_spec(pe_flat8.shape), _const_spec(w1.shape), _const_spec(w2.shape),
        ],
        out_specs=pl.BlockSpec((1,) + out_dims, lambda i: (i, 0, 0)),
        out_shape=jax.ShapeDtypeStruct((NSA_KV_HEADS,) + out_dims, BF16),
        compiler_params=pltpu.CompilerParams(
            dimension_semantics=("parallel",), vmem_limit_bytes=VMEM_LIMIT),
        name="compress",
    )(t16, pe_flat8, w1, w2)


def _flash_init(m_ref, l_ref, acc_ref):
    m_ref[...] = jnp.full_like(m_ref, M_INIT)
    l_ref[...] = jnp.zeros_like(l_ref)
    acc_ref[...] = jnp.zeros_like(acc_ref)


def _flash_update(st, vt, m_ref, l_ref, acc_ref):
    m_prev = m_ref[...]
    m_new = jnp.maximum(m_prev, jnp.max(st, axis=0, keepdims=True))
    alpha = jnp.exp2(m_prev - m_new)
    p = jnp.exp2(st - m_new)
    l_ref[...] = alpha * l_ref[...] + jnp.sum(p, axis=0, keepdims=True)
    acc_ref[...] = alpha * acc_ref[...] + _dot(vt, p.astype(BF16))
    m_ref[...] = m_new


FLASH_PAIRS_PER_TRIP = 2


def _flash_causal(scores, values, causal_mask, n_plain, last_tile, two_diagonal_tiles,
                  sa_ref, sb_ref, m_ref, l_ref, acc_ref):
    _flash_init(m_ref, l_ref, acc_ref)
    sa_ref[...] = scores(0)

    def plain_pair(t):
        sb_ref[...] = scores(t + 1)
        _flash_update(sa_ref[...], values(t), m_ref, l_ref, acc_ref)
        sa_ref[...] = scores(t + 2)
        _flash_update(sb_ref[...], values(t + 1), m_ref, l_ref, acc_ref)

    def body(j, carry):
        for u in range(FLASH_PAIRS_PER_TRIP):
            plain_pair(2 * (FLASH_PAIRS_PER_TRIP * j + u))
        return carry

    n_pairs = n_plain // 2
    n_trips = n_pairs // FLASH_PAIRS_PER_TRIP
    lax.fori_loop(0, n_trips, body, 0)
    for u in range(FLASH_PAIRS_PER_TRIP - 1):
        @pl.when(n_trips * FLASH_PAIRS_PER_TRIP + u < n_pairs)
        def _():
            plain_pair(2 * (n_trips * FLASH_PAIRS_PER_TRIP + u))
    t = 2 * n_pairs

    def masked_update(tile, st):
        _flash_update(jnp.where(causal_mask(tile), st, MASK_BIAS), values(tile), m_ref, l_ref, acc_ref)

    if two_diagonal_tiles:
        half = m_ref.shape[1] // 2
        right = lambda ref: ref.at[:, half:]
        masked_update(t, sa_ref[...])
        st = jnp.where(causal_mask(t + 1, half), scores(t + 1, half), MASK_BIAS)
        _flash_update(st, values(t + 1), right(m_ref), right(l_ref), right(acc_ref))
    else:
        masked_update(t, sa_ref[...])

        @pl.when(t + 1 <= last_tile)
        def _():
            masked_update(t + 1, scores(t + 1))


def _masked_probs_t(st, mask):
    st = jnp.where(mask, st, NEG_INF)
    m = jnp.max(st, axis=0, keepdims=True)
    p = jnp.where(mask, jnp.exp2(st - m), 0.0)
    return p * (1.0 / jnp.maximum(jnp.sum(p, axis=0, keepdims=True), 1e-30))


MLA_TQ = 1024
MLA_TK = 512


def _mla_kernel(q_ref, k_ref, vt_ref, o_ref, qt_ref, sa_ref, sb_ref, m_ref, l_ref, acc_ref):
    i = pl.program_id(1)
    qt_ref[...] = q_ref[0].astype(F32).T.astype(BF16)

    def scores(kt, first_query=0):
        k0 = pl.multiple_of(kt * MLA_TK, MLA_TK)
        return _dot(k_ref[0, pl.ds(k0, MLA_TK), :], qt_ref[:, first_query:])

    def values(kt):
        return vt_ref[0, :, pl.ds(pl.multiple_of(kt * MLA_TK, MLA_TK), MLA_TK)]

    qpos = i * MLA_TQ + lax.broadcasted_iota(jnp.int32, (1, MLA_TQ), 1)

    def causal_mask(kt, first_query=0):
        return kt * MLA_TK + lax.broadcasted_iota(jnp.int32, (MLA_TK, 1), 0) <= qpos[:, first_query:]

    tiles_per_q = MLA_TQ // MLA_TK
    _flash_causal(scores, values, causal_mask, i * tiles_per_q, (i + 1) * tiles_per_q - 1, tiles_per_q == 2,
                  sa_ref, sb_ref, m_ref, l_ref, acc_ref)
    o_ref[...] = (acc_ref[...] * (1.0 / l_ref[...])).T.astype(o_ref.dtype)


def _mla(q, k, vt):
    nh, s, dq = q.shape
    dv = vt.shape[1]
    assert MLA_TQ in (MLA_TK, 2 * MLA_TK) and s % MLA_TQ == 0
    return pl.pallas_call(
        _mla_kernel,
        grid=(nh, s // MLA_TQ),
        in_specs=[
            pl.BlockSpec((1, MLA_TQ, dq), lambda h, i: (h, i, 0)),
            pl.BlockSpec((1, s, dq), lambda h, i: (h, 0, 0)),
            pl.BlockSpec((1, dv, s), lambda h, i: (h, 0, 0)),
        ],
        out_specs=pl.BlockSpec((MLA_TQ, dv), lambda h, i: (i, h)),
        out_shape=jax.ShapeDtypeStruct((s, nh * dv), BF16),
        scratch_shapes=[pltpu.VMEM((dq, MLA_TQ), BF16),
                        pltpu.VMEM((MLA_TK, MLA_TQ), F32), pltpu.VMEM((MLA_TK, MLA_TQ), F32),
                        pltpu.VMEM((1, MLA_TQ), F32), pltpu.VMEM((1, MLA_TQ), F32),
                        pltpu.VMEM((dv, MLA_TQ), F32)],
        compiler_params=pltpu.CompilerParams(
            dimension_semantics=("parallel", "arbitrary"), vmem_limit_bytes=VMEM_LIMIT),
        name="mla",
    )(q, k, vt)


NSA_TQ = 256
NSA_ROWS = NSA_GROUP * NSA_TQ
NSA_TK = 512
_SEL_HALF_KEYS = LANE * SEL_BLOCK
_WIN_TILES = WINDOW // NSA_TQ + 1
SEL_CLASS_BLOCKS = 64


def _nsa_kernel(q_ref, kc_ref, vct_ref, ovt_ref, ks_ref, vst_ref, *rest):
    kw_refs = rest[:_WIN_TILES]
    vwt_refs = rest[_WIN_TILES:2 * _WIN_TILES]
    (wbias_ref, qoh_ref, gt_ref, o_ref,
     qaug_ref, oc_ref, sa_ref, sb_ref, m_ref, l_ref, acc_ref) = rest[2 * _WIN_TILES:]

    i = pl.program_id(1)
    q0 = i * NSA_TQ
    qt = q_ref[...].reshape(NSA_ROWS, NSA_HEAD_DIM).astype(F32).T.astype(BF16)
    tq = q0 + (lax.broadcasted_iota(jnp.int32, (1, NSA_ROWS), 1) & (NSA_TQ - 1))
    tq1 = q0 + lax.broadcasted_iota(jnp.int32, (1, NSA_TQ), 1)

    q_oh = jnp.concatenate([qt, qoh_ref[...]], axis=0)

    kw = jnp.concatenate([r[0] for r in kw_refs], axis=0)
    vwt = jnp.concatenate([r[0] for r in vwt_refs], axis=1)
    s_w = _dot(jnp.concatenate([kw, wbias_ref[0]], axis=1), q_oh)
    e_w = jnp.exp2(s_w - jnp.max(s_w, axis=0, keepdims=True))
    o_w = _dot(vwt, e_w.astype(BF16)) * (1.0 / jnp.sum(e_w, axis=0, keepdims=True))

    def compress_and_select(nb):
        nc = nb * (SEL_BLOCK // CMP_STRIDE)
        cend = lax.broadcasted_iota(jnp.int32, (nc, NSA_TQ), 0) * CMP_STRIDE + (CMP_BLOCK - 1)
        cbias = jnp.where(cend <= tq1, 0.0, NEG_INF).astype(BF16)
        s_c = _dot(jnp.concatenate([kc_ref[0, 0:nc, :], cbias], axis=1), q_oh)
        e_c = jnp.exp2(s_c - jnp.max(s_c, axis=0, keepdims=True))
        inv_c = jnp.where(tq >= CMP_BLOCK - 1, 1.0 / jnp.sum(e_c, axis=0, keepdims=True), 0.0)
        p_c = e_c * inv_c
        oc_ref[...] = _dot(vct_ref[0, :, 0:nc], p_c.astype(BF16))

        psum = p_c[:, 0:NSA_TQ]
        for g in range(1, NSA_GROUP):
            psum = psum + p_c[:, g * NSA_TQ:(g + 1) * NSA_TQ]
        p_hi = psum.astype(BF16)
        p_lo = (psum - p_hi.astype(F32)).astype(BF16)
        ovt = ovt_ref[0:nb, 0:nc]
        imp = _dot(ovt, p_hi) + _dot(ovt, p_lo)

        bj = lax.broadcasted_iota(jnp.int32, (nb, NSA_TQ), 0)
        cur = tq1 // SEL_BLOCK
        valid = bj * SEL_BLOCK <= tq1
        forced = (bj == 0) | (bj == cur) | (bj == cur - 1)
        score = jnp.where(valid, imp + FORCE_BONUS * jnp.where(forced, 1.0, 0.0), NEG_INF)
        bjf = bj.astype(F32)
        work = score
        for _ in range(SEL_TOPK):
            mx = jnp.max(work, axis=0, keepdims=True)
            first = jnp.min(jnp.where(work == mx, bjf, float(N_SEL)), axis=0, keepdims=True)
            work = jnp.where(bjf == first, TOPK_TAKEN, work)
        selected = (work < 0.5 * TOPK_TAKEN) & (score > 0.5 * NEG_INF)
        bias = jnp.where(selected, 0.0, MASK_BIAS)
        if nb < N_SEL:
            bias = jnp.concatenate([bias, jnp.full((N_SEL - nb, NSA_TQ), MASK_BIAS, F32)], axis=0)
        bias = jnp.concatenate([bias.astype(BF16)] * NSA_GROUP, axis=1)
        for half in range(N_SEL // LANE):
            qaug_ref[half, 0:NSA_HEAD_DIM, :] = qt
            qaug_ref[half, NSA_HEAD_DIM:NSA_HEAD_DIM + LANE, :] = bias[half * LANE:(half + 1) * LANE, :]

    size_class = (q0 + NSA_TQ - 1) // (SEL_BLOCK * SEL_CLASS_BLOCKS)
    for c in range(N_SEL // SEL_CLASS_BLOCKS):
        pl.when(size_class == c)(functools.partial(compress_and_select, (c + 1) * SEL_CLASS_BLOCKS))
    o_c = oc_ref[...]

    last = (q0 + NSA_TQ - 1) // NSA_TK

    def scores(kt):
        k0 = pl.multiple_of(kt * NSA_TK, NSA_TK)
        qa = qaug_ref[kt // (_SEL_HALF_KEYS // NSA_TK)]
        return _dot(ks_ref[0, pl.ds(k0, NSA_TK), :], qa)

    def values(kt):
        return vst_ref[0, :, pl.ds(pl.multiple_of(kt * NSA_TK, NSA_TK), NSA_TK)]

    def causal_mask(kt):
        return kt * NSA_TK + lax.broadcasted_iota(jnp.int32, (NSA_TK, 1), 0) <= tq

    _flash_causal(scores, values, causal_mask, last, last, False, sa_ref, sb_ref, m_ref, l_ref, acc_ref)
    o_s = acc_ref[...] * (1.0 / l_ref[...])

    gt = gt_ref[0]
    gate = lambda br: jnp.concatenate(
        [gt[br * NSA_GROUP + g:br * NSA_GROUP + g + 1, :] for g in range(NSA_GROUP)], axis=1)
    out = (gate(0) * o_c + gate(1) * o_s + gate(2) * o_w).T
    for g in range(NSA_GROUP):
        o_ref[:, g * NSA_HEAD_DIM:(g + 1) * NSA_HEAD_DIM] = out[g * NSA_TQ:(g + 1) * NSA_TQ].astype(o_ref.dtype)


def _nsa(q, kc, vct, ovt, ks, vst, kw, vwt, gates):
    s = q.shape[1]
    d = NSA_HEAD_DIM
    assert NSA_TQ <= NSA_TK and NSA_TK % NSA_TQ == 0

    def win_specs(transposed):
        def spec(j):
            back = _WIN_TILES - 1 - j
            if transposed:
                return pl.BlockSpec((1, d, NSA_TQ), lambda h, i: (h, 0, jnp.maximum(i - back, 0)))
            return pl.BlockSpec((1, NSA_TQ, d), lambda h, i: (h, jnp.maximum(i - back, 0), 0))
        return [spec(j) for j in range(_WIN_TILES)]

    return pl.pallas_call(
        _nsa_kernel,
        grid=(NSA_KV_HEADS, s // NSA_TQ),
        in_specs=[
            pl.BlockSpec((NSA_GROUP, NSA_TQ, d), lambda h, i: (h, i, 0)),
            pl.BlockSpec((1, N_CMP_PAD, d), lambda h, i: (h, 0, 0)),
            pl.BlockSpec((1, d, N_CMP_PAD), lambda h, i: (h, 0, 0)),
            pl.BlockSpec((N_SEL, N_CMP_PAD), lambda h, i: (0, 0)),
            pl.BlockSpec((1, s, 2 * LANE), lambda h, i: (h, 0, 0)),
            pl.BlockSpec((1, d, s), lambda h, i: (h, 0, 0)),
            *win_specs(False), *win_specs(True),
            pl.BlockSpec((1, WINDOW + NSA_TQ, NSA_TQ), lambda h, i: (jnp.minimum(i, _WIN_TILES - 1), 0, 0)),
            pl.BlockSpec((NSA_TQ, NSA_ROWS), lambda h, i: (0, 0)),
            pl.BlockSpec((1, GATE_ROWS, NSA_TQ), lambda h, i: (h, 0, i)),
        ],
        out_specs=pl.BlockSpec((NSA_TQ, NSA_GROUP * d), lambda h, i: (i, h)),
        out_shape=jax.ShapeDtypeStruct((s, NSA_HEADS * d), BF16),
        scratch_shapes=[pltpu.VMEM((N_SEL // LANE, NSA_HEAD_DIM + LANE, NSA_ROWS), BF16),
                        pltpu.VMEM((d, NSA_ROWS), F32),
                        pltpu.VMEM((NSA_TK, NSA_ROWS), F32), pltpu.VMEM((NSA_TK, NSA_ROWS), F32),
                        pltpu.VMEM((1, NSA_ROWS), F32), pltpu.VMEM((1, NSA_ROWS), F32),
                        pltpu.VMEM((d, NSA_ROWS), F32)],
        compiler_params=pltpu.CompilerParams(
            dimension_semantics=("parallel", "arbitrary"), vmem_limit_bytes=VMEM_LIMIT),
        name="nsa",
    )(q, kc, vct, ovt, ks, vst, *([kw] * _WIN_TILES), *([vwt] * _WIN_TILES), _window_bias(), _query_onehot(), gates)


def _window_bias():
    r = np.arange(WINDOW + NSA_TQ)[None, :, None]
    c = np.arange(NSA_TQ)[None, None, :]
    q0 = (np.arange(_WIN_TILES) * NSA_TQ)[:, None, None]
    back = WINDOW + c - r
    ok = (back >= 0) & (back < WINDOW) & ((q0 - WINDOW + r >= 0) | (q0 >= WINDOW))
    return jnp.asarray(np.where(ok, 0.0, NEG_INF), dtype=BF16)


def _query_onehot():
    return jnp.asarray(np.tile(np.eye(NSA_TQ, dtype=np.float32), (1, NSA_GROUP)), dtype=BF16)


OUT_TM = 512


def _out_ln_kernel(om_ref, on_ref, wa_ref, wb_ref, x_ref, g_ref, b_ref, o_ref):
    mix = _dot(om_ref[...], wa_ref[...]) + _dot(on_ref[...], wb_ref[...])
    o_ref[...] = _layer_norm(DEEPNORM_ALPHA * x_ref[...] + mix, g_ref[...], b_ref[...])


def _out_ln(o_mla, o_nsa, wa, wb, x1, g, b):
    s, d = x1.shape
    tm = OUT_TM
    return pl.pallas_call(
        _out_ln_kernel,
        grid=(s // tm,),
        in_specs=[
            pl.BlockSpec((tm, o_mla.shape[1]), lambda i: (i, 0)),
            pl.BlockSpec((tm, o_nsa.shape[1]), lambda i: (i, 0)),
            _const_spec(wa.shape), _const_spec(wb.shape),
            pl.BlockSpec((tm, d), lambda i: (i, 0)),
            _const_spec(g.shape), _const_spec(b.shape),
        ],
        out_specs=pl.BlockSpec((tm, d), lambda i: (i, 0)),
        out_shape=jax.ShapeDtypeStruct((s, d), F32),
        compiler_params=pltpu.CompilerParams(
            dimension_semantics=("parallel",), vmem_limit_bytes=VMEM_LIMIT),
        name="out_ln",
    )(o_mla, o_nsa, wa, wb, x1, g, b)


def _pad_cols(w, n):
    return jnp.pad(w, ((0, 0), (0, n - w.shape[1])))


def _relayout_w_in(w_in, gate_b):
    sizes = (MLA_Q_RANK, MLA_KV_RANK, MLA_ROPE_DIM, NSA_HEADS * NSA_HEAD_DIM) + (NSA_KV_HEADS * NSA_HEAD_DIM,) * 6 \
        + (N_BRANCH * NSA_HEADS,)
    offs = [int(v) for v in np.cumsum(sizes)[:-1]]
    c_q, c_kv, k_r, q_n, k_c, v_c, k_s, v_s, k_w, v_w, g = jnp.split(w_in, offs, axis=1)
    half = MLA_ROPE_DIM // 2
    k_r_sw = jnp.concatenate([k_r[:, half:], k_r[:, :half]], axis=1)
    gate_idx = np.array([[br * NSA_HEADS + hk * NSA_GROUP + gi for br in range(N_BRANCH) for gi in range(NSA_GROUP)]
                         for hk in range(NSA_KV_HEADS)])
    g_chunks = [_pad_cols(g[:, gate_idx[hk]], LANE) for hk in range(NSA_KV_HEADS)]
    w = jnp.concatenate([c_q, c_kv, q_n, k_c, v_c, k_s, v_s, k_w, v_w,
                         _pad_cols(k_r, LANE), _pad_cols(k_r_sw, LANE)] + g_chunks, axis=1)
    assert w.shape[1] == _PROJ_COLS
    gb = jnp.stack([jnp.pad(gate_b[gate_idx[hk]], (0, LANE - gate_idx.shape[1])) for hk in range(NSA_KV_HEADS)])
    return w.astype(BF16), gb.reshape(NSA_KV_HEADS, 1, LANE)


def _relayout_w_uq(w_uq):
    dh = MLA_NOPE_DIM + MLA_ROPE_DIM
    half = MLA_ROPE_DIM // 2
    w = w_uq.reshape(MLA_Q_RANK, MLA_HEADS, dh)
    nope = w[:, :, :MLA_NOPE_DIM]
    r = w[:, :, MLA_NOPE_DIM:]
    r_sw = jnp.concatenate([r[:, :, half:], r[:, :, :half]], axis=2)
    pad = lambda t: jnp.pad(t, ((0, 0), (0, 0), (0, LANE - MLA_ROPE_DIM)))
    parts = [nope, pad(r), pad(r_sw)]
    return jnp.concatenate([p.reshape(MLA_Q_RANK, MLA_HEADS * LANE) for p in parts], axis=1).astype(BF16)


def _relayout_w_ukv(w_ukv):
    w = w_ukv.reshape(MLA_KV_RANK, MLA_HEADS, MLA_NOPE_DIM + MLA_V_DIM)
    k = w[:, :, :MLA_NOPE_DIM].reshape(MLA_KV_RANK, MLA_HEADS * MLA_NOPE_DIM)
    v = w[:, :, MLA_NOPE_DIM:].reshape(MLA_KV_RANK, MLA_HEADS * MLA_V_DIM)
    return jnp.concatenate([k, v], axis=1).astype(BF16)


def _rope_tables(s):
    pos = jnp.arange(s).astype(F32)[:, None]

    def cs(d):
        half = d // 2
        inv = ROPE_THETA ** (-jnp.arange(half, dtype=F32) * (2.0 / d))
        ang = pos * inv[None, :]
        return jnp.cos(ang), jnp.sin(ang)

    cn, sn = cs(NSA_HEAD_DIM)
    assert NSA_HEAD_DIM == 2 * MLA_ROPE_DIM
    cm, sm = cn[:, ::2], sn[:, ::2]
    cosm = _pad_cols(jnp.concatenate([cm, cm], axis=1), LANE)
    sinm = _pad_cols(jnp.concatenate([-sm, sm], axis=1), LANE)
    cosn = jnp.concatenate([cn, cn], axis=1)
    sinn = jnp.concatenate([-sn, sn], axis=1)
    return cosm, sinm, cosn, sinn


def _overlap_weights():
    ci = np.arange(N_CMP_PAD)[:, None] * CMP_STRIDE
    sj = np.arange(N_SEL)[None, :] * SEL_BLOCK
    ov = np.clip(np.minimum(ci + CMP_BLOCK, sj + SEL_BLOCK) - np.maximum(ci, sj), 0, None)
    ov[N_CMP:] = 0
    return jnp.asarray((ov.astype(np.float32) / CMP_STRIDE).T, dtype=BF16)


def kernel(x, ffn1_w_gate, ffn1_w_up, ffn1_w_down, ln1_g, ln1_b, w_in, mla_q_norm_g, mla_w_uq, mla_kv_norm_g,
           mla_w_ukv, nsa_gate_b, nsa_cmp_pe_k, nsa_cmp_w1_k, nsa_cmp_w2_k, nsa_cmp_pe_v, nsa_cmp_w1_v,
           nsa_cmp_w2_v, w_out, ln2_g, ln2_b, ffn2_w_gate, ffn2_w_up, ffn2_w_down, ln3_g, ln3_b):
    b, s, d = x.shape
    assert (b, s, d) == (1, SEQ, D_MODEL) and ffn1_w_gate.shape[0] == DEPTH
    l = 0
    row = lambda v: v.reshape(1, -1)
    xs = x.reshape(s, d)

    x1 = _ffn_ln(xs, ffn1_w_gate[l].astype(BF16), ffn1_w_up[l].astype(BF16), ffn1_w_down[l].astype(BF16),
                 row(ln1_g[l]), row(ln1_b[l]))

    win_p, gb_p = _relayout_w_in(w_in[l], nsa_gate_b[l])
    cosm, sinm, cosn, sinn = _rope_tables(s)
    (q_m, k_m, v_m, q_n, kc_in, vc_in, ks, vs, kw, vw, gates) = _proj(
        x1, win_p, row(mla_q_norm_g[l]), row(mla_kv_norm_g[l]),
        _relayout_w_uq(mla_w_uq[l]), _relayout_w_ukv(mla_w_ukv[l]), gb_p, cosm, sinm, cosn, sinn)

    def compress(t, pe, w1, w2, transpose_out):
        t16 = t.reshape(NSA_KV_HEADS, N_CMP_PAD, _CMP_HALF)
        pe8 = jnp.broadcast_to(pe.reshape(1, CMP_BLOCK * NSA_HEAD_DIM), (8, CMP_BLOCK * NSA_HEAD_DIM))
        return _compress(t16, pe8, w1, w2, transpose_out)

    kc = compress(kc_in, nsa_cmp_pe_k[l], nsa_cmp_w1_k[l], nsa_cmp_w2_k[l], False)
    vc = compress(vc_in, nsa_cmp_pe_v[l], nsa_cmp_w1_v[l], nsa_cmp_w2_v[l], True)

    o_mla = _mla(q_m, k_m, v_m)
    o_nsa = _nsa(q_n, kc, vc, _overlap_weights(), ks, vs, kw, vw, gates)

    n_mla = MLA_HEADS * MLA_V_DIM
    w_o = w_out[l].astype(BF16)
    x2 = _out_ln(o_mla, o_nsa, w_o[:n_mla], w_o[n_mla:], x1, row(ln2_g[l]), row(ln2_b[l]))

    x3 = _ffn_ln(x2, ffn2_w_gate[l].astype(BF16), ffn2_w_up[l].astype(BF16), ffn2_w_down[l].astype(BF16),
                 row(ln3_g[l]), row(ln3_b[l]))
    return x3.reshape(b, s, d)
```

```python
import functools

import jax
import jax.numpy as jnp
import numpy as np
from jax import lax
from jax.experimental import pallas as pl
from jax.experimental.pallas import tpu as pltpu

F32 = jnp.float32
BF16 = jnp.bfloat16

D_MODEL = 2048
SEQ = 16384
DEPTH = 1
DEEPNORM_ALPHA = (2.0 * DEPTH) ** 0.25
LN_EPS = 1e-5
RMS_EPS = 1e-6
ROPE_THETA = 10000.0
NEG_INF = -1e30
D_FF = 5632
FFN_RES_WEIGHT = 0.5

MLA_HEADS = 8
MLA_Q_RANK = 512
MLA_KV_RANK = 256
MLA_NOPE_DIM = 128
MLA_ROPE_DIM = 64
MLA_V_DIM = 128
MLA_QK_PAD = 256

NSA_HEADS = 8
NSA_KV_HEADS = 2
NSA_HEAD_DIM = 128
NSA_GROUP = NSA_HEADS // NSA_KV_HEADS
CMP_BLOCK = 32
CMP_STRIDE = 16
CMP_HIDDEN = 256
SEL_BLOCK = 64
SEL_TOPK = 16
WINDOW = 512
N_BRANCH = 3
FORCE_BONUS = 1e4
N_CMP = (SEQ - CMP_BLOCK) // CMP_STRIDE + 1
N_CMP_PAD = SEQ // CMP_STRIDE
N_SEL = SEQ // SEL_BLOCK

LANE = 128
MASK_BIAS = NEG_INF
TOPK_TAKEN = -2.0 ** 127
M_INIT = -1e30

VMEM_LIMIT = 56 * 1024 * 1024


def _dot(a, b, **kw):
    return jnp.dot(a, b, preferred_element_type=F32, **kw)


def _layer_norm(y, g, b):
    mu = jnp.mean(y, axis=-1, keepdims=True)
    yc = y - mu
    var = jnp.mean(yc * yc, axis=-1, keepdims=True)
    return yc * lax.rsqrt(var + LN_EPS) * g + b


def _rms_norm(x, g):
    ms = jnp.mean(x * x, axis=-1, keepdims=True)
    return x * lax.rsqrt(ms + RMS_EPS) * g


FFN_TM = 512
FFN_TF = 512


def _ffn_ln_kernel(x_ref, wg_ref, wu_ref, wd_ref, g_ref, b_ref, o_ref, xb_ref, acc_ref):
    k = pl.program_id(1)

    @pl.when(k == 0)
    def _():
        xb_ref[...] = x_ref[...].astype(BF16)
        acc_ref[...] = jnp.zeros_like(acc_ref)

    xb = xb_ref[...]
    gate = _dot(xb, wg_ref[...])
    up = _dot(xb, wu_ref[...])
    h = (jax.nn.silu(gate) * up).astype(BF16)
    acc_ref[...] += _dot(h, wd_ref[...])

    @pl.when(k == pl.num_programs(1) - 1)
    def _():
        y = DEEPNORM_ALPHA * x_ref[...] + FFN_RES_WEIGHT * acc_ref[...]
        o_ref[...] = _layer_norm(y, g_ref[...], b_ref[...])


def _ffn_ln(x, wg, wu, wd, g, b):
    s, d = x.shape
    f = wg.shape[1]
    grid = (s // FFN_TM, f // FFN_TF)
    return pl.pallas_call(
        _ffn_ln_kernel,
        grid=grid,
        in_specs=[
            pl.BlockSpec((FFN_TM, d), lambda i, k: (i, 0)),
            pl.BlockSpec((d, FFN_TF), lambda i, k: (0, k)),
            pl.BlockSpec((d, FFN_TF), lambda i, k: (0, k)),
            pl.BlockSpec((FFN_TF, d), lambda i, k: (k, 0)),
            pl.BlockSpec((1, d), lambda i, k: (0, 0)),
            pl.BlockSpec((1, d), lambda i, k: (0, 0)),
        ],
        out_specs=pl.BlockSpec((FFN_TM, d), lambda i, k: (i, 0)),
        out_shape=jax.ShapeDtypeStruct((s, d), F32),
        scratch_shapes=[pltpu.VMEM((FFN_TM, d), BF16), pltpu.VMEM((FFN_TM, d), F32)],
        compiler_params=pltpu.CompilerParams(
            dimension_semantics=("parallel", "arbitrary"), vmem_limit_bytes=VMEM_LIMIT),
        name="ffn_ln",
    )(x, wg, wu, wd, g, b)


PROJ_TM = 256
_O_CQ = 0
_O_CKV = _O_CQ + MLA_Q_RANK
_O_NQ = _O_CKV + MLA_KV_RANK
_O_KC = _O_NQ + NSA_HEADS * NSA_HEAD_DIM
_O_VC = _O_KC + NSA_KV_HEADS * NSA_HEAD_DIM
_O_KS = _O_VC + NSA_KV_HEADS * NSA_HEAD_DIM
_O_VS = _O_KS + NSA_KV_HEADS * NSA_HEAD_DIM
_O_KW = _O_VS + NSA_KV_HEADS * NSA_HEAD_DIM
_O_VW = _O_KW + NSA_KV_HEADS * NSA_HEAD_DIM
_O_KR = _O_VW + NSA_KV_HEADS * NSA_HEAD_DIM
_O_KRS = _O_KR + LANE
_O_G = _O_KRS + LANE
_PROJ_COLS = _O_G + NSA_KV_HEADS * LANE
GATE_ROWS = 16

LOG2E = 1.4426950408889634
MLA_SCALE = (MLA_NOPE_DIM + MLA_ROPE_DIM) ** -0.5 * LOG2E
NSA_SCALE = NSA_HEAD_DIM ** -0.5 * LOG2E


def _proj_kernel(x_ref, win_ref, gq_ref, gkv_ref, wuq_ref, wukv_ref, gb_ref,
                 cosm_ref, sinm_ref, cosn_ref, sinn_ref,
                 qm_ref, km_ref, vm_ref, qn_ref, kc_ref, vc_ref, ks_ref, vs_ref, kw_ref, vw_ref, gt_ref):
    tm = x_ref.shape[0]
    xb = x_ref[...].astype(BF16)
    h = _dot(xb, win_ref[...])
    cosm, sinm = cosm_ref[...], sinm_ref[...]
    cosn, sinn = cosn_ref[...], sinn_ref[...]

    cqn = _rms_norm(h[:, _O_CQ:_O_CQ + MLA_Q_RANK], gq_ref[...]).astype(BF16)
    qall = _dot(cqn, wuq_ref[...])
    hw = MLA_HEADS * LANE
    for hd in range(MLA_HEADS):
        sl = slice(hd * LANE, (hd + 1) * LANE)
        qm_ref[hd, :, 0:LANE] = (qall[:, sl] * MLA_SCALE).astype(BF16)
        qr = qall[:, hw + hd * LANE: hw + (hd + 1) * LANE]
        qrs = qall[:, 2 * hw + hd * LANE: 2 * hw + (hd + 1) * LANE]
        qm_ref[hd, :, LANE:2 * LANE] = ((qr * cosm + qrs * sinm) * MLA_SCALE).astype(BF16)

    ckvn = _rms_norm(h[:, _O_CKV:_O_CKV + MLA_KV_RANK], gkv_ref[...]).astype(BF16)
    kv = _dot(ckvn, wukv_ref[...])
    kr = (h[:, _O_KR:_O_KR + LANE] * cosm + h[:, _O_KRS:_O_KRS + LANE] * sinm).astype(BF16)
    for hd in range(MLA_HEADS):
        km_ref[hd, :, 0:LANE] = kv[:, hd * LANE:(hd + 1) * LANE].astype(BF16)
        km_ref[hd, :, LANE:2 * LANE] = kr
        vm_ref[hd] = kv[:, hw + hd * LANE: hw + (hd + 1) * LANE].T.astype(BF16)

    def rope128(t):
        return t * cosn + pltpu.roll(t, NSA_HEAD_DIM // 2, 1) * sinn

    for hd in range(NSA_HEADS):
        t = h[:, _O_NQ + hd * LANE:_O_NQ + (hd + 1) * LANE]
        qn_ref[hd] = (rope128(t) * NSA_SCALE).astype(BF16)

    row0 = pl.program_id(0) * tm
    kpos = row0 + lax.broadcasted_iota(jnp.int32, (tm, LANE), 0)
    lane = lax.broadcasted_iota(jnp.int32, (tm, LANE), 1)
    onehot = jnp.where(((kpos // SEL_BLOCK) % LANE) == lane, 1.0, 0.0).astype(BF16)
    for hk in range(NSA_KV_HEADS):
        sl = lambda off: slice(off + hk * LANE, off + (hk + 1) * LANE)
        kc_ref[hk] = rope128(h[:, sl(_O_KC)])
        vc_ref[hk] = h[:, sl(_O_VC)]
        ks_ref[hk, :, 0:LANE] = rope128(h[:, sl(_O_KS)]).astype(BF16)
        ks_ref[hk, :, LANE:2 * LANE] = onehot
        vs_ref[hk] = h[:, sl(_O_VS)].T.astype(BF16)
        kw_ref[hk] = rope128(h[:, sl(_O_KW)]).astype(BF16)
        vw_ref[hk] = h[:, sl(_O_VW)].T.astype(BF16)
        gt_ref[hk] = jax.nn.sigmoid(h[:, sl(_O_G)] + gb_ref[hk]).T[0:GATE_ROWS]


def _const_spec(shape):
    nd = len(shape)
    return pl.BlockSpec(shape, lambda i: (0,) * nd)


def _proj(x1, win_p, gq, gkv, wuq_all, wukv_p, gb_p, cosm, sinm, cosn, sinn):
    s, d = x1.shape
    tm = PROJ_TM
    grid = (s // tm,)
    row_spec = lambda w: pl.BlockSpec((tm, w), lambda i: (i, 0))
    head_spec = lambda nh, w: pl.BlockSpec((nh, tm, w), lambda i: (0, i, 0))
    out_shape = (
        jax.ShapeDtypeStruct((MLA_HEADS, s, MLA_QK_PAD), BF16),
        jax.ShapeDtypeStruct((MLA_HEADS, s, MLA_QK_PAD), BF16),
        jax.ShapeDtypeStruct((MLA_HEADS, MLA_V_DIM, s), BF16),
        jax.ShapeDtypeStruct((NSA_HEADS, s, NSA_HEAD_DIM), BF16),
        jax.ShapeDtypeStruct((NSA_KV_HEADS, s, NSA_HEAD_DIM), F32),
        jax.ShapeDtypeStruct((NSA_KV_HEADS, s, NSA_HEAD_DIM), F32),
        jax.ShapeDtypeStruct((NSA_KV_HEADS, s, 2 * LANE), BF16),
        jax.ShapeDtypeStruct((NSA_KV_HEADS, NSA_HEAD_DIM, s), BF16),
        jax.ShapeDtypeStruct((NSA_KV_HEADS, s, NSA_HEAD_DIM), BF16),
        jax.ShapeDtypeStruct((NSA_KV_HEADS, NSA_HEAD_DIM, s), BF16),
        jax.ShapeDtypeStruct((NSA_KV_HEADS, GATE_ROWS, s), F32),
    )
    head_spec_t = lambda nh, r: pl.BlockSpec((nh, r, tm), lambda i: (0, 0, i))
    out_specs = (
        head_spec(MLA_HEADS, MLA_QK_PAD), head_spec(MLA_HEADS, MLA_QK_PAD), head_spec_t(MLA_HEADS, MLA_V_DIM),
        head_spec(NSA_HEADS, NSA_HEAD_DIM),
        head_spec(NSA_KV_HEADS, NSA_HEAD_DIM), head_spec(NSA_KV_HEADS, NSA_HEAD_DIM),
        head_spec(NSA_KV_HEADS, 2 * LANE), head_spec_t(NSA_KV_HEADS, NSA_HEAD_DIM),
        head_spec(NSA_KV_HEADS, NSA_HEAD_DIM), head_spec_t(NSA_KV_HEADS, NSA_HEAD_DIM),
        head_spec_t(NSA_KV_HEADS, GATE_ROWS),
    )
    return pl.pallas_call(
        _proj_kernel,
        grid=grid,
        in_specs=[
            row_spec(d),
            _const_spec(win_p.shape), _const_spec(gq.shape), _const_spec(gkv.shape),
            _const_spec(wuq_all.shape), _const_spec(wukv_p.shape), _const_spec(gb_p.shape),
            row_spec(LANE), row_spec(LANE), row_spec(LANE), row_spec(LANE),
        ],
        out_specs=out_specs,
        out_shape=out_shape,
        compiler_params=pltpu.CompilerParams(
            dimension_semantics=("parallel",), vmem_limit_bytes=VMEM_LIMIT),
        name="proj",
    )(x1, win_p, gq, gkv, wuq_all, wukv_p, gb_p, cosm, sinm, cosn, sinn)


_CMP_HALF = CMP_STRIDE * NSA_HEAD_DIM


def _compress_kernel(t_ref, pe_ref, w1_ref, w2_ref, o_ref, *, transpose_out):
    hi = lax.Precision.HIGHEST
    t = t_ref[0]
    w1a = w1_ref[0:_CMP_HALF, :]
    w1b = w1_ref[_CMP_HALF:2 * _CMP_HALF, :]
    first = _dot(t, w1a, precision=hi)
    second = _dot(t, w1b, precision=hi)
    pe_row = _dot(pe_ref[...], w1_ref[...], precision=hi)[0:1]
    hid = first + pltpu.roll(second, N_CMP_PAD - 1, 0) + pe_row
    out = _dot(jax.nn.gelu(hid), w2_ref[...], precision=hi)
    o_ref[0] = (out.T if transpose_out else out).astype(o_ref.dtype)


def _compress(t16, pe_flat8, w1, w2, transpose_out):
    out_dims = (NSA_HEAD_DIM, N_CMP_PAD) if transpose_out else (N_CMP_PAD, NSA_HEAD_DIM)
    return pl.pallas_call(
        functools.partial(_compress_kernel, transpose_out=transpose_out),
        grid=(NSA_KV_HEADS,),
        in_specs=[
            pl.BlockSpec((1, N_CMP_PAD, _CMP_HALF), lambda i: (i, 0, 0)),
            _const_spec(pe_flat8.shape), _const_spec(w1.shape), _const_spec(w2.shape),
        ],
        out_specs=pl.BlockSpec((1,) + out_dims, lambda i: (i, 0, 0)),
        out_shape=jax.ShapeDtypeStruct((NSA_KV_HEADS,) + out_dims, BF16),
        compiler_params=pltpu.CompilerParams(
            dimension_semantics=("parallel",), vmem_limit_bytes=VMEM_LIMIT),
        name="compress",
    )(t16, pe_flat8, w1, w2)


def _flash_init(m_ref, l_ref, acc_ref):
    m_ref[...] = jnp.full_like(m_ref, M_INIT)
    l_ref[...] = jnp.zeros_like(l_ref)
    acc_ref[...] = jnp.zeros_like(acc_ref)


def _flash_update(st, vt, m_ref, l_ref, acc_ref):
    m_prev = m_ref[...]
    m_new = jnp.maximum(m_prev, jnp.max(st, axis=0, keepdims=True))
    alpha = jnp.exp2(m_prev - m_new)
    p = jnp.exp2(st - m_new)
    l_ref[...] = alpha * l_ref[...] + jnp.sum(p, axis=0, keepdims=True)
    acc_ref[...] = alpha * acc_ref[...] + _dot(vt, p.astype(BF16))
    m_ref[...] = m_new


FLASH_PAIRS_PER_TRIP = 2


class _FlashChain:
    def __init__(self, scores, values, causal_mask, sa_ref, sb_ref, m_ref, l_ref, acc_ref):
        self.scores, self.values, self.causal_mask = scores, values, causal_mask
        self.sa, self.sb, self.m, self.l, self.acc = sa_ref, sb_ref, m_ref, l_ref, acc_ref

    def update(self, st, tile, refs=None):
        _flash_update(st, self.values(tile), *(refs or (self.m, self.l, self.acc)))

    def masked_update(self, tile, st):
        self.update(jnp.where(self.causal_mask(tile), st, MASK_BIAS), tile)


def _flash_causal(chains, n_plain, last_tile, two_diagonal_tiles):
    for c in chains:
        _flash_init(c.m, c.l, c.acc)
        c.sa[...] = c.scores(0)

    def plain_pair(t):
        for c in chains:
            c.sb[...] = c.scores(t + 1)
            c.update(c.sa[...], t)
            c.sa[...] = c.scores(t + 2)
            c.update(c.sb[...], t + 1)

    def body(j, carry):
        for u in range(FLASH_PAIRS_PER_TRIP):
            plain_pair(2 * (FLASH_PAIRS_PER_TRIP * j + u))
        return carry

    n_pairs = n_plain // 2
    n_trips = n_pairs // FLASH_PAIRS_PER_TRIP
    lax.fori_loop(0, n_trips, body, 0)
    for u in range(FLASH_PAIRS_PER_TRIP - 1):
        @pl.when(n_trips * FLASH_PAIRS_PER_TRIP + u < n_pairs)
        def _():
            plain_pair(2 * (n_trips * FLASH_PAIRS_PER_TRIP + u))
    t = 2 * n_pairs

    if two_diagonal_tiles:
        for c in chains:
            half = c.m.shape[-1] // 2
            c.masked_update(t, c.sa[...])
            st = jnp.where(c.causal_mask(t + 1, half), c.scores(t + 1, half), MASK_BIAS)
            c.update(st, t + 1, tuple(r.at[:, half:] for r in (c.m, c.l, c.acc)))
    else:
        for c in chains:
            c.masked_update(t, c.sa[...])

        @pl.when(t + 1 <= last_tile)
        def _():
            for c in chains:
                c.masked_update(t + 1, c.scores(t + 1))


def _masked_probs_t(st, mask):
    st = jnp.where(mask, st, NEG_INF)
    m = jnp.max(st, axis=0, keepdims=True)
    p = jnp.where(mask, jnp.exp2(st - m), 0.0)
    return p * (1.0 / jnp.maximum(jnp.sum(p, axis=0, keepdims=True), 1e-30))


MLA_TQ = 1024
MLA_TK = 512


MLA_HEADS_PER_STEP = 2


def _mla_kernel(q_ref, k_ref, vt_ref, o_ref, qt_ref, sa_ref, sb_ref, m_ref, l_ref, acc_ref):
    i = pl.program_id(1)
    qpos = i * MLA_TQ + lax.broadcasted_iota(jnp.int32, (1, MLA_TQ), 1)

    def causal_mask(kt, first_query=0):
        return kt * MLA_TK + lax.broadcasted_iota(jnp.int32, (MLA_TK, 1), 0) <= qpos[:, first_query:]

    def chain(h):
        qt_ref[h] = q_ref[h].astype(F32).T.astype(BF16)

        def scores(kt, first_query=0):
            k0 = pl.multiple_of(kt * MLA_TK, MLA_TK)
            return _dot(k_ref[h, pl.ds(k0, MLA_TK), :], qt_ref[h, :, first_query:])

        def values(kt):
            return vt_ref[h, :, pl.ds(pl.multiple_of(kt * MLA_TK, MLA_TK), MLA_TK)]

        return _FlashChain(scores, values, causal_mask,
                           sa_ref.at[h], sb_ref.at[h], m_ref.at[h], l_ref.at[h], acc_ref.at[h])

    tiles_per_q = MLA_TQ // MLA_TK
    _flash_causal([chain(h) for h in range(MLA_HEADS_PER_STEP)],
                  i * tiles_per_q, (i + 1) * tiles_per_q - 1, tiles_per_q == 2)
    dv = acc_ref.shape[1]
    for h in range(MLA_HEADS_PER_STEP):
        o_ref[:, h * dv:(h + 1) * dv] = (acc_ref[h] * (1.0 / l_ref[h])).T.astype(o_ref.dtype)


def _mla(q, k, vt):
    nh, s, dq = q.shape
    dv = vt.shape[1]
    hs = MLA_HEADS_PER_STEP
    assert MLA_TQ in (MLA_TK, 2 * MLA_TK) and s % MLA_TQ == 0 and nh % hs == 0
    resident = pl.Buffered(1)
    return pl.pallas_call(
        _mla_kernel,
        grid=(nh // hs, s // MLA_TQ),
        in_specs=[
            pl.BlockSpec((hs, MLA_TQ, dq), lambda h, i: (h, i, 0)),
            pl.BlockSpec((hs, s, dq), lambda h, i: (h, 0, 0), pipeline_mode=resident),
            pl.BlockSpec((hs, dv, s), lambda h, i: (h, 0, 0), pipeline_mode=resident),
        ],
        out_specs=pl.BlockSpec((MLA_TQ, hs * dv), lambda h, i: (i, h)),
        out_shape=jax.ShapeDtypeStruct((s, nh * dv), BF16),
        scratch_shapes=[pltpu.VMEM((hs, dq, MLA_TQ), BF16),
                        pltpu.VMEM((hs, MLA_TK, MLA_TQ), F32), pltpu.VMEM((hs, MLA_TK, MLA_TQ), F32),
                        pltpu.VMEM((hs, 1, MLA_TQ), F32), pltpu.VMEM((hs, 1, MLA_TQ), F32),
                        pltpu.VMEM((hs, dv, MLA_TQ), F32)],
        compiler_params=pltpu.CompilerParams(
            dimension_semantics=("parallel", "arbitrary"), vmem_limit_bytes=VMEM_LIMIT),
        name="mla",
    )(q, k, vt)


NSA_TQ = 256
NSA_ROWS = NSA_GROUP * NSA_TQ
NSA_TK = 512
_SEL_HALF_KEYS = LANE * SEL_BLOCK
_WIN_TILES = WINDOW // NSA_TQ + 1
SEL_CLASS_BLOCKS = 64


def _nsa_kernel(q_ref, kc_ref, vct_ref, ovt_ref, ks_ref, vst_ref, *rest):
    kw_refs = rest[:_WIN_TILES]
    vwt_refs = rest[_WIN_TILES:2 * _WIN_TILES]
    (wbias_ref, qoh_ref, gt_ref, o_ref,
     qaug_ref, oc_ref, sa_ref, sb_ref, m_ref, l_ref, acc_ref) = rest[2 * _WIN_TILES:]
    heads = range(NSA_KV_HEADS)

    i = pl.program_id(0)
    q0 = i * NSA_TQ
    tq = q0 + (lax.broadcasted_iota(jnp.int32, (1, NSA_ROWS), 1) & (NSA_TQ - 1))
    tq1 = q0 + lax.broadcasted_iota(jnp.int32, (1, NSA_TQ), 1)

    qts = [q_ref[hk * NSA_GROUP:(hk + 1) * NSA_GROUP].reshape(NSA_ROWS, NSA_HEAD_DIM).astype(F32).T.astype(BF16)
           for hk in heads]
    q_ohs = [jnp.concatenate([qt, qoh_ref[...]], axis=0) for qt in qts]

    o_ws = []
    for hk in heads:
        kw = jnp.concatenate([r[hk] for r in kw_refs], axis=0)
        vwt = jnp.concatenate([r[hk] for r in vwt_refs], axis=1)
        s_w = _dot(jnp.concatenate([kw, wbias_ref[0]], axis=1), q_ohs[hk])
        e_w = jnp.exp2(s_w - jnp.max(s_w, axis=0, keepdims=True))
        o_ws.append(_dot(vwt, e_w.astype(BF16)) * (1.0 / jnp.sum(e_w, axis=0, keepdims=True)))

    def compress_and_select(nb):
        nc = nb * (SEL_BLOCK // CMP_STRIDE)
        cend = lax.broadcasted_iota(jnp.int32, (nc, NSA_TQ), 0) * CMP_STRIDE + (CMP_BLOCK - 1)
        cbias = jnp.where(cend <= tq1, 0.0, NEG_INF).astype(BF16)
        bj = lax.broadcasted_iota(jnp.int32, (nb, NSA_TQ), 0)
        cur = tq1 // SEL_BLOCK
        valid = bj * SEL_BLOCK <= tq1
        bonus = FORCE_BONUS * jnp.where((bj == 0) | (bj == cur) | (bj == cur - 1), 1.0, 0.0)
        bjf = bj.astype(F32)
        ovt = ovt_ref[0:nb, 0:nc]
        for hk in heads:
            s_c = _dot(jnp.concatenate([kc_ref[hk, 0:nc, :], cbias], axis=1), q_ohs[hk])
            e_c = jnp.exp2(s_c - jnp.max(s_c, axis=0, keepdims=True))
            inv_c = jnp.where(tq >= CMP_BLOCK - 1, 1.0 / jnp.sum(e_c, axis=0, keepdims=True), 0.0)
            p_c = e_c * inv_c
            oc_ref[hk] = _dot(vct_ref[hk, :, 0:nc], p_c.astype(BF16))

            psum = p_c[:, 0:NSA_TQ]
            for g in range(1, NSA_GROUP):
                psum = psum + p_c[:, g * NSA_TQ:(g + 1) * NSA_TQ]
            p_hi = psum.astype(BF16)
            p_lo = (psum - p_hi.astype(F32)).astype(BF16)
            imp = _dot(ovt, p_hi) + _dot(ovt, p_lo)

            score = jnp.where(valid, imp + bonus, NEG_INF)
            work = score
            for _ in range(SEL_TOPK):
                mx = jnp.max(work, axis=0, keepdims=True)
                first = jnp.min(jnp.where(work == mx, bjf, float(N_SEL)), axis=0, keepdims=True)
                work = jnp.where(bjf == first, TOPK_TAKEN, work)
            selected = (work < 0.5 * TOPK_TAKEN) & (score > 0.5 * NEG_INF)
            bias = jnp.where(selected, 0.0, MASK_BIAS)
            if nb < N_SEL:
                bias = jnp.concatenate([bias, jnp.full((N_SEL - nb, NSA_TQ), MASK_BIAS, F32)], axis=0)
            bias = jnp.concatenate([bias.astype(BF16)] * NSA_GROUP, axis=1)
            for half in range(N_SEL // LANE):
                qaug_ref[hk, half, 0:NSA_HEAD_DIM, :] = qts[hk]
                qaug_ref[hk, half, NSA_HEAD_DIM:NSA_HEAD_DIM + LANE, :] = bias[half * LANE:(half + 1) * LANE, :]

    size_class = (q0 + NSA_TQ - 1) // (SEL_BLOCK * SEL_CLASS_BLOCKS)
    for c in range(N_SEL // SEL_CLASS_BLOCKS):
        pl.when(size_class == c)(functools.partial(compress_and_select, (c + 1) * SEL_CLASS_BLOCKS))

    last = (q0 + NSA_TQ - 1) // NSA_TK

    def causal_mask(kt):
        return kt * NSA_TK + lax.broadcasted_iota(jnp.int32, (NSA_TK, 1), 0) <= tq

    def chain(hk):
        def scores(kt):
            k0 = pl.multiple_of(kt * NSA_TK, NSA_TK)
            qa = qaug_ref[hk, kt // (_SEL_HALF_KEYS // NSA_TK)]
            return _dot(ks_ref[hk, pl.ds(k0, NSA_TK), :], qa)

        def values(kt):
            return vst_ref[hk, :, pl.ds(pl.multiple_of(kt * NSA_TK, NSA_TK), NSA_TK)]

        return _FlashChain(scores, values, causal_mask,
                           sa_ref.at[hk], sb_ref.at[hk], m_ref.at[hk], l_ref.at[hk], acc_ref.at[hk])

    _flash_causal([chain(hk) for hk in heads], last, last, False)

    for hk in heads:
        o_s = acc_ref[hk] * (1.0 / l_ref[hk])
        gt = gt_ref[hk]
        gate = lambda br: jnp.concatenate(
            [gt[br * NSA_GROUP + g:br * NSA_GROUP + g + 1, :] for g in range(NSA_GROUP)], axis=1)
        out = (gate(0) * oc_ref[hk] + gate(1) * o_s + gate(2) * o_ws[hk]).T
        for g in range(NSA_GROUP):
            col = (hk * NSA_GROUP + g) * NSA_HEAD_DIM
            o_ref[:, col:col + NSA_HEAD_DIM] = out[g * NSA_TQ:(g + 1) * NSA_TQ].astype(o_ref.dtype)


def _nsa(q, kc, vct, ovt, ks, vst, kw, vwt, gates):
    s = q.shape[1]
    d = NSA_HEAD_DIM
    hk = NSA_KV_HEADS
    assert NSA_TQ <= NSA_TK and NSA_TK % NSA_TQ == 0

    def win_specs(transposed):
        def spec(j):
            back = _WIN_TILES - 1 - j
            if transposed:
                return pl.BlockSpec((hk, d, NSA_TQ), lambda i: (0, 0, jnp.maximum(i - back, 0)))
            return pl.BlockSpec((hk, NSA_TQ, d), lambda i: (0, jnp.maximum(i - back, 0), 0))
        return [spec(j) for j in range(_WIN_TILES)]

    resident = pl.Buffered(1)
    return pl.pallas_call(
        _nsa_kernel,
        grid=(s // NSA_TQ,),
        in_specs=[
            pl.BlockSpec((NSA_HEADS, NSA_TQ, d), lambda i: (0, i, 0)),
            pl.BlockSpec((hk, N_CMP_PAD, d), lambda i: (0, 0, 0)),
            pl.BlockSpec((hk, d, N_CMP_PAD), lambda i: (0, 0, 0)),
            pl.BlockSpec((N_SEL, N_CMP_PAD), lambda i: (0, 0)),
            pl.BlockSpec((hk, s, 2 * LANE), lambda i: (0, 0, 0), pipeline_mode=resident),
            pl.BlockSpec((hk, d, s), lambda i: (0, 0, 0), pipeline_mode=resident),
            *win_specs(False), *win_specs(True),
            pl.BlockSpec((1, WINDOW + NSA_TQ, NSA_TQ), lambda i: (jnp.minimum(i, _WIN_TILES - 1), 0, 0)),
            pl.BlockSpec((NSA_TQ, NSA_ROWS), lambda i: (0, 0)),
            pl.BlockSpec((hk, GATE_ROWS, NSA_TQ), lambda i: (0, 0, i)),
        ],
        out_specs=pl.BlockSpec((NSA_TQ, NSA_HEADS * d), lambda i: (i, 0)),
        out_shape=jax.ShapeDtypeStruct((s, NSA_HEADS * d), BF16),
        scratch_shapes=[pltpu.VMEM((hk, N_SEL // LANE, NSA_HEAD_DIM + LANE, NSA_ROWS), BF16),
                        pltpu.VMEM((hk, d, NSA_ROWS), F32),
                        pltpu.VMEM((hk, NSA_TK, NSA_ROWS), F32), pltpu.VMEM((hk, NSA_TK, NSA_ROWS), F32),
                        pltpu.VMEM((hk, 1, NSA_ROWS), F32), pltpu.VMEM((hk, 1, NSA_ROWS), F32),
                        pltpu.VMEM((hk, d, NSA_ROWS), F32)],
        compiler_params=pltpu.CompilerParams(
            dimension_semantics=("arbitrary",), vmem_limit_bytes=VMEM_LIMIT),
        name="nsa",
    )(q, kc, vct, ovt, ks, vst, *([kw] * _WIN_TILES), *([vwt] * _WIN_TILES), _window_bias(), _query_onehot(), gates)


def _window_bias():
    r = np.arange(WINDOW + NSA_TQ)[None, :, None]
    c = np.arange(NSA_TQ)[None, None, :]
    q0 = (np.arange(_WIN_TILES) * NSA_TQ)[:, None, None]
    back = WINDOW + c - r
    ok = (back >= 0) & (back < WINDOW) & ((q0 - WINDOW + r >= 0) | (q0 >= WINDOW))
    return jnp.asarray(np.where(ok, 0.0, NEG_INF), dtype=BF16)


def _query_onehot():
    return jnp.asarray(np.tile(np.eye(NSA_TQ, dtype=np.float32), (1, NSA_GROUP)), dtype=BF16)


OUT_TM = 512


def _out_ln_kernel(om_ref, on_ref, wa_ref, wb_ref, x_ref, g_ref, b_ref, o_ref):
    mix = _dot(om_ref[...], wa_ref[...]) + _dot(on_ref[...], wb_ref[...])
    o_ref[...] = _layer_norm(DEEPNORM_ALPHA * x_ref[...] + mix, g_ref[...], b_ref[...])


def _out_ln(o_mla, o_nsa, wa, wb, x1, g, b):
    s, d = x1.shape
    tm = OUT_TM
    return pl.pallas_call(
        _out_ln_kernel,
        grid=(s // tm,),
        in_specs=[
            pl.BlockSpec((tm, o_mla.shape[1]), lambda i: (i, 0)),
            pl.BlockSpec((tm, o_nsa.shape[1]), lambda i: (i, 0)),
            _const_spec(wa.shape), _const_spec(wb.shape),
            pl.BlockSpec((tm, d), lambda i: (i, 0)),
            _const_spec(g.shape), _const_spec(b.shape),
        ],
        out_specs=pl.BlockSpec((tm, d), lambda i: (i, 0)),
        out_shape=jax.ShapeDtypeStruct((s, d), F32),
        compiler_params=pltpu.CompilerParams(
            dimension_semantics=("parallel",), vmem_limit_bytes=VMEM_LIMIT),
        name="out_ln",
    )(o_mla, o_nsa, wa, wb, x1, g, b)


def _pad_cols(w, n):
    return jnp.pad(w, ((0, 0), (0, n - w.shape[1])))


def _relayout_w_in(w_in, gate_b):
    sizes = (MLA_Q_RANK, MLA_KV_RANK, MLA_ROPE_DIM, NSA_HEADS * NSA_HEAD_DIM) + (NSA_KV_HEADS * NSA_HEAD_DIM,) * 6 \
        + (N_BRANCH * NSA_HEADS,)
    offs = [int(v) for v in np.cumsum(sizes)[:-1]]
    c_q, c_kv, k_r, q_n, k_c, v_c, k_s, v_s, k_w, v_w, g = jnp.split(w_in, offs, axis=1)
    half = MLA_ROPE_DIM // 2
    k_r_sw = jnp.concatenate([k_r[:, half:], k_r[:, :half]], axis=1)
    gate_idx = np.array([[br * NSA_HEADS + hk * NSA_GROUP + gi for br in range(N_BRANCH) for gi in range(NSA_GROUP)]
                         for hk in range(NSA_KV_HEADS)])
    g_chunks = [_pad_cols(g[:, gate_idx[hk]], LANE) for hk in range(NSA_KV_HEADS)]
    w = jnp.concatenate([c_q, c_kv, q_n, k_c, v_c, k_s, v_s, k_w, v_w,
                         _pad_cols(k_r, LANE), _pad_cols(k_r_sw, LANE)] + g_chunks, axis=1)
    assert w.shape[1] == _PROJ_COLS
    gb = jnp.stack([jnp.pad(gate_b[gate_idx[hk]], (0, LANE - gate_idx.shape[1])) for hk in range(NSA_KV_HEADS)])
    return w.astype(BF16), gb.reshape(NSA_KV_HEADS, 1, LANE)


def _relayout_w_uq(w_uq):
    dh = MLA_NOPE_DIM + MLA_ROPE_DIM
    half = MLA_ROPE_DIM // 2
    w = w_uq.reshape(MLA_Q_RANK, MLA_HEADS, dh)
    nope = w[:, :, :MLA_NOPE_DIM]
    r = w[:, :, MLA_NOPE_DIM:]
    r_sw = jnp.concatenate([r[:, :, half:], r[:, :, :half]], axis=2)
    pad = lambda t: jnp.pad(t, ((0, 0), (0, 0), (0, LANE - MLA_ROPE_DIM)))
    parts = [nope, pad(r), pad(r_sw)]
    return jnp.concatenate([p.reshape(MLA_Q_RANK, MLA_HEADS * LANE) for p in parts], axis=1).astype(BF16)


def _relayout_w_ukv(w_ukv):
    w = w_ukv.reshape(MLA_KV_RANK, MLA_HEADS, MLA_NOPE_DIM + MLA_V_DIM)
    k = w[:, :, :MLA_NOPE_DIM].reshape(MLA_KV_RANK, MLA_HEADS * MLA_NOPE_DIM)
    v = w[:, :, MLA_NOPE_DIM:].reshape(MLA_KV_RANK, MLA_HEADS * MLA_V_DIM)
    return jnp.concatenate([k, v], axis=1).astype(BF16)


def _rope_tables(s):
    pos = jnp.arange(s).astype(F32)[:, None]

    def cs(d):
        half = d // 2
        inv = ROPE_THETA ** (-jnp.arange(half, dtype=F32) * (2.0 / d))
        ang = pos * inv[None, :]
        return jnp.cos(ang), jnp.sin(ang)

    cm, sm = cs(MLA_ROPE_DIM)
    cn, sn = cs(NSA_HEAD_DIM)
    cosm = _pad_cols(jnp.concatenate([cm, cm], axis=1), LANE)
    sinm = _pad_cols(jnp.concatenate([-sm, sm], axis=1), LANE)
    cosn = jnp.concatenate([cn, cn], axis=1)
    sinn = jnp.concatenate([-sn, sn], axis=1)
    return cosm, sinm, cosn, sinn


def _overlap_weights():
    ci = np.arange(N_CMP_PAD)[:, None] * CMP_STRIDE
    sj = np.arange(N_SEL)[None, :] * SEL_BLOCK
    ov = np.clip(np.minimum(ci + CMP_BLOCK, sj + SEL_BLOCK) - np.maximum(ci, sj), 0, None)
    ov[N_CMP:] = 0
    return jnp.asarray((ov.astype(np.float32) / CMP_STRIDE).T, dtype=BF16)


def kernel(x, ffn1_w_gate, ffn1_w_up, ffn1_w_down, ln1_g, ln1_b, w_in, mla_q_norm_g, mla_w_uq, mla_kv_norm_g,
           mla_w_ukv, nsa_gate_b, nsa_cmp_pe_k, nsa_cmp_w1_k, nsa_cmp_w2_k, nsa_cmp_pe_v, nsa_cmp_w1_v,
           nsa_cmp_w2_v, w_out, ln2_g, ln2_b, ffn2_w_gate, ffn2_w_up, ffn2_w_down, ln3_g, ln3_b):
    b, s, d = x.shape
    assert (b, s, d) == (1, SEQ, D_MODEL) and ffn1_w_gate.shape[0] == DEPTH
    l = 0
    row = lambda v: v.reshape(1, -1)
    xs = x.reshape(s, d)

    x1 = _ffn_ln(xs, ffn1_w_gate[l].astype(BF16), ffn1_w_up[l].astype(BF16), ffn1_w_down[l].astype(BF16),
                 row(ln1_g[l]), row(ln1_b[l]))

    win_p, gb_p = _relayout_w_in(w_in[l], nsa_gate_b[l])
    cosm, sinm, cosn, sinn = _rope_tables(s)
    (q_m, k_m, v_m, q_n, kc_in, vc_in, ks, vs, kw, vw, gates) = _proj(
        x1, win_p, row(mla_q_norm_g[l]), row(mla_kv_norm_g[l]),
        _relayout_w_uq(mla_w_uq[l]), _relayout_w_ukv(mla_w_ukv[l]), gb_p, cosm, sinm, cosn, sinn)

    def compress(t, pe, w1, w2, transpose_out):
        t16 = t.reshape(NSA_KV_HEADS, N_CMP_PAD, _CMP_HALF)
        pe8 = jnp.broadcast_to(pe.reshape(1, CMP_BLOCK * NSA_HEAD_DIM), (8, CMP_BLOCK * NSA_HEAD_DIM))
        return _compress(t16, pe8, w1, w2, transpose_out)

    kc = compress(kc_in, nsa_cmp_pe_k[l], nsa_cmp_w1_k[l], nsa_cmp_w2_k[l], False)
    vc = compress(vc_in, nsa_cmp_pe_v[l], nsa_cmp_w1_v[l], nsa_cmp_w2_v[l], True)

    o_mla = _mla(q_m, k_m, v_m)
    o_nsa = _nsa(q_n, kc, vc, _overlap_weights(), ks, vs, kw, vw, gates)

    n_mla = MLA_HEADS * MLA_V_DIM
    w_o = w_out[l].astype(BF16)
    x2 = _out_ln(o_mla, o_nsa, w_o[:n_mla], w_o[n_mla:], x1, row(ln2_g[l]), row(ln2_b[l]))

    x3 = _ffn_ln(x2, ffn2_w_gate[l].astype(BF16), ffn2_w_up[l].astype(BF16), ffn2_w_down[l].astype(BF16),
                 row(ln3_g[l]), row(ln3_b[l]))
    return x3.reshape(b, s, d)
```

```python
import functools

import jax
import jax.numpy as jnp
import numpy as np
from jax import lax
from jax.experimental import pallas as pl
from jax.experimental.pallas import tpu as pltpu

F32 = jnp.float32
BF16 = jnp.bfloat16

D_MODEL = 2048
SEQ = 16384
DEPTH = 1
DEEPNORM_ALPHA = (2.0 * DEPTH) ** 0.25
LN_EPS = 1e-5
RMS_EPS = 1e-6
ROPE_THETA = 10000.0
NEG_INF = -1e30
D_FF = 5632
FFN_RES_WEIGHT = 0.5

MLA_HEADS = 8
MLA_Q_RANK = 512
MLA_KV_RANK = 256
MLA_NOPE_DIM = 128
MLA_ROPE_DIM = 64
MLA_V_DIM = 128
MLA_QK_PAD = 256

NSA_HEADS = 8
NSA_KV_HEADS = 2
NSA_HEAD_DIM = 128
NSA_GROUP = NSA_HEADS // NSA_KV_HEADS
CMP_BLOCK = 32
CMP_STRIDE = 16
CMP_HIDDEN = 256
SEL_BLOCK = 64
SEL_TOPK = 16
WINDOW = 512
N_BRANCH = 3
FORCE_BONUS = 1e4
N_CMP = (SEQ - CMP_BLOCK) // CMP_STRIDE + 1
N_CMP_PAD = SEQ // CMP_STRIDE
N_SEL = SEQ // SEL_BLOCK

LANE = 128
MASK_BIAS = NEG_INF
TOPK_TAKEN = -2.0 ** 127
M_INIT = -1e30

VMEM_LIMIT = 56 * 1024 * 1024


def _dot(a, b, **kw):
    return jnp.dot(a, b, preferred_element_type=F32, **kw)


def _layer_norm(y, g, b):
    mu = jnp.mean(y, axis=-1, keepdims=True)
    yc = y - mu
    var = jnp.mean(yc * yc, axis=-1, keepdims=True)
    return yc * lax.rsqrt(var + LN_EPS) * g + b


def _rms_norm(x, g):
    ms = jnp.mean(x * x, axis=-1, keepdims=True)
    return x * lax.rsqrt(ms + RMS_EPS) * g


FFN_TM = 512
FFN_TF = 512


def _ffn_ln_kernel(x_ref, wg_ref, wu_ref, wd_ref, g_ref, b_ref, o_ref, xb_ref, acc_ref):
    k = pl.program_id(1)

    @pl.when(k == 0)
    def _():
        xb_ref[...] = x_ref[...].astype(BF16)
        acc_ref[...] = jnp.zeros_like(acc_ref)

    xb = xb_ref[...]
    gate = _dot(xb, wg_ref[...])
    up = _dot(xb, wu_ref[...])
    h = (jax.nn.silu(gate) * up).astype(BF16)
    acc_ref[...] += _dot(h, wd_ref[...])

    @pl.when(k == pl.num_programs(1) - 1)
    def _():
        y = DEEPNORM_ALPHA * x_ref[...] + FFN_RES_WEIGHT * acc_ref[...]
        o_ref[...] = _layer_norm(y, g_ref[...], b_ref[...])


def _ffn_ln(x, wg, wu, wd, g, b):
    s, d = x.shape
    f = wg.shape[1]
    grid = (s // FFN_TM, f // FFN_TF)
    return pl.pallas_call(
        _ffn_ln_kernel,
        grid=grid,
        in_specs=[
            pl.BlockSpec((FFN_TM, d), lambda i, k: (i, 0)),
            pl.BlockSpec((d, FFN_TF), lambda i, k: (0, k)),
            pl.BlockSpec((d, FFN_TF), lambda i, k: (0, k)),
            pl.BlockSpec((FFN_TF, d), lambda i, k: (k, 0)),
            pl.BlockSpec((1, d), lambda i, k: (0, 0)),
            pl.BlockSpec((1, d), lambda i, k: (0, 0)),
        ],
        out_specs=pl.BlockSpec((FFN_TM, d), lambda i, k: (i, 0)),
        out_shape=jax.ShapeDtypeStruct((s, d), F32),
        scratch_shapes=[pltpu.VMEM((FFN_TM, d), BF16), pltpu.VMEM((FFN_TM, d), F32)],
        compiler_params=pltpu.CompilerParams(
            dimension_semantics=("parallel", "arbitrary"), vmem_limit_bytes=VMEM_LIMIT),
        name="ffn_ln",
    )(x, wg, wu, wd, g, b)


PROJ_TM = 256
_O_CQ = 0
_O_CKV = _O_CQ + MLA_Q_RANK
_O_NQ = _O_CKV + MLA_KV_RANK
_O_KC = _O_NQ + NSA_HEADS * NSA_HEAD_DIM
_O_VC = _O_KC + NSA_KV_HEADS * NSA_HEAD_DIM
_O_KS = _O_VC + NSA_KV_HEADS * NSA_HEAD_DIM
_O_VS = _O_KS + NSA_KV_HEADS * NSA_HEAD_DIM
_O_KW = _O_VS + NSA_KV_HEADS * NSA_HEAD_DIM
_O_VW = _O_KW + NSA_KV_HEADS * NSA_HEAD_DIM
_O_KR = _O_VW + NSA_KV_HEADS * NSA_HEAD_DIM
_O_KRS = _O_KR + LANE
_O_G = (_O_KR, _O_KRS)
_PROJ_COLS = _O_KRS + LANE
GATE_ROWS = 16
GATE_LANE0 = MLA_ROPE_DIM
assert NSA_KV_HEADS == 2 and GATE_LANE0 % 8 == 0 and GATE_LANE0 + GATE_ROWS <= LANE

LOG2E = 1.4426950408889634
MLA_SCALE = (MLA_NOPE_DIM + MLA_ROPE_DIM) ** -0.5 * LOG2E
NSA_SCALE = NSA_HEAD_DIM ** -0.5 * LOG2E


def _proj_kernel(x_ref, win_ref, gq_ref, gkv_ref, wuq_ref, wukv_ref, gb_ref,
                 cosm_ref, sinm_ref, cosn_ref, sinn_ref,
                 qm_ref, km_ref, vm_ref, qn_ref, kc_ref, vc_ref, ks_ref, vs_ref, kw_ref, vw_ref, gt_ref,
                 cmp_ref):
    tm = x_ref.shape[0]
    xb = x_ref[...].astype(BF16)
    h = _dot(xb, win_ref[...])
    cosm, sinm = cosm_ref[...], sinm_ref[...]
    cosn, sinn = cosn_ref[...], sinn_ref[...]

    cqn = _rms_norm(h[:, _O_CQ:_O_CQ + MLA_Q_RANK], gq_ref[...]).astype(BF16)
    qall = _dot(cqn, wuq_ref[...])
    hw = MLA_HEADS * LANE
    for hd in range(MLA_HEADS):
        sl = slice(hd * LANE, (hd + 1) * LANE)
        qm_ref[hd, :, 0:LANE] = (qall[:, sl] * MLA_SCALE).astype(BF16)
        qr = qall[:, hw + hd * LANE: hw + (hd + 1) * LANE]
        qrs = qall[:, 2 * hw + hd * LANE: 2 * hw + (hd + 1) * LANE]
        qm_ref[hd, :, LANE:2 * LANE] = ((qr * cosm + qrs * sinm) * MLA_SCALE).astype(BF16)

    ckvn = _rms_norm(h[:, _O_CKV:_O_CKV + MLA_KV_RANK], gkv_ref[...]).astype(BF16)
    kv = _dot(ckvn, wukv_ref[...])
    kr = (h[:, _O_KR:_O_KR + LANE] * cosm + h[:, _O_KRS:_O_KRS + LANE] * sinm).astype(BF16)
    for hd in range(MLA_HEADS):
        km_ref[hd, :, 0:LANE] = kv[:, hd * LANE:(hd + 1) * LANE].astype(BF16)
        km_ref[hd, :, LANE:2 * LANE] = kr
        vm_ref[hd] = kv[:, hw + hd * LANE: hw + (hd + 1) * LANE].T.astype(BF16)

    def rope128(t):
        return t * cosn + pltpu.roll(t, NSA_HEAD_DIM // 2, 1) * sinn

    for hd in range(NSA_HEADS):
        t = h[:, _O_NQ + hd * LANE:_O_NQ + (hd + 1) * LANE]
        qn_ref[hd] = (rope128(t) * NSA_SCALE).astype(BF16)

    row0 = pl.program_id(0) * tm
    kpos = row0 + lax.broadcasted_iota(jnp.int32, (tm, LANE), 0)
    lane = lax.broadcasted_iota(jnp.int32, (tm, LANE), 1)
    onehot = jnp.where(((kpos // SEL_BLOCK) % LANE) == lane, 1.0, 0.0).astype(BF16)
    for hk in range(NSA_KV_HEADS):
        sl = lambda off: slice(off + hk * LANE, off + (hk + 1) * LANE)
        for out_ref, t in ((kc_ref, rope128(h[:, sl(_O_KC)])), (vc_ref, h[:, sl(_O_VC)])):
            cmp_ref[...] = t
            for tok in range(CMP_STRIDE):
                out_ref[hk, :, tok * LANE:(tok + 1) * LANE] = cmp_ref[pl.ds(tok, tm // CMP_STRIDE, stride=CMP_STRIDE), :]
        ks_ref[hk, :, 0:LANE] = rope128(h[:, sl(_O_KS)]).astype(BF16)
        ks_ref[hk, :, LANE:2 * LANE] = onehot
        vs_ref[hk] = h[:, sl(_O_VS)].T.astype(BF16)
        kw_ref[hk] = rope128(h[:, sl(_O_KW)]).astype(BF16)
        vw_ref[hk] = h[:, sl(_O_VW)].T.astype(BF16)
        gate = jax.nn.sigmoid(h[:, _O_G[hk]:_O_G[hk] + LANE] + gb_ref[hk])
        gt_ref[hk] = gate.T[GATE_LANE0:GATE_LANE0 + GATE_ROWS]


def _const_spec(shape):
    nd = len(shape)
    return pl.BlockSpec(shape, lambda i: (0,) * nd)


def _proj(x1, win_p, gq, gkv, wuq_all, wukv_p, gb_p, cosm, sinm, cosn, sinn):
    s, d = x1.shape
    tm = PROJ_TM
    grid = (s // tm,)
    row_spec = lambda w: pl.BlockSpec((tm, w), lambda i: (i, 0))
    head_spec = lambda nh, w: pl.BlockSpec((nh, tm, w), lambda i: (0, i, 0))
    out_shape = (
        jax.ShapeDtypeStruct((MLA_HEADS, s, MLA_QK_PAD), BF16),
        jax.ShapeDtypeStruct((MLA_HEADS, s, MLA_QK_PAD), BF16),
        jax.ShapeDtypeStruct((MLA_HEADS, MLA_V_DIM, s), BF16),
        jax.ShapeDtypeStruct((NSA_HEADS, s, NSA_HEAD_DIM), BF16),
        jax.ShapeDtypeStruct((NSA_KV_HEADS, s // CMP_STRIDE, CMP_STRIDE * NSA_HEAD_DIM), F32),
        jax.ShapeDtypeStruct((NSA_KV_HEADS, s // CMP_STRIDE, CMP_STRIDE * NSA_HEAD_DIM), F32),
        jax.ShapeDtypeStruct((NSA_KV_HEADS, s, 2 * LANE), BF16),
        jax.ShapeDtypeStruct((NSA_KV_HEADS, NSA_HEAD_DIM, s), BF16),
        jax.ShapeDtypeStruct((NSA_KV_HEADS, s, NSA_HEAD_DIM), BF16),
        jax.ShapeDtypeStruct((NSA_KV_HEADS, NSA_HEAD_DIM, s), BF16),
        jax.ShapeDtypeStruct((NSA_KV_HEADS, GATE_ROWS, s), F32),
    )
    head_spec_t = lambda nh, r: pl.BlockSpec((nh, r, tm), lambda i: (0, 0, i))
    cmp_spec = pl.BlockSpec((NSA_KV_HEADS, tm // CMP_STRIDE, CMP_STRIDE * NSA_HEAD_DIM), lambda i: (0, i, 0))
    out_specs = (
        head_spec(MLA_HEADS, MLA_QK_PAD), head_spec(MLA_HEADS, MLA_QK_PAD), head_spec_t(MLA_HEADS, MLA_V_DIM),
        head_spec(NSA_HEADS, NSA_HEAD_DIM),
        cmp_spec, cmp_spec,
        head_spec(NSA_KV_HEADS, 2 * LANE), head_spec_t(NSA_KV_HEADS, NSA_HEAD_DIM),
        head_spec(NSA_KV_HEADS, NSA_HEAD_DIM), head_spec_t(NSA_KV_HEADS, NSA_HEAD_DIM),
        head_spec_t(NSA_KV_HEADS, GATE_ROWS),
    )
    return pl.pallas_call(
        _proj_kernel,
        grid=grid,
        in_specs=[
            row_spec(d),
            _const_spec(win_p.shape), _const_spec(gq.shape), _const_spec(gkv.shape),
            _const_spec(wuq_all.shape), _const_spec(wukv_p.shape), _const_spec(gb_p.shape),
            row_spec(LANE), row_spec(LANE), row_spec(LANE), row_spec(LANE),
        ],
        out_specs=out_specs,
        out_shape=out_shape,
        scratch_shapes=[pltpu.VMEM((tm, NSA_HEAD_DIM), F32)],
        compiler_params=pltpu.CompilerParams(
            dimension_semantics=("parallel",), vmem_limit_bytes=VMEM_LIMIT),
        name="proj",
    )(x1, win_p, gq, gkv, wuq_all, wukv_p, gb_p, cosm, sinm, cosn, sinn)


_CMP_HALF = CMP_STRIDE * NSA_HEAD_DIM


def _compress_kernel(t_ref, pe_ref, w1_ref, w2_ref, o_ref, *, transpose_out):
    hi = lax.Precision.HIGHEST
    t = t_ref[0]
    w1a = w1_ref[0:_CMP_HALF, :]
    w1b = w1_ref[_CMP_HALF:2 * _CMP_HALF, :]
    first = _dot(t, w1a, precision=hi)
    second = _dot(t, w1b, precision=hi)
    pe_row = _dot(pe_ref[...], w1_ref[...], precision=hi)[0:1]
    hid = first + pltpu.roll(second, N_CMP_PAD - 1, 0) + pe_row
    out = _dot(jax.nn.gelu(hid), w2_ref[...], precision=hi)
    o_ref[0] = (out.T if transpose_out else out).astype(o_ref.dtype)


def _compress(t16, pe_flat8, w1, w2, transpose_out):
    out_dims = (NSA_HEAD_DIM, N_CMP_PAD) if transpose_out else (N_CMP_PAD, NSA_HEAD_DIM)
    return pl.pallas_call(
        functools.partial(_compress_kernel, transpose_out=transpose_out),
        grid=(NSA_KV_HEADS,),
        in_specs=[
            pl.BlockSpec((1, N_CMP_PAD, _CMP_HALF), lambda i: (i, 0, 0)),
            _const_spec(pe_flat8.shape), _const_spec(w1.shape), _const_spec(w2.shape),
        ],
        out_specs=pl.BlockSpec((1,) + out_dims, lambda i: (i, 0, 0)),
        out_shape=jax.ShapeDtypeStruct((NSA_KV_HEADS,) + out_dims, BF16),
        compiler_params=pltpu.CompilerParams(
            dimension_semantics=("parallel",), vmem_limit_bytes=VMEM_LIMIT),
        name="compress",
    )(t16, pe_flat8, w1, w2)


def _flash_init(m_ref, l_ref, acc_ref):
    m_ref[...] = jnp.full_like(m_ref, M_INIT)
    l_ref[...] = jnp.zeros_like(l_ref)
    acc_ref[...] = jnp.zeros_like(acc_ref)


def _flash_update(st, vt, m_ref, l_ref, acc_ref):
    m_prev = m_ref[...]
    m_new = jnp.maximum(m_prev, jnp.max(st, axis=0, keepdims=True))
    alpha = jnp.exp2(m_prev - m_new)
    p = jnp.exp2(st - m_new)
    l_ref[...] = alpha * l_ref[...] + jnp.sum(p, axis=0, keepdims=True)
    acc_ref[...] = alpha * acc_ref[...] + _dot(vt, p.astype(BF16))
    m_ref[...] = m_new


FLASH_PAIRS_PER_TRIP = 2


class _FlashChain:
    def __init__(self, scores, values, causal_mask, sa_ref, sb_ref, m_ref, l_ref, acc_ref):
        self.scores, self.values, self.causal_mask = scores, values, causal_mask
        self.sa, self.sb, self.m, self.l, self.acc = sa_ref, sb_ref, m_ref, l_ref, acc_ref

    def update(self, st, tile, refs=None):
        _flash_update(st, self.values(tile), *(refs or (self.m, self.l, self.acc)))

    def masked_update(self, tile, st):
        self.update(jnp.where(self.causal_mask(tile), st, MASK_BIAS), tile)


def _flash_causal(chains, n_plain, last_tile, two_diagonal_tiles):
    for c in chains:
        _flash_init(c.m, c.l, c.acc)
        c.sa[...] = c.scores(0)

    def plain_pair(t):
        for c in chains:
            c.sb[...] = c.scores(t + 1)
            c.update(c.sa[...], t)
            c.sa[...] = c.scores(t + 2)
            c.update(c.sb[...], t + 1)

    def body(j, carry):
        for u in range(FLASH_PAIRS_PER_TRIP):
            plain_pair(2 * (FLASH_PAIRS_PER_TRIP * j + u))
        return carry

    n_pairs = n_plain // 2
    n_trips = n_pairs // FLASH_PAIRS_PER_TRIP
    lax.fori_loop(0, n_trips, body, 0)
    for u in range(FLASH_PAIRS_PER_TRIP - 1):
        @pl.when(n_trips * FLASH_PAIRS_PER_TRIP + u < n_pairs)
        def _():
            plain_pair(2 * (n_trips * FLASH_PAIRS_PER_TRIP + u))
    t = 2 * n_pairs

    if two_diagonal_tiles:
        for c in chains:
            half = c.m.shape[-1] // 2
            c.masked_update(t, c.sa[...])
            st = jnp.where(c.causal_mask(t + 1, half), c.scores(t + 1, half), MASK_BIAS)
            c.update(st, t + 1, tuple(r.at[:, half:] for r in (c.m, c.l, c.acc)))
    else:
        for c in chains:
            c.masked_update(t, c.sa[...])

        @pl.when(t + 1 <= last_tile)
        def _():
            for c in chains:
                c.masked_update(t + 1, c.scores(t + 1))


def _masked_probs_t(st, mask):
    st = jnp.where(mask, st, NEG_INF)
    m = jnp.max(st, axis=0, keepdims=True)
    p = jnp.where(mask, jnp.exp2(st - m), 0.0)
    return p * (1.0 / jnp.maximum(jnp.sum(p, axis=0, keepdims=True), 1e-30))


MLA_TQ = 1024
MLA_TK = 512


MLA_HEADS_PER_STEP = 2


def _mla_kernel(q_ref, k_ref, vt_ref, o_ref, qt_ref, sa_ref, sb_ref, m_ref, l_ref, acc_ref):
    i = pl.program_id(1)
    qpos = i * MLA_TQ + lax.broadcasted_iota(jnp.int32, (1, MLA_TQ), 1)

    def causal_mask(kt, first_query=0):
        return kt * MLA_TK + lax.broadcasted_iota(jnp.int32, (MLA_TK, 1), 0) <= qpos[:, first_query:]

    def chain(h):
        qt_ref[h] = q_ref[h].astype(F32).T.astype(BF16)

        def scores(kt, first_query=0):
            k0 = pl.multiple_of(kt * MLA_TK, MLA_TK)
            return _dot(k_ref[h, pl.ds(k0, MLA_TK), :], qt_ref[h, :, first_query:])

        def values(kt):
            return vt_ref[h, :, pl.ds(pl.multiple_of(kt * MLA_TK, MLA_TK), MLA_TK)]

        return _FlashChain(scores, values, causal_mask,
                           sa_ref.at[h], sb_ref.at[h], m_ref.at[h], l_ref.at[h], acc_ref.at[h])

    tiles_per_q = MLA_TQ // MLA_TK
    _flash_causal([chain(h) for h in range(MLA_HEADS_PER_STEP)],
                  i * tiles_per_q, (i + 1) * tiles_per_q - 1, tiles_per_q == 2)
    dv = acc_ref.shape[1]
    for h in range(MLA_HEADS_PER_STEP):
        o_ref[:, h * dv:(h + 1) * dv] = (acc_ref[h] * (1.0 / l_ref[h])).T.astype(o_ref.dtype)


def _mla(q, k, vt):
    nh, s, dq = q.shape
    dv = vt.shape[1]
    hs = MLA_HEADS_PER_STEP
    assert MLA_TQ in (MLA_TK, 2 * MLA_TK) and s % MLA_TQ == 0 and nh % hs == 0
    resident = pl.Buffered(1)
    return pl.pallas_call(
        _mla_kernel,
        grid=(nh // hs, s // MLA_TQ),
        in_specs=[
            pl.BlockSpec((hs, MLA_TQ, dq), lambda h, i: (h, i, 0)),
            pl.BlockSpec((hs, s, dq), lambda h, i: (h, 0, 0), pipeline_mode=resident),
            pl.BlockSpec((hs, dv, s), lambda h, i: (h, 0, 0), pipeline_mode=resident),
        ],
        out_specs=pl.BlockSpec((MLA_TQ, hs * dv), lambda h, i: (i, h)),
        out_shape=jax.ShapeDtypeStruct((s, nh * dv), BF16),
        scratch_shapes=[pltpu.VMEM((hs, dq, MLA_TQ), BF16),
                        pltpu.VMEM((hs, MLA_TK, MLA_TQ), F32), pltpu.VMEM((hs, MLA_TK, MLA_TQ), F32),
                        pltpu.VMEM((hs, 1, MLA_TQ), F32), pltpu.VMEM((hs, 1, MLA_TQ), F32),
                        pltpu.VMEM((hs, dv, MLA_TQ), F32)],
        compiler_params=pltpu.CompilerParams(
            dimension_semantics=("parallel", "arbitrary"), vmem_limit_bytes=VMEM_LIMIT),
        name="mla",
    )(q, k, vt)


NSA_TQ = 256
NSA_ROWS = NSA_GROUP * NSA_TQ
NSA_TK = 512
_SEL_HALF_KEYS = LANE * SEL_BLOCK
_WIN_TILES = WINDOW // NSA_TQ + 1
SEL_CLASS_BLOCKS = 64


def _nsa_kernel(q_ref, kc_ref, vct_ref, ovt_ref, ks_ref, vst_ref, *rest):
    kw_refs = rest[:_WIN_TILES]
    vwt_refs = rest[_WIN_TILES:2 * _WIN_TILES]
    (wbias_ref, qoh_ref, gt_ref, o_ref,
     qaug_ref, oc_ref, sa_ref, sb_ref, m_ref, l_ref, acc_ref) = rest[2 * _WIN_TILES:]
    heads = range(NSA_KV_HEADS)

    i = pl.program_id(0)
    q0 = i * NSA_TQ
    tq = q0 + (lax.broadcasted_iota(jnp.int32, (1, NSA_ROWS), 1) & (NSA_TQ - 1))
    tq1 = q0 + lax.broadcasted_iota(jnp.int32, (1, NSA_TQ), 1)

    qts = [q_ref[hk * NSA_GROUP:(hk + 1) * NSA_GROUP].reshape(NSA_ROWS, NSA_HEAD_DIM).astype(F32).T.astype(BF16)
           for hk in heads]
    q_ohs = [jnp.concatenate([qt, qoh_ref[...]], axis=0) for qt in qts]

    o_ws = []
    for hk in heads:
        kw = jnp.concatenate([r[hk] for r in kw_refs], axis=0)
        vwt = jnp.concatenate([r[hk] for r in vwt_refs], axis=1)
        s_w = _dot(jnp.concatenate([kw, wbias_ref[0]], axis=1), q_ohs[hk])
        e_w = jnp.exp2(s_w - jnp.max(s_w, axis=0, keepdims=True))
        o_ws.append(_dot(vwt, e_w.astype(BF16)) * (1.0 / jnp.sum(e_w, axis=0, keepdims=True)))

    def compress_and_select(nb):
        nc = nb * (SEL_BLOCK // CMP_STRIDE)
        cend = lax.broadcasted_iota(jnp.int32, (nc, NSA_TQ), 0) * CMP_STRIDE + (CMP_BLOCK - 1)
        cbias = jnp.where(cend <= tq1, 0.0, NEG_INF).astype(BF16)
        bj = lax.broadcasted_iota(jnp.int32, (nb, NSA_TQ), 0)
        cur = tq1 // SEL_BLOCK
        valid = bj * SEL_BLOCK <= tq1
        bonus = FORCE_BONUS * jnp.where((bj == 0) | (bj == cur) | (bj == cur - 1), 1.0, 0.0)
        bjf = bj.astype(F32)
        ovt = ovt_ref[0:nb, 0:nc]
        for hk in heads:
            s_c = _dot(jnp.concatenate([kc_ref[hk, 0:nc, :], cbias], axis=1), q_ohs[hk])
            e_c = jnp.exp2(s_c - jnp.max(s_c, axis=0, keepdims=True))
            inv_c = jnp.where(tq >= CMP_BLOCK - 1, 1.0 / jnp.sum(e_c, axis=0, keepdims=True), 0.0)
            p_c = e_c * inv_c
            oc_ref[hk] = _dot(vct_ref[hk, :, 0:nc], p_c.astype(BF16))

            psum = p_c[:, 0:NSA_TQ]
            for g in range(1, NSA_GROUP):
                psum = psum + p_c[:, g * NSA_TQ:(g + 1) * NSA_TQ]
            p_hi = psum.astype(BF16)
            p_lo = (psum - p_hi.astype(F32)).astype(BF16)
            imp = _dot(ovt, p_hi) + _dot(ovt, p_lo)

            score = jnp.where(valid, imp + bonus, NEG_INF)
            work = score
            for _ in range(SEL_TOPK):
                mx = jnp.max(work, axis=0, keepdims=True)
                first = jnp.min(jnp.where(work == mx, bjf, float(N_SEL)), axis=0, keepdims=True)
                work = jnp.where(bjf == first, TOPK_TAKEN, work)
            selected = (work < 0.5 * TOPK_TAKEN) & (score > 0.5 * NEG_INF)
            bias = jnp.where(selected, 0.0, MASK_BIAS)
            if nb < N_SEL:
                bias = jnp.concatenate([bias, jnp.full((N_SEL - nb, NSA_TQ), MASK_BIAS, F32)], axis=0)
            bias = jnp.concatenate([bias.astype(BF16)] * NSA_GROUP, axis=1)
            for half in range(N_SEL // LANE):
                qaug_ref[hk, half, 0:NSA_HEAD_DIM, :] = qts[hk]
                qaug_ref[hk, half, NSA_HEAD_DIM:NSA_HEAD_DIM + LANE, :] = bias[half * LANE:(half + 1) * LANE, :]

    size_class = (q0 + NSA_TQ - 1) // (SEL_BLOCK * SEL_CLASS_BLOCKS)
    for c in range(N_SEL // SEL_CLASS_BLOCKS):
        pl.when(size_class == c)(functools.partial(compress_and_select, (c + 1) * SEL_CLASS_BLOCKS))

    last = (q0 + NSA_TQ - 1) // NSA_TK

    def causal_mask(kt):
        return kt * NSA_TK + lax.broadcasted_iota(jnp.int32, (NSA_TK, 1), 0) <= tq

    def chain(hk):
        def scores(kt):
            k0 = pl.multiple_of(kt * NSA_TK, NSA_TK)
            qa = qaug_ref[hk, kt // (_SEL_HALF_KEYS // NSA_TK)]
            return _dot(ks_ref[hk, pl.ds(k0, NSA_TK), :], qa)

        def values(kt):
            return vst_ref[hk, :, pl.ds(pl.multiple_of(kt * NSA_TK, NSA_TK), NSA_TK)]

        return _FlashChain(scores, values, causal_mask,
                           sa_ref.at[hk], sb_ref.at[hk], m_ref.at[hk], l_ref.at[hk], acc_ref.at[hk])

    _flash_causal([chain(hk) for hk in heads], last, last, False)

    for hk in heads:
        o_s = acc_ref[hk] * (1.0 / l_ref[hk])
        gt = gt_ref[hk]
        gate = lambda br: jnp.concatenate(
            [gt[br * NSA_GROUP + g:br * NSA_GROUP + g + 1, :] for g in range(NSA_GROUP)], axis=1)
        out = (gate(0) * oc_ref[hk] + gate(1) * o_s + gate(2) * o_ws[hk]).T
        for g in range(NSA_GROUP):
            col = (hk * NSA_GROUP + g) * NSA_HEAD_DIM
            o_ref[:, col:col + NSA_HEAD_DIM] = out[g * NSA_TQ:(g + 1) * NSA_TQ].astype(o_ref.dtype)


def _nsa(q, kc, vct, ovt, ks, vst, kw, vwt, gates):
    s = q.shape[1]
    d = NSA_HEAD_DIM
    hk = NSA_KV_HEADS
    assert NSA_TQ <= NSA_TK and NSA_TK % NSA_TQ == 0

    def win_specs(transposed):
        def spec(j):
            back = _WIN_TILES - 1 - j
            if transposed:
                return pl.BlockSpec((hk, d, NSA_TQ), lambda i: (0, 0, jnp.maximum(i - back, 0)))
            return pl.BlockSpec((hk, NSA_TQ, d), lambda i: (0, jnp.maximum(i - back, 0), 0))
        return [spec(j) for j in range(_WIN_TILES)]

    resident = pl.Buffered(1)
    return pl.pallas_call(
        _nsa_kernel,
        grid=(s // NSA_TQ,),
        in_specs=[
            pl.BlockSpec((NSA_HEADS, NSA_TQ, d), lambda i: (0, i, 0)),
            pl.BlockSpec((hk, N_CMP_PAD, d), lambda i: (0, 0, 0)),
            pl.BlockSpec((hk, d, N_CMP_PAD), lambda i: (0, 0, 0)),
            pl.BlockSpec((N_SEL, N_CMP_PAD), lambda i: (0, 0)),
            pl.BlockSpec((hk, s, 2 * LANE), lambda i: (0, 0, 0), pipeline_mode=resident),
            pl.BlockSpec((hk, d, s), lambda i: (0, 0, 0), pipeline_mode=resident),
            *win_specs(False), *win_specs(True),
            pl.BlockSpec((1, WINDOW + NSA_TQ, NSA_TQ), lambda i: (jnp.minimum(i, _WIN_TILES - 1), 0, 0)),
            pl.BlockSpec((NSA_TQ, NSA_ROWS), lambda i: (0, 0)),
            pl.BlockSpec((hk, GATE_ROWS, NSA_TQ), lambda i: (0, 0, i)),
        ],
        out_specs=pl.BlockSpec((NSA_TQ, NSA_HEADS * d), lambda i: (i, 0)),
        out_shape=jax.ShapeDtypeStruct((s, NSA_HEADS * d), BF16),
        scratch_shapes=[pltpu.VMEM((hk, N_SEL // LANE, NSA_HEAD_DIM + LANE, NSA_ROWS), BF16),
                        pltpu.VMEM((hk, d, NSA_ROWS), F32),
                        pltpu.VMEM((hk, NSA_TK, NSA_ROWS), F32), pltpu.VMEM((hk, NSA_TK, NSA_ROWS), F32),
                        pltpu.VMEM((hk, 1, NSA_ROWS), F32), pltpu.VMEM((hk, 1, NSA_ROWS), F32),
                        pltpu.VMEM((hk, d, NSA_ROWS), F32)],
        compiler_params=pltpu.CompilerParams(
            dimension_semantics=("arbitrary",), vmem_limit_bytes=VMEM_LIMIT),
        name="nsa",
    )(q, kc, vct, ovt, ks, vst, *([kw] * _WIN_TILES), *([vwt] * _WIN_TILES), _window_bias(), _query_onehot(), gates)


def _window_bias():
    r = np.arange(WINDOW + NSA_TQ)[None, :, None]
    c = np.arange(NSA_TQ)[None, None, :]
    q0 = (np.arange(_WIN_TILES) * NSA_TQ)[:, None, None]
    back = WINDOW + c - r
    ok = (back >= 0) & (back < WINDOW) & ((q0 - WINDOW + r >= 0) | (q0 >= WINDOW))
    return jnp.asarray(np.where(ok, 0.0, NEG_INF), dtype=BF16)


def _query_onehot():
    return jnp.asarray(np.tile(np.eye(NSA_TQ, dtype=np.float32), (1, NSA_GROUP)), dtype=BF16)


OUT_TM = 512


def _out_ln_kernel(om_ref, on_ref, wa_ref, wb_ref, x_ref, g_ref, b_ref, o_ref):
    mix = _dot(om_ref[...], wa_ref[...]) + _dot(on_ref[...], wb_ref[...])
    o_ref[...] = _layer_norm(DEEPNORM_ALPHA * x_ref[...] + mix, g_ref[...], b_ref[...])


def _out_ln(o_mla, o_nsa, wa, wb, x1, g, b):
    s, d = x1.shape
    tm = OUT_TM
    return pl.pallas_call(
        _out_ln_kernel,
        grid=(s // tm,),
        in_specs=[
            pl.BlockSpec((tm, o_mla.shape[1]), lambda i: (i, 0)),
            pl.BlockSpec((tm, o_nsa.shape[1]), lambda i: (i, 0)),
            _const_spec(wa.shape), _const_spec(wb.shape),
            pl.BlockSpec((tm, d), lambda i: (i, 0)),
            _const_spec(g.shape), _const_spec(b.shape),
        ],
        out_specs=pl.BlockSpec((tm, d), lambda i: (i, 0)),
        out_shape=jax.ShapeDtypeStruct((s, d), F32),
        compiler_params=pltpu.CompilerParams(
            dimension_semantics=("parallel",), vmem_limit_bytes=VMEM_LIMIT),
        name="out_ln",
    )(o_mla, o_nsa, wa, wb, x1, g, b)


def _pad_cols(w, n):
    return jnp.pad(w, ((0, 0), (0, n - w.shape[1])))


def _relayout_w_in(w_in, gate_b):
    sizes = (MLA_Q_RANK, MLA_KV_RANK, MLA_ROPE_DIM, NSA_HEADS * NSA_HEAD_DIM) + (NSA_KV_HEADS * NSA_HEAD_DIM,) * 6 \
        + (N_BRANCH * NSA_HEADS,)
    offs = [int(v) for v in np.cumsum(sizes)[:-1]]
    c_q, c_kv, k_r, q_n, k_c, v_c, k_s, v_s, k_w, v_w, g = jnp.split(w_in, offs, axis=1)
    half = MLA_ROPE_DIM // 2
    k_r_sw = jnp.concatenate([k_r[:, half:], k_r[:, :half]], axis=1)
    gate_idx = np.array([[br * NSA_HEADS + hk * NSA_GROUP + gi for br in range(N_BRANCH) for gi in range(NSA_GROUP)]
                         for hk in range(NSA_KV_HEADS)])
    n_gate = gate_idx.shape[1]
    chunk = lambda rope, hk: _pad_cols(jnp.concatenate([rope, g[:, gate_idx[hk]]], axis=1), LANE)
    w = jnp.concatenate([c_q, c_kv, q_n, k_c, v_c, k_s, v_s, k_w, v_w, chunk(k_r, 0), chunk(k_r_sw, 1)], axis=1)
    assert w.shape[1] == _PROJ_COLS and k_r.shape[1] == GATE_LANE0
    gb = jnp.stack([jnp.pad(gate_b[gate_idx[hk]], (GATE_LANE0, LANE - GATE_LANE0 - n_gate))
                    for hk in range(NSA_KV_HEADS)])
    return w.astype(BF16), gb.reshape(NSA_KV_HEADS, 1, LANE)


def _relayout_w_uq(w_uq):
    dh = MLA_NOPE_DIM + MLA_ROPE_DIM
    half = MLA_ROPE_DIM // 2
    w = w_uq.reshape(MLA_Q_RANK, MLA_HEADS, dh)
    nope = w[:, :, :MLA_NOPE_DIM]
    r = w[:, :, MLA_NOPE_DIM:]
    r_sw = jnp.concatenate([r[:, :, half:], r[:, :, :half]], axis=2)
    pad = lambda t: jnp.pad(t, ((0, 0), (0, 0), (0, LANE - MLA_ROPE_DIM)))
    parts = [nope, pad(r), pad(r_sw)]
    return jnp.concatenate([p.reshape(MLA_Q_RANK, MLA_HEADS * LANE) for p in parts], axis=1).astype(BF16)


def _relayout_w_ukv(w_ukv):
    w = w_ukv.reshape(MLA_KV_RANK, MLA_HEADS, MLA_NOPE_DIM + MLA_V_DIM)
    k = w[:, :, :MLA_NOPE_DIM].reshape(MLA_KV_RANK, MLA_HEADS * MLA_NOPE_DIM)
    v = w[:, :, MLA_NOPE_DIM:].reshape(MLA_KV_RANK, MLA_HEADS * MLA_V_DIM)
    return jnp.concatenate([k, v], axis=1).astype(BF16)


def _rope_tables(s):
    pos = jnp.arange(s).astype(F32)[:, None]

    def cs(d):
        half = d // 2
        inv = ROPE_THETA ** (-jnp.arange(half, dtype=F32) * (2.0 / d))
        ang = pos * inv[None, :]
        return jnp.cos(ang), jnp.sin(ang)

    cm, sm = cs(MLA_ROPE_DIM)
    cn, sn = cs(NSA_HEAD_DIM)
    cosm = _pad_cols(jnp.concatenate([cm, cm], axis=1), LANE)
    sinm = _pad_cols(jnp.concatenate([-sm, sm], axis=1), LANE)
    cosn = jnp.concatenate([cn, cn], axis=1)
    sinn = jnp.concatenate([-sn, sn], axis=1)
    return cosm, sinm, cosn, sinn


def _overlap_weights():
    ci = np.arange(N_CMP_PAD)[:, None] * CMP_STRIDE
    sj = np.arange(N_SEL)[None, :] * SEL_BLOCK
    ov = np.clip(np.minimum(ci + CMP_BLOCK, sj + SEL_BLOCK) - np.maximum(ci, sj), 0, None)
    ov[N_CMP:] = 0
    return jnp.asarray((ov.astype(np.float32) / CMP_STRIDE).T, dtype=BF16)


def kernel(x, ffn1_w_gate, ffn1_w_up, ffn1_w_down, ln1_g, ln1_b, w_in, mla_q_norm_g, mla_w_uq, mla_kv_norm_g,
           mla_w_ukv, nsa_gate_b, nsa_cmp_pe_k, nsa_cmp_w1_k, nsa_cmp_w2_k, nsa_cmp_pe_v, nsa_cmp_w1_v,
           nsa_cmp_w2_v, w_out, ln2_g, ln2_b, ffn2_w_gate, ffn2_w_up, ffn2_w_down, ln3_g, ln3_b):
    b, s, d = x.shape
    assert (b, s, d) == (1, SEQ, D_MODEL) and ffn1_w_gate.shape[0] == DEPTH
    l = 0
    row = lambda v: v.reshape(1, -1)
    xs = x.reshape(s, d)

    x1 = _ffn_ln(xs, ffn1_w_gate[l].astype(BF16), ffn1_w_up[l].astype(BF16), ffn1_w_down[l].astype(BF16),
                 row(ln1_g[l]), row(ln1_b[l]))

    win_p, gb_p = _relayout_w_in(w_in[l], nsa_gate_b[l])
    cosm, sinm, cosn, sinn = _rope_tables(s)
    (q_m, k_m, v_m, q_n, kc_in, vc_in, ks, vs, kw, vw, gates) = _proj(
        x1, win_p, row(mla_q_norm_g[l]), row(mla_kv_norm_g[l]),
        _relayout_w_uq(mla_w_uq[l]), _relayout_w_ukv(mla_w_ukv[l]), gb_p, cosm, sinm, cosn, sinn)

    def compress(t16, pe, w1, w2, transpose_out):
        pe8 = jnp.broadcast_to(pe.reshape(1, CMP_BLOCK * NSA_HEAD_DIM), (8, CMP_BLOCK * NSA_HEAD_DIM))
        return _compress(t16, pe8, w1, w2, transpose_out)

    kc = compress(kc_in, nsa_cmp_pe_k[l], nsa_cmp_w1_k[l], nsa_cmp_w2_k[l], False)
    vc = compress(vc_in, nsa_cmp_pe_v[l], nsa_cmp_w1_v[l], nsa_cmp_w2_v[l], True)

    o_mla = _mla(q_m, k_m, v_m)
    o_nsa = _nsa(q_n, kc, vc, _overlap_weights(), ks, vs, kw, vw, gates)

    n_mla = MLA_HEADS * MLA_V_DIM
    w_o = w_out[l].astype(BF16)
    x2 = _out_ln(o_mla, o_nsa, w_o[:n_mla], w_o[n_mla:], x1, row(ln2_g[l]), row(ln2_b[l]))

    x3 = _ffn_ln(x2, ffn2_w_gate[l].astype(BF16), ffn2_w_up[l].astype(BF16), ffn2_w_down[l].astype(BF16),
                 row(ln3_g[l]), row(ln3_b[l]))
    return x3.reshape(b, s, d)
```

```python
import functools

import jax
import jax.numpy as jnp
import numpy as np
from jax import lax
from jax.experimental import pallas as pl
from jax.experimental.pallas import tpu as pltpu

F32 = jnp.float32
BF16 = jnp.bfloat16

D_MODEL = 2048
SEQ = 16384
DEPTH = 1
DEEPNORM_ALPHA = (2.0 * DEPTH) ** 0.25
LN_EPS = 1e-5
RMS_EPS = 1e-6
ROPE_THETA = 10000.0
NEG_INF = -1e30
FFN_RES_WEIGHT = 0.5

MLA_HEADS = 8
MLA_Q_RANK = 512
MLA_KV_RANK = 256
MLA_NOPE_DIM = 128
MLA_ROPE_DIM = 64
MLA_V_DIM = 128
MLA_QK_PAD = 256

NSA_HEADS = 8
NSA_KV_HEADS = 2
NSA_HEAD_DIM = 128
NSA_GROUP = NSA_HEADS // NSA_KV_HEADS
CMP_BLOCK = 32
CMP_STRIDE = 16
SEL_BLOCK = 64
SEL_TOPK = 16
WINDOW = 512
N_BRANCH = 3
FORCE_BONUS = 1e4
N_CMP = (SEQ - CMP_BLOCK) // CMP_STRIDE + 1
N_CMP_PAD = SEQ // CMP_STRIDE
N_SEL = SEQ // SEL_BLOCK

LANE = 128
MASK_BIAS = NEG_INF
TOPK_TAKEN = -2.0 ** 127
M_INIT = -1e30

VMEM_LIMIT = 56 * 1024 * 1024


def _dot(a, b, **kw):
    return jnp.dot(a, b, preferred_element_type=F32, **kw)


def _layer_norm(y, g, b):
    mu = jnp.mean(y, axis=-1, keepdims=True)
    yc = y - mu
    var = jnp.mean(yc * yc, axis=-1, keepdims=True)
    return yc * lax.rsqrt(var + LN_EPS) * g + b


def _rms_norm(x, g):
    ms = jnp.mean(x * x, axis=-1, keepdims=True)
    return x * lax.rsqrt(ms + RMS_EPS) * g


FFN_TM = 512
FFN_TF = 512


def _ffn_ln_kernel(x_ref, wg_ref, wu_ref, wd_ref, g_ref, b_ref, o_ref, xb_ref, acc_ref):
    k = pl.program_id(1)

    @pl.when(k == 0)
    def _():
        xb_ref[...] = x_ref[...].astype(BF16)
        acc_ref[...] = jnp.zeros_like(acc_ref)

    xb = xb_ref[...]
    gate = _dot(xb, wg_ref[...])
    up = _dot(xb, wu_ref[...])
    h = (jax.nn.silu(gate) * up).astype(BF16)
    acc_ref[...] += _dot(h, wd_ref[...])

    @pl.when(k == pl.num_programs(1) - 1)
    def _():
        y = DEEPNORM_ALPHA * x_ref[...] + FFN_RES_WEIGHT * acc_ref[...]
        o_ref[...] = _layer_norm(y, g_ref[...], b_ref[...])


def _ffn_ln(x, wg, wu, wd, g, b):
    s, d = x.shape
    f = wg.shape[1]
    grid = (s // FFN_TM, f // FFN_TF)
    return pl.pallas_call(
        _ffn_ln_kernel,
        grid=grid,
        in_specs=[
            pl.BlockSpec((FFN_TM, d), lambda i, k: (i, 0)),
            pl.BlockSpec((d, FFN_TF), lambda i, k: (0, k)),
            pl.BlockSpec((d, FFN_TF), lambda i, k: (0, k)),
            pl.BlockSpec((FFN_TF, d), lambda i, k: (k, 0)),
            pl.BlockSpec((1, d), lambda i, k: (0, 0)),
            pl.BlockSpec((1, d), lambda i, k: (0, 0)),
        ],
        out_specs=pl.BlockSpec((FFN_TM, d), lambda i, k: (i, 0)),
        out_shape=jax.ShapeDtypeStruct((s, d), F32),
        scratch_shapes=[pltpu.VMEM((FFN_TM, d), BF16), pltpu.VMEM((FFN_TM, d), F32)],
        compiler_params=pltpu.CompilerParams(
            dimension_semantics=("parallel", "arbitrary"), vmem_limit_bytes=VMEM_LIMIT),
        name="ffn_ln",
    )(x, wg, wu, wd, g, b)


PROJ_TM = 256
_O_CQ = 0
_O_CKV = _O_CQ + MLA_Q_RANK
_O_NQ = _O_CKV + MLA_KV_RANK
_O_KC = _O_NQ + NSA_HEADS * NSA_HEAD_DIM
_O_VC = _O_KC + NSA_KV_HEADS * NSA_HEAD_DIM
_O_KS = _O_VC + NSA_KV_HEADS * NSA_HEAD_DIM
_O_VS = _O_KS + NSA_KV_HEADS * NSA_HEAD_DIM
_O_KW = _O_VS + NSA_KV_HEADS * NSA_HEAD_DIM
_O_VW = _O_KW + NSA_KV_HEADS * NSA_HEAD_DIM
_O_KR = _O_VW + NSA_KV_HEADS * NSA_HEAD_DIM
_O_KRS = _O_KR + LANE
_O_G = (_O_KR, _O_KRS)
_PROJ_COLS = _O_KRS + LANE
GATE_ROWS = 16
GATE_LANE0 = MLA_ROPE_DIM
assert NSA_KV_HEADS == 2 and GATE_LANE0 % 8 == 0 and GATE_LANE0 + GATE_ROWS <= LANE

LOG2E = 1.4426950408889634
MLA_SCALE = (MLA_NOPE_DIM + MLA_ROPE_DIM) ** -0.5 * LOG2E
NSA_SCALE = NSA_HEAD_DIM ** -0.5 * LOG2E


def _proj_kernel(x_ref, win_ref, gq_ref, gkv_ref, wuq_ref, wukv_ref, gb_ref,
                 cosm_ref, sinm_ref, cosn_ref, sinn_ref,
                 qm_ref, km_ref, vm_ref, qn_ref, kc_ref, vc_ref, ks_ref, vs_ref, kw_ref, vw_ref, gt_ref,
                 cmp_ref):
    tm = x_ref.shape[0]
    xb = x_ref[...].astype(BF16)
    h = _dot(xb, win_ref[...])
    cosm, sinm = cosm_ref[...], sinm_ref[...]
    cosn, sinn = cosn_ref[...], sinn_ref[...]

    cqn = _rms_norm(h[:, _O_CQ:_O_CQ + MLA_Q_RANK], gq_ref[...]).astype(BF16)
    qall = _dot(cqn, wuq_ref[...])
    hw = MLA_HEADS * LANE
    for hd in range(MLA_HEADS):
        sl = slice(hd * LANE, (hd + 1) * LANE)
        qm_ref[hd, :, 0:LANE] = (qall[:, sl] * MLA_SCALE).astype(BF16)
        qr = qall[:, hw + hd * LANE: hw + (hd + 1) * LANE]
        qrs = qall[:, 2 * hw + hd * LANE: 2 * hw + (hd + 1) * LANE]
        qm_ref[hd, :, LANE:2 * LANE] = ((qr * cosm + qrs * sinm) * MLA_SCALE).astype(BF16)

    ckvn = _rms_norm(h[:, _O_CKV:_O_CKV + MLA_KV_RANK], gkv_ref[...]).astype(BF16)
    kv = _dot(ckvn, wukv_ref[...])
    kr = (h[:, _O_KR:_O_KR + LANE] * cosm + h[:, _O_KRS:_O_KRS + LANE] * sinm).astype(BF16)
    for hd in range(MLA_HEADS):
        km_ref[hd, :, 0:LANE] = kv[:, hd * LANE:(hd + 1) * LANE].astype(BF16)
        km_ref[hd, :, LANE:2 * LANE] = kr
        vm_ref[hd] = kv[:, hw + hd * LANE: hw + (hd + 1) * LANE].T.astype(BF16)

    def rope128(t):
        return t * cosn + pltpu.roll(t, NSA_HEAD_DIM // 2, 1) * sinn

    for hd in range(NSA_HEADS):
        t = h[:, _O_NQ + hd * LANE:_O_NQ + (hd + 1) * LANE]
        qn_ref[hd] = (rope128(t) * NSA_SCALE).astype(BF16)

    row0 = pl.program_id(0) * tm
    kpos = row0 + lax.broadcasted_iota(jnp.int32, (tm, LANE), 0)
    lane = lax.broadcasted_iota(jnp.int32, (tm, LANE), 1)
    onehot = jnp.where(((kpos // SEL_BLOCK) % LANE) == lane, 1.0, 0.0).astype(BF16)
    for hk in range(NSA_KV_HEADS):
        sl = lambda off: slice(off + hk * LANE, off + (hk + 1) * LANE)
        for out_ref, t in ((kc_ref, rope128(h[:, sl(_O_KC)])), (vc_ref, h[:, sl(_O_VC)])):
            cmp_ref[...] = t
            for tok in range(CMP_STRIDE):
                out_ref[hk, :, tok * LANE:(tok + 1) * LANE] = cmp_ref[pl.ds(tok, tm // CMP_STRIDE, stride=CMP_STRIDE), :]
        ks_ref[hk, :, 0:LANE] = rope128(h[:, sl(_O_KS)]).astype(BF16)
        ks_ref[hk, :, LANE:2 * LANE] = onehot
        vs_ref[hk] = h[:, sl(_O_VS)].T.astype(BF16)
        kw_ref[hk] = rope128(h[:, sl(_O_KW)]).astype(BF16)
        vw_ref[hk] = h[:, sl(_O_VW)].T.astype(BF16)
        gate = jax.nn.sigmoid(h[:, _O_G[hk]:_O_G[hk] + LANE] + gb_ref[hk])
        gt_ref[hk] = gate.T[GATE_LANE0:GATE_LANE0 + GATE_ROWS]


def _const_spec(shape):
    nd = len(shape)
    return pl.BlockSpec(shape, lambda i: (0,) * nd)


def _proj(x1, win_p, gq, gkv, wuq_all, wukv_p, gb_p, cosm, sinm, cosn, sinn):
    s, d = x1.shape
    tm = PROJ_TM
    grid = (s // tm,)
    row_spec = lambda w: pl.BlockSpec((tm, w), lambda i: (i, 0))
    head_spec = lambda nh, w: pl.BlockSpec((nh, tm, w), lambda i: (0, i, 0))
    out_shape = (
        jax.ShapeDtypeStruct((MLA_HEADS, s, MLA_QK_PAD), BF16),
        jax.ShapeDtypeStruct((MLA_HEADS, s, MLA_QK_PAD), BF16),
        jax.ShapeDtypeStruct((MLA_HEADS, MLA_V_DIM, s), BF16),
        jax.ShapeDtypeStruct((NSA_HEADS, s, NSA_HEAD_DIM), BF16),
        jax.ShapeDtypeStruct((NSA_KV_HEADS, s // CMP_STRIDE, CMP_STRIDE * NSA_HEAD_DIM), F32),
        jax.ShapeDtypeStruct((NSA_KV_HEADS, s // CMP_STRIDE, CMP_STRIDE * NSA_HEAD_DIM), F32),
        jax.ShapeDtypeStruct((NSA_KV_HEADS, s, 2 * LANE), BF16),
        jax.ShapeDtypeStruct((NSA_KV_HEADS, NSA_HEAD_DIM, s), BF16),
        jax.ShapeDtypeStruct((NSA_KV_HEADS, s, NSA_HEAD_DIM), BF16),
        jax.ShapeDtypeStruct((NSA_KV_HEADS, NSA_HEAD_DIM, s), BF16),
        jax.ShapeDtypeStruct((NSA_KV_HEADS, GATE_ROWS, s), F32),
    )
    head_spec_t = lambda nh, r: pl.BlockSpec((nh, r, tm), lambda i: (0, 0, i))
    cmp_spec = pl.BlockSpec((NSA_KV_HEADS, tm // CMP_STRIDE, CMP_STRIDE * NSA_HEAD_DIM), lambda i: (0, i, 0))
    out_specs = (
        head_spec(MLA_HEADS, MLA_QK_PAD), head_spec(MLA_HEADS, MLA_QK_PAD), head_spec_t(MLA_HEADS, MLA_V_DIM),
        head_spec(NSA_HEADS, NSA_HEAD_DIM),
        cmp_spec, cmp_spec,
        head_spec(NSA_KV_HEADS, 2 * LANE), head_spec_t(NSA_KV_HEADS, NSA_HEAD_DIM),
        head_spec(NSA_KV_HEADS, NSA_HEAD_DIM), head_spec_t(NSA_KV_HEADS, NSA_HEAD_DIM),
        head_spec_t(NSA_KV_HEADS, GATE_ROWS),
    )
    return pl.pallas_call(
        _proj_kernel,
        grid=grid,
        in_specs=[
            row_spec(d),
            _const_spec(win_p.shape), _const_spec(gq.shape), _const_spec(gkv.shape),
            _const_spec(wuq_all.shape), _const_spec(wukv_p.shape), _const_spec(gb_p.shape),
            row_spec(LANE), row_spec(LANE), row_spec(LANE), row_spec(LANE),
        ],
        out_specs=out_specs,
        out_shape=out_shape,
        scratch_shapes=[pltpu.VMEM((tm, NSA_HEAD_DIM), F32)],
        compiler_params=pltpu.CompilerParams(
            dimension_semantics=("parallel",), vmem_limit_bytes=VMEM_LIMIT),
        name="proj",
    )(x1, win_p, gq, gkv, wuq_all, wukv_p, gb_p, cosm, sinm, cosn, sinn)


_CMP_HALF = CMP_STRIDE * NSA_HEAD_DIM


def _compress_kernel(t_ref, pe_ref, w1_ref, w2_ref, o_ref, *, transpose_out):
    hi = lax.Precision.HIGHEST
    t = t_ref[0]
    w1a = w1_ref[0:_CMP_HALF, :]
    w1b = w1_ref[_CMP_HALF:2 * _CMP_HALF, :]
    first = _dot(t, w1a, precision=hi)
    second = _dot(t, w1b, precision=hi)
    pe_row = _dot(pe_ref[...], w1_ref[...], precision=hi)[0:1]
    hid = first + pltpu.roll(second, N_CMP_PAD - 1, 0) + pe_row
    out = _dot(jax.nn.gelu(hid), w2_ref[...], precision=hi)
    o_ref[0] = (out.T if transpose_out else out).astype(o_ref.dtype)


def _compress(t16, pe_flat8, w1, w2, transpose_out):
    out_dims = (NSA_HEAD_DIM, N_CMP_PAD) if transpose_out else (N_CMP_PAD, NSA_HEAD_DIM)
    return pl.pallas_call(
        functools.partial(_compress_kernel, transpose_out=transpose_out),
        grid=(NSA_KV_HEADS,),
        in_specs=[
            pl.BlockSpec((1, N_CMP_PAD, _CMP_HALF), lambda i: (i, 0, 0)),
            _const_spec(pe_flat8.shape), _const_spec(w1.shape), _const_spec(w2.shape),
        ],
        out_specs=pl.BlockSpec((1,) + out_dims, lambda i: (i, 0, 0)),
        out_shape=jax.ShapeDtypeStruct((NSA_KV_HEADS,) + out_dims, BF16),
        compiler_params=pltpu.CompilerParams(
            dimension_semantics=("parallel",), vmem_limit_bytes=VMEM_LIMIT),
        name="compress",
    )(t16, pe_flat8, w1, w2)


def _flash_init(m_ref, l_ref, acc_ref):
    m_ref[...] = jnp.full_like(m_ref, M_INIT)
    l_ref[...] = jnp.zeros_like(l_ref)
    acc_ref[...] = jnp.zeros_like(acc_ref)


def _flash_update(st, vt, m_ref, l_ref, acc_ref):
    m_prev = m_ref[...]
    m_new = jnp.maximum(m_prev, jnp.max(st, axis=0, keepdims=True))
    alpha = jnp.exp2(m_prev - m_new)
    p = jnp.exp2(st - m_new)
    l_ref[...] = alpha * l_ref[...] + jnp.sum(p, axis=0, keepdims=True)
    acc_ref[...] = alpha * acc_ref[...] + _dot(vt, p.astype(BF16))
    m_ref[...] = m_new


FLASH_PAIRS_PER_TRIP = 2


class _FlashChain:
    def __init__(self, scores, values, causal_mask, sa_ref, sb_ref, m_ref, l_ref, acc_ref):
        self.scores, self.values, self.causal_mask = scores, values, causal_mask
        self.sa, self.sb, self.m, self.l, self.acc = sa_ref, sb_ref, m_ref, l_ref, acc_ref

    def update(self, st, tile, refs=None):
        _flash_update(st, self.values(tile), *(refs or (self.m, self.l, self.acc)))

    def masked_update(self, tile, st):
        self.update(jnp.where(self.causal_mask(tile), st, MASK_BIAS), tile)


def _flash_causal(chains, n_plain, last_tile, two_diagonal_tiles):
    for c in chains:
        _flash_init(c.m, c.l, c.acc)
        c.sa[...] = c.scores(0)

    def plain_pair(t):
        for c in chains:
            c.sb[...] = c.scores(t + 1)
            c.update(c.sa[...], t)
            c.sa[...] = c.scores(t + 2)
            c.update(c.sb[...], t + 1)

    def body(j, carry):
        for u in range(FLASH_PAIRS_PER_TRIP):
            plain_pair(2 * (FLASH_PAIRS_PER_TRIP * j + u))
        return carry

    n_pairs = n_plain // 2
    n_trips = n_pairs // FLASH_PAIRS_PER_TRIP
    lax.fori_loop(0, n_trips, body, 0)
    for u in range(FLASH_PAIRS_PER_TRIP - 1):
        @pl.when(n_trips * FLASH_PAIRS_PER_TRIP + u < n_pairs)
        def _():
            plain_pair(2 * (n_trips * FLASH_PAIRS_PER_TRIP + u))
    t = 2 * n_pairs

    if two_diagonal_tiles:
        for c in chains:
            half = c.m.shape[-1] // 2
            c.masked_update(t, c.sa[...])
            st = jnp.where(c.causal_mask(t + 1, half), c.scores(t + 1, half), MASK_BIAS)
            c.update(st, t + 1, tuple(r.at[:, half:] for r in (c.m, c.l, c.acc)))
    else:
        for c in chains:
            c.masked_update(t, c.sa[...])

        @pl.when(t + 1 <= last_tile)
        def _():
            for c in chains:
                c.masked_update(t + 1, c.scores(t + 1))


MLA_TQ = 1024
MLA_TK = 512


MLA_HEADS_PER_STEP = 2


def _mla_kernel(q_ref, k_ref, vt_ref, o_ref, qt_ref, sa_ref, sb_ref, m_ref, l_ref, acc_ref):
    i = pl.program_id(1)
    qpos = i * MLA_TQ + lax.broadcasted_iota(jnp.int32, (1, MLA_TQ), 1)

    def causal_mask(kt, first_query=0):
        return kt * MLA_TK + lax.broadcasted_iota(jnp.int32, (MLA_TK, 1), 0) <= qpos[:, first_query:]

    def chain(h):
        qt_ref[h] = q_ref[h].astype(F32).T.astype(BF16)

        def scores(kt, first_query=0):
            k0 = pl.multiple_of(kt * MLA_TK, MLA_TK)
            return _dot(k_ref[h, pl.ds(k0, MLA_TK), :], qt_ref[h, :, first_query:])

        def values(kt):
            return vt_ref[h, :, pl.ds(pl.multiple_of(kt * MLA_TK, MLA_TK), MLA_TK)]

        return _FlashChain(scores, values, causal_mask,
                           sa_ref.at[h], sb_ref.at[h], m_ref.at[h], l_ref.at[h], acc_ref.at[h])

    tiles_per_q = MLA_TQ // MLA_TK
    _flash_causal([chain(h) for h in range(MLA_HEADS_PER_STEP)],
                  i * tiles_per_q, (i + 1) * tiles_per_q - 1, tiles_per_q == 2)
    dv = acc_ref.shape[1]
    for h in range(MLA_HEADS_PER_STEP):
        o_ref[:, h * dv:(h + 1) * dv] = (acc_ref[h] * (1.0 / l_ref[h])).T.astype(o_ref.dtype)


def _mla(q, k, vt):
    nh, s, dq = q.shape
    dv = vt.shape[1]
    hs = MLA_HEADS_PER_STEP
    assert MLA_TQ in (MLA_TK, 2 * MLA_TK) and s % MLA_TQ == 0 and nh % hs == 0
    resident = pl.Buffered(1)
    return pl.pallas_call(
        _mla_kernel,
        grid=(nh // hs, s // MLA_TQ),
        in_specs=[
            pl.BlockSpec((hs, MLA_TQ, dq), lambda h, i: (h, i, 0)),
            pl.BlockSpec((hs, s, dq), lambda h, i: (h, 0, 0), pipeline_mode=resident),
            pl.BlockSpec((hs, dv, s), lambda h, i: (h, 0, 0), pipeline_mode=resident),
        ],
        out_specs=pl.BlockSpec((MLA_TQ, hs * dv), lambda h, i: (i, h)),
        out_shape=jax.ShapeDtypeStruct((s, nh * dv), BF16),
        scratch_shapes=[pltpu.VMEM((hs, dq, MLA_TQ), BF16),
                        pltpu.VMEM((hs, MLA_TK, MLA_TQ), F32), pltpu.VMEM((hs, MLA_TK, MLA_TQ), F32),
                        pltpu.VMEM((hs, 1, MLA_TQ), F32), pltpu.VMEM((hs, 1, MLA_TQ), F32),
                        pltpu.VMEM((hs, dv, MLA_TQ), F32)],
        compiler_params=pltpu.CompilerParams(
            dimension_semantics=("parallel", "arbitrary"), vmem_limit_bytes=VMEM_LIMIT),
        name="mla",
    )(q, k, vt)


NSA_TQ = 256
NSA_ROWS = NSA_GROUP * NSA_TQ
NSA_TK = 512
_SEL_HALF_KEYS = LANE * SEL_BLOCK
_WIN_TILES = WINDOW // NSA_TQ + 1
SEL_CLASS_BLOCKS = 32


def _nsa_kernel(q_ref, kc_ref, vct_ref, ovt_ref, ks_ref, vst_ref, *rest):
    kw_refs = rest[:_WIN_TILES]
    vwt_refs = rest[_WIN_TILES:2 * _WIN_TILES]
    (wbias_ref, qoh_ref, gt_ref, o_ref,
     qaug_ref, oc_ref, sa_ref, sb_ref, m_ref, l_ref, acc_ref) = rest[2 * _WIN_TILES:]
    heads = range(NSA_KV_HEADS)

    i = pl.program_id(0)
    q0 = i * NSA_TQ
    tq = q0 + (lax.broadcasted_iota(jnp.int32, (1, NSA_ROWS), 1) & (NSA_TQ - 1))
    tq1 = q0 + lax.broadcasted_iota(jnp.int32, (1, NSA_TQ), 1)

    qts = [q_ref[hk * NSA_GROUP:(hk + 1) * NSA_GROUP].reshape(NSA_ROWS, NSA_HEAD_DIM).astype(F32).T.astype(BF16)
           for hk in heads]
    q_ohs = [jnp.concatenate([qt, qoh_ref[...]], axis=0) for qt in qts]

    o_ws = []
    for hk in heads:
        kw = jnp.concatenate([r[hk] for r in kw_refs], axis=0)
        vwt = jnp.concatenate([r[hk] for r in vwt_refs], axis=1)
        s_w = _dot(jnp.concatenate([kw, wbias_ref[0]], axis=1), q_ohs[hk])
        e_w = jnp.exp2(s_w - jnp.max(s_w, axis=0, keepdims=True))
        o_ws.append(_dot(vwt, e_w.astype(BF16)) * (1.0 / jnp.sum(e_w, axis=0, keepdims=True)))

    def compress_and_select(nb):
        nc = nb * (SEL_BLOCK // CMP_STRIDE)
        cend = lax.broadcasted_iota(jnp.int32, (nc, NSA_TQ), 0) * CMP_STRIDE + (CMP_BLOCK - 1)
        cbias = jnp.where(cend <= tq1, 0.0, NEG_INF).astype(BF16)
        bj = lax.broadcasted_iota(jnp.int32, (nb, NSA_TQ), 0)
        cur = tq1 // SEL_BLOCK
        valid = bj * SEL_BLOCK <= tq1
        bonus = FORCE_BONUS * jnp.where((bj == 0) | (bj == cur) | (bj == cur - 1), 1.0, 0.0)
        bjf = bj.astype(F32)
        ovt = ovt_ref[0:nb, 0:nc]
        for hk in heads:
            s_c = _dot(jnp.concatenate([kc_ref[hk, 0:nc, :], cbias], axis=1), q_ohs[hk])
            e_c = jnp.exp2(s_c - jnp.max(s_c, axis=0, keepdims=True))
            inv_c = jnp.where(tq >= CMP_BLOCK - 1, 1.0 / jnp.sum(e_c, axis=0, keepdims=True), 0.0)
            p_c = e_c * inv_c
            oc_ref[hk] = _dot(vct_ref[hk, :, 0:nc], p_c.astype(BF16))

            psum = p_c[:, 0:NSA_TQ]
            for g in range(1, NSA_GROUP):
                psum = psum + p_c[:, g * NSA_TQ:(g + 1) * NSA_TQ]
            p_hi = psum.astype(BF16)
            p_lo = (psum - p_hi.astype(F32)).astype(BF16)
            imp = _dot(ovt, p_hi) + _dot(ovt, p_lo)

            score = jnp.where(valid, imp + bonus, NEG_INF)
            work = score
            for _ in range(SEL_TOPK):
                mx = jnp.max(work, axis=0, keepdims=True)
                first = jnp.min(jnp.where(work == mx, bjf, float(N_SEL)), axis=0, keepdims=True)
                work = jnp.where(bjf == first, TOPK_TAKEN, work)
            selected = (work < 0.5 * TOPK_TAKEN) & (score > 0.5 * NEG_INF)
            bias = jnp.where(selected, 0.0, MASK_BIAS)
            if nb < N_SEL:
                bias = jnp.concatenate([bias, jnp.full((N_SEL - nb, NSA_TQ), MASK_BIAS, F32)], axis=0)
            bias = jnp.concatenate([bias.astype(BF16)] * NSA_GROUP, axis=1)
            for half in range(N_SEL // LANE):
                qaug_ref[hk, half, 0:NSA_HEAD_DIM, :] = qts[hk]
                qaug_ref[hk, half, NSA_HEAD_DIM:NSA_HEAD_DIM + LANE, :] = bias[half * LANE:(half + 1) * LANE, :]

    size_class = (q0 + NSA_TQ - 1) // (SEL_BLOCK * SEL_CLASS_BLOCKS)
    for c in range(N_SEL // SEL_CLASS_BLOCKS):
        pl.when(size_class == c)(functools.partial(compress_and_select, (c + 1) * SEL_CLASS_BLOCKS))

    last = (q0 + NSA_TQ - 1) // NSA_TK

    def causal_mask(kt):
        return kt * NSA_TK + lax.broadcasted_iota(jnp.int32, (NSA_TK, 1), 0) <= tq

    def chain(hk):
        def scores(kt):
            k0 = pl.multiple_of(kt * NSA_TK, NSA_TK)
            qa = qaug_ref[hk, kt // (_SEL_HALF_KEYS // NSA_TK)]
            return _dot(ks_ref[hk, pl.ds(k0, NSA_TK), :], qa)

        def values(kt):
            return vst_ref[hk, :, pl.ds(pl.multiple_of(kt * NSA_TK, NSA_TK), NSA_TK)]

        return _FlashChain(scores, values, causal_mask,
                           sa_ref.at[hk], sb_ref.at[hk], m_ref.at[hk], l_ref.at[hk], acc_ref.at[hk])

    _flash_causal([chain(hk) for hk in heads], last, last, False)

    for hk in heads:
        o_s = acc_ref[hk] * (1.0 / l_ref[hk])
        gt = gt_ref[hk]
        gate = lambda br: jnp.concatenate(
            [gt[br * NSA_GROUP + g:br * NSA_GROUP + g + 1, :] for g in range(NSA_GROUP)], axis=1)
        out = (gate(0) * oc_ref[hk] + gate(1) * o_s + gate(2) * o_ws[hk]).T
        for g in range(NSA_GROUP):
            col = (hk * NSA_GROUP + g) * NSA_HEAD_DIM
            o_ref[:, col:col + NSA_HEAD_DIM] = out[g * NSA_TQ:(g + 1) * NSA_TQ].astype(o_ref.dtype)


def _nsa(q, kc, vct, ovt, ks, vst, kw, vwt, gates):
    s = q.shape[1]
    d = NSA_HEAD_DIM
    hk = NSA_KV_HEADS
    assert NSA_TQ <= NSA_TK and NSA_TK % NSA_TQ == 0

    def win_specs(transposed):
        def spec(j):
            back = _WIN_TILES - 1 - j
            if transposed:
                return pl.BlockSpec((hk, d, NSA_TQ), lambda i: (0, 0, jnp.maximum(i - back, 0)))
            return pl.BlockSpec((hk, NSA_TQ, d), lambda i: (0, jnp.maximum(i - back, 0), 0))
        return [spec(j) for j in range(_WIN_TILES)]

    resident = pl.Buffered(1)
    return pl.pallas_call(
        _nsa_kernel,
        grid=(s // NSA_TQ,),
        in_specs=[
            pl.BlockSpec((NSA_HEADS, NSA_TQ, d), lambda i: (0, i, 0)),
            pl.BlockSpec((hk, N_CMP_PAD, d), lambda i: (0, 0, 0)),
            pl.BlockSpec((hk, d, N_CMP_PAD), lambda i: (0, 0, 0)),
            pl.BlockSpec((N_SEL, N_CMP_PAD), lambda i: (0, 0)),
            pl.BlockSpec((hk, s, 2 * LANE), lambda i: (0, 0, 0), pipeline_mode=resident),
            pl.BlockSpec((hk, d, s), lambda i: (0, 0, 0), pipeline_mode=resident),
            *win_specs(False), *win_specs(True),
            pl.BlockSpec((1, WINDOW + NSA_TQ, NSA_TQ), lambda i: (jnp.minimum(i, _WIN_TILES - 1), 0, 0)),
            pl.BlockSpec((NSA_TQ, NSA_ROWS), lambda i: (0, 0)),
            pl.BlockSpec((hk, GATE_ROWS, NSA_TQ), lambda i: (0, 0, i)),
        ],
        out_specs=pl.BlockSpec((NSA_TQ, NSA_HEADS * d), lambda i: (i, 0)),
        out_shape=jax.ShapeDtypeStruct((s, NSA_HEADS * d), BF16),
        scratch_shapes=[pltpu.VMEM((hk, N_SEL // LANE, NSA_HEAD_DIM + LANE, NSA_ROWS), BF16),
                        pltpu.VMEM((hk, d, NSA_ROWS), F32),
                        pltpu.VMEM((hk, NSA_TK, NSA_ROWS), F32), pltpu.VMEM((hk, NSA_TK, NSA_ROWS), F32),
                        pltpu.VMEM((hk, 1, NSA_ROWS), F32), pltpu.VMEM((hk, 1, NSA_ROWS), F32),
                        pltpu.VMEM((hk, d, NSA_ROWS), F32)],
        compiler_params=pltpu.CompilerParams(
            dimension_semantics=("arbitrary",), vmem_limit_bytes=VMEM_LIMIT),
        name="nsa",
    )(q, kc, vct, ovt, ks, vst, *([kw] * _WIN_TILES), *([vwt] * _WIN_TILES), _window_bias(), _query_onehot(), gates)


def _window_bias():
    r = np.arange(WINDOW + NSA_TQ)[None, :, None]
    c = np.arange(NSA_TQ)[None, None, :]
    q0 = (np.arange(_WIN_TILES) * NSA_TQ)[:, None, None]
    back = WINDOW + c - r
    ok = (back >= 0) & (back < WINDOW) & ((q0 - WINDOW + r >= 0) | (q0 >= WINDOW))
    return jnp.asarray(np.where(ok, 0.0, NEG_INF), dtype=BF16)


def _query_onehot():
    return jnp.asarray(np.tile(np.eye(NSA_TQ, dtype=np.float32), (1, NSA_GROUP)), dtype=BF16)


OUT_TM = 512


def _out_ln_kernel(om_ref, on_ref, wa_ref, wb_ref, x_ref, g_ref, b_ref, o_ref):
    mix = _dot(om_ref[...], wa_ref[...]) + _dot(on_ref[...], wb_ref[...])
    o_ref[...] = _layer_norm(DEEPNORM_ALPHA * x_ref[...] + mix, g_ref[...], b_ref[...])


def _out_ln(o_mla, o_nsa, wa, wb, x1, g, b):
    s, d = x1.shape
    tm = OUT_TM
    return pl.pallas_call(
        _out_ln_kernel,
        grid=(s // tm,),
        in_specs=[
            pl.BlockSpec((tm, o_mla.shape[1]), lambda i: (i, 0)),
            pl.BlockSpec((tm, o_nsa.shape[1]), lambda i: (i, 0)),
            _const_spec(wa.shape), _const_spec(wb.shape),
            pl.BlockSpec((tm, d), lambda i: (i, 0)),
            _const_spec(g.shape), _const_spec(b.shape),
        ],
        out_specs=pl.BlockSpec((tm, d), lambda i: (i, 0)),
        out_shape=jax.ShapeDtypeStruct((s, d), F32),
        compiler_params=pltpu.CompilerParams(
            dimension_semantics=("parallel",), vmem_limit_bytes=VMEM_LIMIT),
        name="out_ln",
    )(o_mla, o_nsa, wa, wb, x1, g, b)


def _pad_cols(w, n):
    return jnp.pad(w, ((0, 0), (0, n - w.shape[1])))


def _relayout_w_in(w_in, gate_b):
    sizes = (MLA_Q_RANK, MLA_KV_RANK, MLA_ROPE_DIM, NSA_HEADS * NSA_HEAD_DIM) + (NSA_KV_HEADS * NSA_HEAD_DIM,) * 6 \
        + (N_BRANCH * NSA_HEADS,)
    offs = [int(v) for v in np.cumsum(sizes)[:-1]]
    c_q, c_kv, k_r, q_n, k_c, v_c, k_s, v_s, k_w, v_w, g = jnp.split(w_in, offs, axis=1)
    half = MLA_ROPE_DIM // 2
    k_r_sw = jnp.concatenate([k_r[:, half:], k_r[:, :half]], axis=1)
    gate_idx = np.array([[br * NSA_HEADS + hk * NSA_GROUP + gi for br in range(N_BRANCH) for gi in range(NSA_GROUP)]
                         for hk in range(NSA_KV_HEADS)])
    n_gate = gate_idx.shape[1]
    chunk = lambda rope, hk: _pad_cols(jnp.concatenate([rope, g[:, gate_idx[hk]]], axis=1), LANE)
    w = jnp.concatenate([c_q, c_kv, q_n, k_c, v_c, k_s, v_s, k_w, v_w, chunk(k_r, 0), chunk(k_r_sw, 1)], axis=1)
    assert w.shape[1] == _PROJ_COLS and k_r.shape[1] == GATE_LANE0
    gb = jnp.stack([jnp.pad(gate_b[gate_idx[hk]], (GATE_LANE0, LANE - GATE_LANE0 - n_gate))
                    for hk in range(NSA_KV_HEADS)])
    return w.astype(BF16), gb.reshape(NSA_KV_HEADS, 1, LANE)


def _relayout_w_uq(w_uq):
    dh = MLA_NOPE_DIM + MLA_ROPE_DIM
    half = MLA_ROPE_DIM // 2
    w = w_uq.reshape(MLA_Q_RANK, MLA_HEADS, dh)
    nope = w[:, :, :MLA_NOPE_DIM]
    r = w[:, :, MLA_NOPE_DIM:]
    r_sw = jnp.concatenate([r[:, :, half:], r[:, :, :half]], axis=2)
    pad = lambda t: jnp.pad(t, ((0, 0), (0, 0), (0, LANE - MLA_ROPE_DIM)))
    parts = [nope, pad(r), pad(r_sw)]
    return jnp.concatenate([p.reshape(MLA_Q_RANK, MLA_HEADS * LANE) for p in parts], axis=1).astype(BF16)


def _relayout_w_ukv(w_ukv):
    w = w_ukv.reshape(MLA_KV_RANK, MLA_HEADS, MLA_NOPE_DIM + MLA_V_DIM)
    k = w[:, :, :MLA_NOPE_DIM].reshape(MLA_KV_RANK, MLA_HEADS * MLA_NOPE_DIM)
    v = w[:, :, MLA_NOPE_DIM:].reshape(MLA_KV_RANK, MLA_HEADS * MLA_V_DIM)
    return jnp.concatenate([k, v], axis=1).astype(BF16)


def _rope_tables(s):
    pos = jnp.arange(s).astype(F32)[:, None]

    def cs(d):
        half = d // 2
        inv = ROPE_THETA ** (-jnp.arange(half, dtype=F32) * (2.0 / d))
        ang = pos * inv[None, :]
        return jnp.cos(ang), jnp.sin(ang)

    cm, sm = cs(MLA_ROPE_DIM)
    cn, sn = cs(NSA_HEAD_DIM)
    cosm = _pad_cols(jnp.concatenate([cm, cm], axis=1), LANE)
    sinm = _pad_cols(jnp.concatenate([-sm, sm], axis=1), LANE)
    cosn = jnp.concatenate([cn, cn], axis=1)
    sinn = jnp.concatenate([-sn, sn], axis=1)
    return cosm, sinm, cosn, sinn


def _overlap_weights():
    ci = np.arange(N_CMP_PAD)[:, None] * CMP_STRIDE
    sj = np.arange(N_SEL)[None, :] * SEL_BLOCK
    ov = np.clip(np.minimum(ci + CMP_BLOCK, sj + SEL_BLOCK) - np.maximum(ci, sj), 0, None)
    ov[N_CMP:] = 0
    return jnp.asarray((ov.astype(np.float32) / CMP_STRIDE).T, dtype=BF16)


def kernel(x, ffn1_w_gate, ffn1_w_up, ffn1_w_down, ln1_g, ln1_b, w_in, mla_q_norm_g, mla_w_uq, mla_kv_norm_g,
           mla_w_ukv, nsa_gate_b, nsa_cmp_pe_k, nsa_cmp_w1_k, nsa_cmp_w2_k, nsa_cmp_pe_v, nsa_cmp_w1_v,
           nsa_cmp_w2_v, w_out, ln2_g, ln2_b, ffn2_w_gate, ffn2_w_up, ffn2_w_down, ln3_g, ln3_b):
    b, s, d = x.shape
    assert (b, s, d) == (1, SEQ, D_MODEL) and ffn1_w_gate.shape[0] == DEPTH
    l = 0
    row = lambda v: v.reshape(1, -1)
    xs = x.reshape(s, d)

    x1 = _ffn_ln(xs, ffn1_w_gate[l].astype(BF16), ffn1_w_up[l].astype(BF16), ffn1_w_down[l].astype(BF16),
                 row(ln1_g[l]), row(ln1_b[l]))

    win_p, gb_p = _relayout_w_in(w_in[l], nsa_gate_b[l])
    cosm, sinm, cosn, sinn = _rope_tables(s)
    (q_m, k_m, v_m, q_n, kc_in, vc_in, ks, vs, kw, vw, gates) = _proj(
        x1, win_p, row(mla_q_norm_g[l]), row(mla_kv_norm_g[l]),
        _relayout_w_uq(mla_w_uq[l]), _relayout_w_ukv(mla_w_ukv[l]), gb_p, cosm, sinm, cosn, sinn)

    def compress(t16, pe, w1, w2, transpose_out):
        pe8 = jnp.broadcast_to(pe.reshape(1, CMP_BLOCK * NSA_HEAD_DIM), (8, CMP_BLOCK * NSA_HEAD_DIM))
        return _compress(t16, pe8, w1, w2, transpose_out)

    kc = compress(kc_in, nsa_cmp_pe_k[l], nsa_cmp_w1_k[l], nsa_cmp_w2_k[l], False)
    vc = compress(vc_in, nsa_cmp_pe_v[l], nsa_cmp_w1_v[l], nsa_cmp_w2_v[l], True)

    o_mla = _mla(q_m, k_m, v_m)
    o_nsa = _nsa(q_n, kc, vc, _overlap_weights(), ks, vs, kw, vw, gates)

    n_mla = MLA_HEADS * MLA_V_DIM
    w_o = w_out[l].astype(BF16)
    x2 = _out_ln(o_mla, o_nsa, w_o[:n_mla], w_o[n_mla:], x1, row(ln2_g[l]), row(ln2_b[l]))

    x3 = _ffn_ln(x2, ffn2_w_gate[l].astype(BF16), ffn2_w_up[l].astype(BF16), ffn2_w_down[l].astype(BF16),
                 row(ln3_g[l]), row(ln3_b[l]))
    return x3.reshape(b, s, d)
```

```python
import functools

import jax
import jax.numpy as jnp
import numpy as np
from jax import lax
from jax.experimental import pallas as pl
from jax.experimental.pallas import tpu as pltpu

F32 = jnp.float32
BF16 = jnp.bfloat16

D_MODEL = 2048
SEQ = 16384
DEPTH = 1
DEEPNORM_ALPHA = (2.0 * DEPTH) ** 0.25
LN_EPS = 1e-5
RMS_EPS = 1e-6
ROPE_THETA = 10000.0
NEG_INF = -1e30
D_FF = 5632
FFN_RES_WEIGHT = 0.5

MLA_HEADS = 8
MLA_Q_RANK = 512
MLA_KV_RANK = 256
MLA_NOPE_DIM = 128
MLA_ROPE_DIM = 64
MLA_V_DIM = 128
MLA_QK_PAD = 256

NSA_HEADS = 8
NSA_KV_HEADS = 2
NSA_HEAD_DIM = 128
NSA_GROUP = NSA_HEADS // NSA_KV_HEADS
CMP_BLOCK = 32
CMP_STRIDE = 16
CMP_HIDDEN = 256
SEL_BLOCK = 64
SEL_TOPK = 16
WINDOW = 512
N_BRANCH = 3
FORCE_BONUS = 1e4
N_CMP = (SEQ - CMP_BLOCK) // CMP_STRIDE + 1
N_CMP_PAD = SEQ // CMP_STRIDE
N_SEL = SEQ // SEL_BLOCK

LANE = 128
MASK_BIAS = NEG_INF
TOPK_TAKEN = -2.0 ** 127
M_INIT = -1e30

VMEM_LIMIT = 56 * 1024 * 1024


def _dot(a, b, **kw):
    return jnp.dot(a, b, preferred_element_type=F32, **kw)


def _layer_norm(y, g, b):
    mu = jnp.mean(y, axis=-1, keepdims=True)
    yc = y - mu
    var = jnp.mean(yc * yc, axis=-1, keepdims=True)
    return yc * lax.rsqrt(var + LN_EPS) * g + b


def _rms_norm(x, g):
    ms = jnp.mean(x * x, axis=-1, keepdims=True)
    return x * lax.rsqrt(ms + RMS_EPS) * g


FFN_TM = 512
FFN_TF = 512


def _ffn_ln_kernel(x_ref, wg_ref, wu_ref, wd_ref, g_ref, b_ref, o_ref, xb_ref):
    k = pl.program_id(1)

    def down_partial():
        gate = _dot(xb_ref[...], wg_ref[...])
        up = _dot(xb_ref[...], wu_ref[...])
        return _dot((jax.nn.silu(gate) * up).astype(BF16), wd_ref[...])

    @pl.when(k == 0)
    def _():
        xb_ref[...] = x_ref[...].astype(BF16)
        o_ref[...] = down_partial()

    @pl.when(k > 0)
    def _():
        o_ref[...] += down_partial()

    @pl.when(k == pl.num_programs(1) - 1)
    def _():
        y = DEEPNORM_ALPHA * x_ref[...] + FFN_RES_WEIGHT * o_ref[...]
        o_ref[...] = _layer_norm(y, g_ref[...], b_ref[...])


def _ffn_ln(x, wg, wu, wd, g, b):
    s, d = x.shape
    f = wg.shape[1]
    grid = (s // FFN_TM, f // FFN_TF)
    return pl.pallas_call(
        _ffn_ln_kernel,
        grid=grid,
        in_specs=[
            pl.BlockSpec((FFN_TM, d), lambda i, k: (i, 0)),
            pl.BlockSpec((d, FFN_TF), lambda i, k: (0, k)),
            pl.BlockSpec((d, FFN_TF), lambda i, k: (0, k)),
            pl.BlockSpec((FFN_TF, d), lambda i, k: (k, 0)),
            pl.BlockSpec((1, d), lambda i, k: (0, 0)),
            pl.BlockSpec((1, d), lambda i, k: (0, 0)),
        ],
        out_specs=pl.BlockSpec((FFN_TM, d), lambda i, k: (i, 0)),
        out_shape=jax.ShapeDtypeStruct((s, d), F32),
        scratch_shapes=[pltpu.VMEM((FFN_TM, d), BF16)],
        compiler_params=pltpu.CompilerParams(
            dimension_semantics=("parallel", "arbitrary"), vmem_limit_bytes=VMEM_LIMIT),
        name="ffn_ln",
    )(x, wg, wu, wd, g, b)


PROJ_TM = 256
_O_CQ = 0
_O_CKV = _O_CQ + MLA_Q_RANK
_O_NQ = _O_CKV + MLA_KV_RANK
_O_KC = _O_NQ + NSA_HEADS * NSA_HEAD_DIM
_O_VC = _O_KC + NSA_KV_HEADS * NSA_HEAD_DIM
_O_KS = _O_VC + NSA_KV_HEADS * NSA_HEAD_DIM
_O_VS = _O_KS + NSA_KV_HEADS * NSA_HEAD_DIM
_O_KW = _O_VS + NSA_KV_HEADS * NSA_HEAD_DIM
_O_VW = _O_KW + NSA_KV_HEADS * NSA_HEAD_DIM
_O_KR = _O_VW + NSA_KV_HEADS * NSA_HEAD_DIM
_O_KRS = _O_KR + LANE
_O_G = (_O_KR, _O_KRS)
_PROJ_COLS = _O_KRS + LANE
GATE_ROWS = 16
GATE_LANE0 = MLA_ROPE_DIM
assert NSA_KV_HEADS == 2 and GATE_LANE0 % 8 == 0 and GATE_LANE0 + GATE_ROWS <= LANE

LOG2E = 1.4426950408889634
MLA_SCALE = (MLA_NOPE_DIM + MLA_ROPE_DIM) ** -0.5 * LOG2E
NSA_SCALE = NSA_HEAD_DIM ** -0.5 * LOG2E


def _proj_kernel(x_ref, win_ref, gq_ref, gkv_ref, wuq_ref, wukv_ref, gb_ref,
                 cosm_ref, sinm_ref, cosn_ref, sinn_ref,
                 qm_ref, km_ref, vm_ref, qn_ref, kc_ref, vc_ref, ks_ref, vs_ref, kw_ref, vw_ref, gt_ref,
                 cmp_ref):
    tm = x_ref.shape[0]
    xb = x_ref[...].astype(BF16)
    h = _dot(xb, win_ref[...])
    cosm, sinm = cosm_ref[...], sinm_ref[...]
    cosn, sinn = cosn_ref[...], sinn_ref[...]

    cqn = _rms_norm(h[:, _O_CQ:_O_CQ + MLA_Q_RANK], gq_ref[...]).astype(BF16)
    qall = _dot(cqn, wuq_ref[...])
    hw = MLA_HEADS * LANE
    for hd in range(MLA_HEADS):
        sl = slice(hd * LANE, (hd + 1) * LANE)
        qm_ref[hd, :, 0:LANE] = (qall[:, sl] * MLA_SCALE).astype(BF16)
        qr = qall[:, hw + hd * LANE: hw + (hd + 1) * LANE]
        qrs = qall[:, 2 * hw + hd * LANE: 2 * hw + (hd + 1) * LANE]
        qm_ref[hd, :, LANE:2 * LANE] = ((qr * cosm + qrs * sinm) * MLA_SCALE).astype(BF16)

    ckvn = _rms_norm(h[:, _O_CKV:_O_CKV + MLA_KV_RANK], gkv_ref[...]).astype(BF16)
    kv = _dot(ckvn, wukv_ref[...])
    kr = (h[:, _O_KR:_O_KR + LANE] * cosm + h[:, _O_KRS:_O_KRS + LANE] * sinm).astype(BF16)
    for hd in range(MLA_HEADS):
        km_ref[hd, :, 0:LANE] = kv[:, hd * LANE:(hd + 1) * LANE].astype(BF16)
        km_ref[hd, :, LANE:2 * LANE] = kr
        vm_ref[hd] = kv[:, hw + hd * LANE: hw + (hd + 1) * LANE].T.astype(BF16)

    def rope128(t):
        return t * cosn + pltpu.roll(t, NSA_HEAD_DIM // 2, 1) * sinn

    for hd in range(NSA_HEADS):
        t = h[:, _O_NQ + hd * LANE:_O_NQ + (hd + 1) * LANE]
        qn_ref[hd] = (rope128(t) * NSA_SCALE).astype(BF16)

    row0 = pl.program_id(0) * tm
    kpos = row0 + lax.broadcasted_iota(jnp.int32, (tm, LANE), 0)
    lane = lax.broadcasted_iota(jnp.int32, (tm, LANE), 1)
    onehot = jnp.where(((kpos // SEL_BLOCK) % LANE) == lane, 1.0, 0.0).astype(BF16)
    for hk in range(NSA_KV_HEADS):
        sl = lambda off: slice(off + hk * LANE, off + (hk + 1) * LANE)
        for out_ref, t in ((kc_ref, rope128(h[:, sl(_O_KC)])), (vc_ref, h[:, sl(_O_VC)])):
            cmp_ref[...] = t
            for tok in range(CMP_STRIDE):
                out_ref[hk, :, tok * LANE:(tok + 1) * LANE] = cmp_ref[pl.ds(tok, tm // CMP_STRIDE, stride=CMP_STRIDE), :]
        ks_ref[hk, :, 0:LANE] = rope128(h[:, sl(_O_KS)]).astype(BF16)
        ks_ref[hk, :, LANE:2 * LANE] = onehot
        vs_ref[hk] = h[:, sl(_O_VS)].T.astype(BF16)
        kw_ref[hk] = rope128(h[:, sl(_O_KW)]).astype(BF16)
        vw_ref[hk] = h[:, sl(_O_VW)].T.astype(BF16)
        gate = jax.nn.sigmoid(h[:, _O_G[hk]:_O_G[hk] + LANE] + gb_ref[hk])
        gt_ref[hk] = gate.T[GATE_LANE0:GATE_LANE0 + GATE_ROWS]


def _const_spec(shape):
    nd = len(shape)
    return pl.BlockSpec(shape, lambda i: (0,) * nd)


def _proj(x1, win_p, gq, gkv, wuq_all, wukv_p, gb_p, cosm, sinm, cosn, sinn):
    s, d = x1.shape
    tm = PROJ_TM
    grid = (s // tm,)
    row_spec = lambda w: pl.BlockSpec((tm, w), lambda i: (i, 0))
    head_spec = lambda nh, w: pl.BlockSpec((nh, tm, w), lambda i: (0, i, 0))
    out_shape = (
        jax.ShapeDtypeStruct((MLA_HEADS, s, MLA_QK_PAD), BF16),
        jax.ShapeDtypeStruct((MLA_HEADS, s, MLA_QK_PAD), BF16),
        jax.ShapeDtypeStruct((MLA_HEADS, MLA_V_DIM, s), BF16),
        jax.ShapeDtypeStruct((NSA_HEADS, s, NSA_HEAD_DIM), BF16),
        jax.ShapeDtypeStruct((NSA_KV_HEADS, s // CMP_STRIDE, CMP_STRIDE * NSA_HEAD_DIM), F32),
        jax.ShapeDtypeStruct((NSA_KV_HEADS, s // CMP_STRIDE, CMP_STRIDE * NSA_HEAD_DIM), F32),
        jax.ShapeDtypeStruct((NSA_KV_HEADS, s, 2 * LANE), BF16),
        jax.ShapeDtypeStruct((NSA_KV_HEADS, NSA_HEAD_DIM, s), BF16),
        jax.ShapeDtypeStruct((NSA_KV_HEADS, s, NSA_HEAD_DIM), BF16),
        jax.ShapeDtypeStruct((NSA_KV_HEADS, NSA_HEAD_DIM, s), BF16),
        jax.ShapeDtypeStruct((NSA_KV_HEADS, GATE_ROWS, s), F32),
    )
    head_spec_t = lambda nh, r: pl.BlockSpec((nh, r, tm), lambda i: (0, 0, i))
    cmp_spec = pl.BlockSpec((NSA_KV_HEADS, tm // CMP_STRIDE, CMP_STRIDE * NSA_HEAD_DIM), lambda i: (0, i, 0))
    out_specs = (
        head_spec(MLA_HEADS, MLA_QK_PAD), head_spec(MLA_HEADS, MLA_QK_PAD), head_spec_t(MLA_HEADS, MLA_V_DIM),
        head_spec(NSA_HEADS, NSA_HEAD_DIM),
        cmp_spec, cmp_spec,
        head_spec(NSA_KV_HEADS, 2 * LANE), head_spec_t(NSA_KV_HEADS, NSA_HEAD_DIM),
        head_spec(NSA_KV_HEADS, NSA_HEAD_DIM), head_spec_t(NSA_KV_HEADS, NSA_HEAD_DIM),
        head_spec_t(NSA_KV_HEADS, GATE_ROWS),
    )
    return pl.pallas_call(
        _proj_kernel,
        grid=grid,
        in_specs=[
            row_spec(d),
            _const_spec(win_p.shape), _const_spec(gq.shape), _const_spec(gkv.shape),
            _const_spec(wuq_all.shape), _const_spec(wukv_p.shape), _const_spec(gb_p.shape),
            row_spec(LANE), row_spec(LANE), row_spec(LANE), row_spec(LANE),
        ],
        out_specs=out_specs,
        out_shape=out_shape,
        scratch_shapes=[pltpu.VMEM((tm, NSA_HEAD_DIM), F32)],
        compiler_params=pltpu.CompilerParams(
            dimension_semantics=("parallel",), vmem_limit_bytes=VMEM_LIMIT),
        name="proj",
    )(x1, win_p, gq, gkv, wuq_all, wukv_p, gb_p, cosm, sinm, cosn, sinn)


_CMP_HALF = CMP_STRIDE * NSA_HEAD_DIM


def _compress_kernel(t_ref, pe_ref, w1_ref, w2_ref, o_ref, *, transpose_out):
    hi = lax.Precision.HIGHEST
    t = t_ref[0]
    w1a = w1_ref[0:_CMP_HALF, :]
    w1b = w1_ref[_CMP_HALF:2 * _CMP_HALF, :]
    first = _dot(t, w1a, precision=hi)
    second = _dot(t, w1b, precision=hi)
    pe_row = _dot(pe_ref[...], w1_ref[...], precision=hi)[0:1]
    hid = first + pltpu.roll(second, N_CMP_PAD - 1, 0) + pe_row
    out = _dot(jax.nn.gelu(hid), w2_ref[...], precision=hi)
    o_ref[0] = (out.T if transpose_out else out).astype(o_ref.dtype)


def _compress(t16, pe_flat8, w1, w2, transpose_out):
    out_dims = (NSA_HEAD_DIM, N_CMP_PAD) if transpose_out else (N_CMP_PAD, NSA_HEAD_DIM)
    return pl.pallas_call(
        functools.partial(_compress_kernel, transpose_out=transpose_out),
        grid=(NSA_KV_HEADS,),
        in_specs=[
            pl.BlockSpec((1, N_CMP_PAD, _CMP_HALF), lambda i: (i, 0, 0)),
            _const_spec(pe_flat8.shape), _const_spec(w1.shape), _const_spec(w2.shape),
        ],
        out_specs=pl.BlockSpec((1,) + out_dims, lambda i: (i, 0, 0)),
        out_shape=jax.ShapeDtypeStruct((NSA_KV_HEADS,) + out_dims, BF16),
        compiler_params=pltpu.CompilerParams(
            dimension_semantics=("parallel",), vmem_limit_bytes=VMEM_LIMIT),
        name="compress",
    )(t16, pe_flat8, w1, w2)


def _flash_init(m_ref, l_ref, acc_ref):
    m_ref[...] = jnp.full_like(m_ref, M_INIT)
    l_ref[...] = jnp.zeros_like(l_ref)
    acc_ref[...] = jnp.zeros_like(acc_ref)


def _flash_update(st, vt, m_ref, l_ref, acc_ref):
    m_prev = m_ref[...]
    m_new = jnp.maximum(m_prev, jnp.max(st, axis=0, keepdims=True))
    alpha = jnp.exp2(m_prev - m_new)
    p = jnp.exp2(st - m_new)
    l_ref[...] = alpha * l_ref[...] + jnp.sum(p, axis=0, keepdims=True)
    acc_ref[...] = alpha * acc_ref[...] + _dot(vt, p.astype(BF16))
    m_ref[...] = m_new


FLASH_PAIRS_PER_TRIP = 2


class _FlashChain:
    def __init__(self, scores, values, causal_mask, sa_ref, sb_ref, m_ref, l_ref, acc_ref):
        self.scores, self.values, self.causal_mask = scores, values, causal_mask
        self.sa, self.sb, self.m, self.l, self.acc = sa_ref, sb_ref, m_ref, l_ref, acc_ref

    def update(self, st, tile, refs=None):
        _flash_update(st, self.values(tile), *(refs or (self.m, self.l, self.acc)))

    def masked_update(self, tile, st):
        self.update(jnp.where(self.causal_mask(tile), st, MASK_BIAS), tile)


def _flash_causal(chains, n_plain, last_tile, two_diagonal_tiles):
    for c in chains:
        _flash_init(c.m, c.l, c.acc)
        c.sa[...] = c.scores(0)

    def plain_pair(t):
        for c in chains:
            c.sb[...] = c.scores(t + 1)
            c.update(c.sa[...], t)
            c.sa[...] = c.scores(t + 2)
            c.update(c.sb[...], t + 1)

    def body(j, carry):
        for u in range(FLASH_PAIRS_PER_TRIP):
            plain_pair(2 * (FLASH_PAIRS_PER_TRIP * j + u))
        return carry

    n_pairs = n_plain // 2
    n_trips = n_pairs // FLASH_PAIRS_PER_TRIP
    lax.fori_loop(0, n_trips, body, 0)
    for u in range(FLASH_PAIRS_PER_TRIP - 1):
        @pl.when(n_trips * FLASH_PAIRS_PER_TRIP + u < n_pairs)
        def _():
            plain_pair(2 * (n_trips * FLASH_PAIRS_PER_TRIP + u))
    t = 2 * n_pairs

    if two_diagonal_tiles:
        for c in chains:
            half = c.m.shape[-1] // 2
            c.masked_update(t, c.sa[...])
            st = jnp.where(c.causal_mask(t + 1, half), c.scores(t + 1, half), MASK_BIAS)
            c.update(st, t + 1, tuple(r.at[:, half:] for r in (c.m, c.l, c.acc)))
    else:
        for c in chains:
            c.masked_update(t, c.sa[...])

        @pl.when(t + 1 <= last_tile)
        def _():
            for c in chains:
                c.masked_update(t + 1, c.scores(t + 1))


def _masked_probs_t(st, mask):
    st = jnp.where(mask, st, NEG_INF)
    m = jnp.max(st, axis=0, keepdims=True)
    p = jnp.where(mask, jnp.exp2(st - m), 0.0)
    return p * (1.0 / jnp.maximum(jnp.sum(p, axis=0, keepdims=True), 1e-30))


MLA_TQ = 1024
MLA_TK = 512


MLA_HEADS_PER_STEP = 2


def _mla_kernel(q_ref, k_ref, vt_ref, o_ref, qt_ref, sa_ref, sb_ref, m_ref, l_ref, acc_ref):
    i = pl.program_id(1)
    qpos = i * MLA_TQ + lax.broadcasted_iota(jnp.int32, (1, MLA_TQ), 1)

    def causal_mask(kt, first_query=0):
        return kt * MLA_TK + lax.broadcasted_iota(jnp.int32, (MLA_TK, 1), 0) <= qpos[:, first_query:]

    def chain(h):
        qt_ref[h] = q_ref[h].astype(F32).T.astype(BF16)

        def scores(kt, first_query=0):
            k0 = pl.multiple_of(kt * MLA_TK, MLA_TK)
            return _dot(k_ref[h, pl.ds(k0, MLA_TK), :], qt_ref[h, :, first_query:])

        def values(kt):
            return vt_ref[h, :, pl.ds(pl.multiple_of(kt * MLA_TK, MLA_TK), MLA_TK)]

        return _FlashChain(scores, values, causal_mask,
                           sa_ref.at[h], sb_ref.at[h], m_ref.at[h], l_ref.at[h], acc_ref.at[h])

    tiles_per_q = MLA_TQ // MLA_TK
    _flash_causal([chain(h) for h in range(MLA_HEADS_PER_STEP)],
                  i * tiles_per_q, (i + 1) * tiles_per_q - 1, tiles_per_q == 2)
    dv = acc_ref.shape[1]
    for h in range(MLA_HEADS_PER_STEP):
        o_ref[:, h * dv:(h + 1) * dv] = (acc_ref[h] * (1.0 / l_ref[h])).T.astype(o_ref.dtype)


def _mla(q, k, vt):
    nh, s, dq = q.shape
    dv = vt.shape[1]
    hs = MLA_HEADS_PER_STEP
    assert MLA_TQ in (MLA_TK, 2 * MLA_TK) and s % MLA_TQ == 0 and nh % hs == 0
    resident = pl.Buffered(1)
    return pl.pallas_call(
        _mla_kernel,
        grid=(nh // hs, s // MLA_TQ),
        in_specs=[
            pl.BlockSpec((hs, MLA_TQ, dq), lambda h, i: (h, i, 0)),
            pl.BlockSpec((hs, s, dq), lambda h, i: (h, 0, 0), pipeline_mode=resident),
            pl.BlockSpec((hs, dv, s), lambda h, i: (h, 0, 0), pipeline_mode=resident),
        ],
        out_specs=pl.BlockSpec((MLA_TQ, hs * dv), lambda h, i: (i, h)),
        out_shape=jax.ShapeDtypeStruct((s, nh * dv), BF16),
        scratch_shapes=[pltpu.VMEM((hs, dq, MLA_TQ), BF16),
                        pltpu.VMEM((hs, MLA_TK, MLA_TQ), F32), pltpu.VMEM((hs, MLA_TK, MLA_TQ), F32),
                        pltpu.VMEM((hs, 1, MLA_TQ), F32), pltpu.VMEM((hs, 1, MLA_TQ), F32),
                        pltpu.VMEM((hs, dv, MLA_TQ), F32)],
        compiler_params=pltpu.CompilerParams(
            dimension_semantics=("parallel", "arbitrary"), vmem_limit_bytes=VMEM_LIMIT),
        name="mla",
    )(q, k, vt)


NSA_TQ = 256
NSA_ROWS = NSA_GROUP * NSA_TQ
NSA_TK = 512
_SEL_HALF_KEYS = LANE * SEL_BLOCK
_WIN_TILES = WINDOW // NSA_TQ + 1
SEL_CLASS_BLOCKS = 64


def _nsa_kernel(q_ref, kc_ref, vct_ref, ovt_ref, ks_ref, vst_ref, *rest):
    kw_refs = rest[:_WIN_TILES]
    vwt_refs = rest[_WIN_TILES:2 * _WIN_TILES]
    (wbias_ref, qoh_ref, gt_ref, o_ref,
     qaug_ref, oc_ref, sa_ref, sb_ref, m_ref, l_ref, acc_ref) = rest[2 * _WIN_TILES:]
    heads = range(NSA_KV_HEADS)

    i = pl.program_id(0)
    q0 = i * NSA_TQ
    tq = q0 + (lax.broadcasted_iota(jnp.int32, (1, NSA_ROWS), 1) & (NSA_TQ - 1))
    tq1 = q0 + lax.broadcasted_iota(jnp.int32, (1, NSA_TQ), 1)

    qts = [q_ref[hk * NSA_GROUP:(hk + 1) * NSA_GROUP].reshape(NSA_ROWS, NSA_HEAD_DIM).astype(F32).T.astype(BF16)
           for hk in heads]
    q_ohs = [jnp.concatenate([qt, qoh_ref[...]], axis=0) for qt in qts]

    o_ws = []
    for hk in heads:
        kw = jnp.concatenate([r[hk] for r in kw_refs], axis=0)
        vwt = jnp.concatenate([r[hk] for r in vwt_refs], axis=1)
        s_w = _dot(jnp.concatenate([kw, wbias_ref[0]], axis=1), q_ohs[hk])
        e_w = jnp.exp2(s_w - jnp.max(s_w, axis=0, keepdims=True))
        o_ws.append(_dot(vwt, e_w.astype(BF16)) * (1.0 / jnp.sum(e_w, axis=0, keepdims=True)))

    def compress_and_select(nb):
        nc = nb * (SEL_BLOCK // CMP_STRIDE)
        cend = lax.broadcasted_iota(jnp.int32, (nc, NSA_TQ), 0) * CMP_STRIDE + (CMP_BLOCK - 1)
        cbias = jnp.where(cend <= tq1, 0.0, NEG_INF).astype(BF16)
        bj = lax.broadcasted_iota(jnp.int32, (nb, NSA_TQ), 0)
        cur = tq1 // SEL_BLOCK
        valid = bj * SEL_BLOCK <= tq1
        bonus = FORCE_BONUS * jnp.where((bj == 0) | (bj == cur) | (bj == cur - 1), 1.0, 0.0)
        bjf = bj.astype(F32)
        ovt = ovt_ref[0:nb, 0:nc]
        for hk in heads:
            s_c = _dot(jnp.concatenate([kc_ref[hk, 0:nc, :], cbias], axis=1), q_ohs[hk])
            e_c = jnp.exp2(s_c - jnp.max(s_c, axis=0, keepdims=True))
            inv_c = jnp.where(tq >= CMP_BLOCK - 1, 1.0 / jnp.sum(e_c, axis=0, keepdims=True), 0.0)
            p_c = e_c * inv_c
            oc_ref[hk] = _dot(vct_ref[hk, :, 0:nc], p_c.astype(BF16))

            psum = p_c[:, 0:NSA_TQ]
            for g in range(1, NSA_GROUP):
                psum = psum + p_c[:, g * NSA_TQ:(g + 1) * NSA_TQ]
            p_hi = psum.astype(BF16)
            p_lo = (psum - p_hi.astype(F32)).astype(BF16)
            imp = _dot(ovt, p_hi) + _dot(ovt, p_lo)

            score = jnp.where(valid, imp + bonus, NEG_INF)
            work = score
            for _ in range(SEL_TOPK):
                mx = jnp.max(work, axis=0, keepdims=True)
                first = jnp.min(jnp.where(work == mx, bjf, float(N_SEL)), axis=0, keepdims=True)
                work = jnp.where(bjf == first, TOPK_TAKEN, work)
            selected = (work < 0.5 * TOPK_TAKEN) & (score > 0.5 * NEG_INF)
            bias = jnp.where(selected, 0.0, MASK_BIAS)
            if nb < N_SEL:
                bias = jnp.concatenate([bias, jnp.full((N_SEL - nb, NSA_TQ), MASK_BIAS, F32)], axis=0)
            bias = jnp.concatenate([bias.astype(BF16)] * NSA_GROUP, axis=1)
            for half in range(N_SEL // LANE):
                qaug_ref[hk, half, 0:NSA_HEAD_DIM, :] = qts[hk]
                qaug_ref[hk, half, NSA_HEAD_DIM:NSA_HEAD_DIM + LANE, :] = bias[half * LANE:(half + 1) * LANE, :]

    size_class = (q0 + NSA_TQ - 1) // (SEL_BLOCK * SEL_CLASS_BLOCKS)
    for c in range(N_SEL // SEL_CLASS_BLOCKS):
        pl.when(size_class == c)(functools.partial(compress_and_select, (c + 1) * SEL_CLASS_BLOCKS))

    last = (q0 + NSA_TQ - 1) // NSA_TK

    def causal_mask(kt):
        return kt * NSA_TK + lax.broadcasted_iota(jnp.int32, (NSA_TK, 1), 0) <= tq

    def chain(hk):
        def scores(kt):
            k0 = pl.multiple_of(kt * NSA_TK, NSA_TK)
            qa = qaug_ref[hk, kt // (_SEL_HALF_KEYS // NSA_TK)]
            return _dot(ks_ref[hk, pl.ds(k0, NSA_TK), :], qa)

        def values(kt):
            return vst_ref[hk, :, pl.ds(pl.multiple_of(kt * NSA_TK, NSA_TK), NSA_TK)]

        return _FlashChain(scores, values, causal_mask,
                           sa_ref.at[hk], sb_ref.at[hk], m_ref.at[hk], l_ref.at[hk], acc_ref.at[hk])

    _flash_causal([chain(hk) for hk in heads], last, last, False)

    for hk in heads:
        o_s = acc_ref[hk] * (1.0 / l_ref[hk])
        gt = gt_ref[hk]
        gate = lambda br: jnp.concatenate(
            [gt[br * NSA_GROUP + g:br * NSA_GROUP + g + 1, :] for g in range(NSA_GROUP)], axis=1)
        out = (gate(0) * oc_ref[hk] + gate(1) * o_s + gate(2) * o_ws[hk]).T
        for g in range(NSA_GROUP):
            col = (hk * NSA_GROUP + g) * NSA_HEAD_DIM
            o_ref[:, col:col + NSA_HEAD_DIM] = out[g * NSA_TQ:(g + 1) * NSA_TQ].astype(o_ref.dtype)


def _nsa(q, kc, vct, ovt, ks, vst, kw, vwt, gates):
    s = q.shape[1]
    d = NSA_HEAD_DIM
    hk = NSA_KV_HEADS
    assert NSA_TQ <= NSA_TK and NSA_TK % NSA_TQ == 0

    def win_specs(transposed):
        def spec(j):
            back = _WIN_TILES - 1 - j
            if transposed:
                return pl.BlockSpec((hk, d, NSA_TQ), lambda i: (0, 0, jnp.maximum(i - back, 0)))
            return pl.BlockSpec((hk, NSA_TQ, d), lambda i: (0, jnp.maximum(i - back, 0), 0))
        return [spec(j) for j in range(_WIN_TILES)]

    resident = pl.Buffered(1)
    return pl.pallas_call(
        _nsa_kernel,
        grid=(s // NSA_TQ,),
        in_specs=[
            pl.BlockSpec((NSA_HEADS, NSA_TQ, d), lambda i: (0, i, 0)),
            pl.BlockSpec((hk, N_CMP_PAD, d), lambda i: (0, 0, 0)),
            pl.BlockSpec((hk, d, N_CMP_PAD), lambda i: (0, 0, 0)),
            pl.BlockSpec((N_SEL, N_CMP_PAD), lambda i: (0, 0)),
            pl.BlockSpec((hk, s, 2 * LANE), lambda i: (0, 0, 0), pipeline_mode=resident),
            pl.BlockSpec((hk, d, s), lambda i: (0, 0, 0), pipeline_mode=resident),
            *win_specs(False), *win_specs(True),
            pl.BlockSpec((1, WINDOW + NSA_TQ, NSA_TQ), lambda i: (jnp.minimum(i, _WIN_TILES - 1), 0, 0)),
            pl.BlockSpec((NSA_TQ, NSA_ROWS), lambda i: (0, 0)),
            pl.BlockSpec((hk, GATE_ROWS, NSA_TQ), lambda i: (0, 0, i)),
        ],
        out_specs=pl.BlockSpec((NSA_TQ, NSA_HEADS * d), lambda i: (i, 0)),
        out_shape=jax.ShapeDtypeStruct((s, NSA_HEADS * d), BF16),
        scratch_shapes=[pltpu.VMEM((hk, N_SEL // LANE, NSA_HEAD_DIM + LANE, NSA_ROWS), BF16),
                        pltpu.VMEM((hk, d, NSA_ROWS), F32),
                        pltpu.VMEM((hk, NSA_TK, NSA_ROWS), F32), pltpu.VMEM((hk, NSA_TK, NSA_ROWS), F32),
                        pltpu.VMEM((hk, 1, NSA_ROWS), F32), pltpu.VMEM((hk, 1, NSA_ROWS), F32),
                        pltpu.VMEM((hk, d, NSA_ROWS), F32)],
        compiler_params=pltpu.CompilerParams(
            dimension_semantics=("arbitrary",), vmem_limit_bytes=VMEM_LIMIT),
        name="nsa",
    )(q, kc, vct, ovt, ks, vst, *([kw] * _WIN_TILES), *([vwt] * _WIN_TILES), _window_bias(), _query_onehot(), gates)


def _window_bias():
    r = np.arange(WINDOW + NSA_TQ)[None, :, None]
    c = np.arange(NSA_TQ)[None, None, :]
    q0 = (np.arange(_WIN_TILES) * NSA_TQ)[:, None, None]
    back = WINDOW + c - r
    ok = (back >= 0) & (back < WINDOW) & ((q0 - WINDOW + r >= 0) | (q0 >= WINDOW))
    return jnp.asarray(np.where(ok, 0.0, NEG_INF), dtype=BF16)


def _query_onehot():
    return jnp.asarray(np.tile(np.eye(NSA_TQ, dtype=np.float32), (1, NSA_GROUP)), dtype=BF16)


OUT_TM = 512


def _out_ln_kernel(om_ref, on_ref, wa_ref, wb_ref, x_ref, g_ref, b_ref, o_ref):
    mix = _dot(om_ref[...], wa_ref[...]) + _dot(on_ref[...], wb_ref[...])
    o_ref[...] = _layer_norm(DEEPNORM_ALPHA * x_ref[...] + mix, g_ref[...], b_ref[...])


def _out_ln(o_mla, o_nsa, wa, wb, x1, g, b):
    s, d = x1.shape
    tm = OUT_TM
    return pl.pallas_call(
        _out_ln_kernel,
        grid=(s // tm,),
        in_specs=[
            pl.BlockSpec((tm, o_mla.shape[1]), lambda i: (i, 0)),
            pl.BlockSpec((tm, o_nsa.shape[1]), lambda i: (i, 0)),
            _const_spec(wa.shape), _const_spec(wb.shape),
            pl.BlockSpec((tm, d), lambda i: (i, 0)),
            _const_spec(g.shape), _const_spec(b.shape),
        ],
        out_specs=pl.BlockSpec((tm, d), lambda i: (i, 0)),
        out_shape=jax.ShapeDtypeStruct((s, d), F32),
        compiler_params=pltpu.CompilerParams(
            dimension_semantics=("parallel",), vmem_limit_bytes=VMEM_LIMIT),
        name="out_ln",
    )(o_mla, o_nsa, wa, wb, x1, g, b)


def _pad_cols(w, n):
    return jnp.pad(w, ((0, 0), (0, n - w.shape[1])))


def _relayout_w_in(w_in, gate_b):
    sizes = (MLA_Q_RANK, MLA_KV_RANK, MLA_ROPE_DIM, NSA_HEADS * NSA_HEAD_DIM) + (NSA_KV_HEADS * NSA_HEAD_DIM,) * 6 \
        + (N_BRANCH * NSA_HEADS,)
    offs = [int(v) for v in np.cumsum(sizes)[:-1]]
    c_q, c_kv, k_r, q_n, k_c, v_c, k_s, v_s, k_w, v_w, g = jnp.split(w_in, offs, axis=1)
    half = MLA_ROPE_DIM // 2
    k_r_sw = jnp.concatenate([k_r[:, half:], k_r[:, :half]], axis=1)
    gate_idx = np.array([[br * NSA_HEADS + hk * NSA_GROUP + gi for br in range(N_BRANCH) for gi in range(NSA_GROUP)]
                         for hk in range(NSA_KV_HEADS)])
    n_gate = gate_idx.shape[1]
    chunk = lambda rope, hk: _pad_cols(jnp.concatenate([rope, g[:, gate_idx[hk]]], axis=1), LANE)
    w = jnp.concatenate([c_q, c_kv, q_n, k_c, v_c, k_s, v_s, k_w, v_w, chunk(k_r, 0), chunk(k_r_sw, 1)], axis=1)
    assert w.shape[1] == _PROJ_COLS and k_r.shape[1] == GATE_LANE0
    gb = jnp.stack([jnp.pad(gate_b[gate_idx[hk]], (GATE_LANE0, LANE - GATE_LANE0 - n_gate))
                    for hk in range(NSA_KV_HEADS)])
    return w.astype(BF16), gb.reshape(NSA_KV_HEADS, 1, LANE)


def _relayout_w_uq(w_uq):
    dh = MLA_NOPE_DIM + MLA_ROPE_DIM
    half = MLA_ROPE_DIM // 2
    w = w_uq.reshape(MLA_Q_RANK, MLA_HEADS, dh)
    nope = w[:, :, :MLA_NOPE_DIM]
    r = w[:, :, MLA_NOPE_DIM:]
    r_sw = jnp.concatenate([r[:, :, half:], r[:, :, :half]], axis=2)
    pad = lambda t: jnp.pad(t, ((0, 0), (0, 0), (0, LANE - MLA_ROPE_DIM)))
    parts = [nope, pad(r), pad(r_sw)]
    return jnp.concatenate([p.reshape(MLA_Q_RANK, MLA_HEADS * LANE) for p in parts], axis=1).astype(BF16)


def _relayout_w_ukv(w_ukv):
    w = w_ukv.reshape(MLA_KV_RANK, MLA_HEADS, MLA_NOPE_DIM + MLA_V_DIM)
    k = w[:, :, :MLA_NOPE_DIM].reshape(MLA_KV_RANK, MLA_HEADS * MLA_NOPE_DIM)
    v = w[:, :, MLA_NOPE_DIM:].reshape(MLA_KV_RANK, MLA_HEADS * MLA_V_DIM)
    return jnp.concatenate([k, v], axis=1).astype(BF16)


def _rope_tables(s):
    pos = jnp.arange(s).astype(F32)[:, None]

    def cs(d):
        half = d // 2
        inv = ROPE_THETA ** (-jnp.arange(half, dtype=F32) * (2.0 / d))
        ang = pos * inv[None, :]
        return jnp.cos(ang), jnp.sin(ang)

    cm, sm = cs(MLA_ROPE_DIM)
    cn, sn = cs(NSA_HEAD_DIM)
    cosm = _pad_cols(jnp.concatenate([cm, cm], axis=1), LANE)
    sinm = _pad_cols(jnp.concatenate([-sm, sm], axis=1), LANE)
    cosn = jnp.concatenate([cn, cn], axis=1)
    sinn = jnp.concatenate([-sn, sn], axis=1)
    return cosm, sinm, cosn, sinn


def _overlap_weights():
    ci = np.arange(N_CMP_PAD)[:, None] * CMP_STRIDE
    sj = np.arange(N_SEL)[None, :] * SEL_BLOCK
    ov = np.clip(np.minimum(ci + CMP_BLOCK, sj + SEL_BLOCK) - np.maximum(ci, sj), 0, None)
    ov[N_CMP:] = 0
    return jnp.asarray((ov.astype(np.float32) / CMP_STRIDE).T, dtype=BF16)


def kernel(x, ffn1_w_gate, ffn1_w_up, ffn1_w_down, ln1_g, ln1_b, w_in, mla_q_norm_g, mla_w_uq, mla_kv_norm_g,
           mla_w_ukv, nsa_gate_b, nsa_cmp_pe_k, nsa_cmp_w1_k, nsa_cmp_w2_k, nsa_cmp_pe_v, nsa_cmp_w1_v,
           nsa_cmp_w2_v, w_out, ln2_g, ln2_b, ffn2_w_gate, ffn2_w_up, ffn2_w_down, ln3_g, ln3_b):
    b, s, d = x.shape
    assert (b, s, d) == (1, SEQ, D_MODEL) and ffn1_w_gate.shape[0] == DEPTH
    l = 0
    row = lambda v: v.reshape(1, -1)
    xs = x.reshape(s, d)

    x1 = _ffn_ln(xs, ffn1_w_gate[l].astype(BF16), ffn1_w_up[l].astype(BF16), ffn1_w_down[l].astype(BF16),
                 row(ln1_g[l]), row(ln1_b[l]))

    win_p, gb_p = _relayout_w_in(w_in[l], nsa_gate_b[l])
    cosm, sinm, cosn, sinn = _rope_tables(s)
    (q_m, k_m, v_m, q_n, kc_in, vc_in, ks, vs, kw, vw, gates) = _proj(
        x1, win_p, row(mla_q_norm_g[l]), row(mla_kv_norm_g[l]),
        _relayout_w_uq(mla_w_uq[l]), _relayout_w_ukv(mla_w_ukv[l]), gb_p, cosm, sinm, cosn, sinn)

    def compress(t16, pe, w1, w2, transpose_out):
        pe8 = jnp.broadcast_to(pe.reshape(1, CMP_BLOCK * NSA_HEAD_DIM), (8, CMP_BLOCK * NSA_HEAD_DIM))
        return _compress(t16, pe8, w1, w2, transpose_out)

    kc = compress(kc_in, nsa_cmp_pe_k[l], nsa_cmp_w1_k[l], nsa_cmp_w2_k[l], False)
    vc = compress(vc_in, nsa_cmp_pe_v[l], nsa_cmp_w1_v[l], nsa_cmp_w2_v[l], True)

    o_mla = _mla(q_m, k_m, v_m)
    o_nsa = _nsa(q_n, kc, vc, _overlap_weights(), ks, vs, kw, vw, gates)

    n_mla = MLA_HEADS * MLA_V_DIM
    w_o = w_out[l].astype(BF16)
    x2 = _out_ln(o_mla, o_nsa, w_o[:n_mla], w_o[n_mla:], x1, row(ln2_g[l]), row(ln2_b[l]))

    x3 = _ffn_ln(x2, ffn2_w_gate[l].astype(BF16), ffn2_w_up[l].astype(BF16), ffn2_w_down[l].astype(BF16),
                 row(ln3_g[l]), row(ln3_b[l]))
    return x3.reshape(b, s, d)
```
